```python
import math
import jax
import jax.numpy as jnp
from jax import lax
import numpy as np

D_MODEL = 2048
BATCH = 16
SEQ = 256
DEPTH = 2
DEC_BATCH = 8
DEC_SEQ = 4096
PAST_LEN = 256

GRID_W = 64
EPS = 1e-6
A_HEADS = 8
A_KV_HEADS = 2
A_HEAD_DIM = 128
A_GROUP = A_HEADS // A_KV_HEADS
A_BLOCK = 128
A_WINDOW = 128
ROPE_BASE = 10000.0
ROPE_FREQS = A_HEAD_DIM // 4
B_WIDTH = 1024
B_GROUPS = 4
B_CHUNK = 128
B_GDIM = B_WIDTH // B_GROUPS
C_INNER = 1024
C_HEAD_DIM = 64
C_HEADS = C_INNER // C_HEAD_DIM
C_GROUPS = 2
C_REP = C_HEADS // C_GROUPS
C_STATE = 64
C_CONV = 3
C_CHUNK = 128
C_XBC = C_INNER + 2 * C_GROUPS * C_STATE
N_BRANCH = 3
BRANCH_WIDTH = 1024
IN_SIZES = (A_HEADS * A_HEAD_DIM, A_KV_HEADS * A_HEAD_DIM, A_KV_HEADS * A_HEAD_DIM,
            B_WIDTH, B_WIDTH, C_INNER, C_XBC, 2 * C_HEADS, N_BRANCH * D_MODEL)
IN_WIDTH = sum(IN_SIZES)
IN_SPLIT = tuple(int(s) for s in np.cumsum(IN_SIZES)[:-1])
MOE_GROUPS = 4
MOE_PER_GROUP = 4
MOE_EXPERTS = MOE_GROUPS * MOE_PER_GROUP
MOE_TOPK = 2
MOE_HIDDEN = 512

kernel_name = 'hybrid_diffusion_trunk_step'


def rms_norm(x, g):
    xf = x.astype(jnp.float32)
    y = xf * lax.rsqrt(jnp.mean(xf * xf, axis=-1, keepdims=True) + EPS)
    return (y * g.astype(jnp.float32)).astype(x.dtype)


def adaln(cond, w, b):
    m = jax.nn.silu(cond) @ w + b
    return tuple(t[:, None, :] for t in jnp.split(m, 6, axis=-1))


def rope_tables(n_tokens):
    rows = n_tokens // GRID_W
    row = jnp.repeat(jnp.arange(rows), GRID_W).astype(jnp.float32)
    col = jnp.tile(jnp.arange(GRID_W), rows).astype(jnp.float32)
    inv = ROPE_BASE ** (-jnp.arange(ROPE_FREQS, dtype=jnp.float32) / ROPE_FREQS)
    ang = jnp.stack([row[:, None] * inv, col[:, None] * inv], axis=1)
    return jnp.cos(ang), jnp.sin(ang)


def apply_rope(x, cos, sin):
    shp = x.shape
    xf = x.astype(jnp.float32).reshape(shp[:-1] + (2, 2, ROPE_FREQS))
    x1, x2 = xf[..., 0, :], xf[..., 1, :]
    c = cos[None, :, None]
    s = sin[None, :, None]
    out = jnp.stack([x1 * c - x2 * s, x2 * c + x1 * s], axis=-2)
    return out.reshape(shp).astype(x.dtype)


def sink_attention(q, k, v, valid, sink):
    s = jnp.einsum('bqhgd,bkhd->bhgqk', q, k).astype(jnp.float32) * (A_HEAD_DIM ** -0.5)
    if valid is not None:
        s = jnp.where(valid, s, -jnp.inf)
    sk = jnp.broadcast_to(sink.astype(jnp.float32).reshape(1, A_KV_HEADS, A_GROUP, 1, 1),
                          s.shape[:-1] + (1,))
    p = jax.nn.softmax(jnp.concatenate([sk, s], axis=-1), axis=-1)[..., 1:]
    return jnp.einsum('bhgqk,bkhd->bqhgd', p.astype(v.dtype), v)


def context_attention(q, k, v, sink):
    bsz, n = q.shape[:2]
    nb = n // A_BLOCK
    qb = q.reshape(bsz, nb, A_BLOCK, A_KV_HEADS, A_GROUP, A_HEAD_DIM).swapaxes(0, 1)
    o = lax.map(lambda qi: sink_attention(qi, k, v, None, sink), qb)
    return o.swapaxes(0, 1).reshape(bsz, n, A_HEADS * A_HEAD_DIM)


def latent_attention(q, k, v, k_ctx, v_ctx, sink):
    bsz, n = q.shape[:2]
    nb = n // A_BLOCK
    n_ctx = k_ctx.shape[1]
    pad = ((0, 0), (A_BLOCK, A_BLOCK), (0, 0), (0, 0))
    kp = jnp.pad(k, pad)
    vp = jnp.pad(v, pad)
    qb = q.reshape(bsz, nb, A_BLOCK, A_KV_HEADS, A_GROUP, A_HEAD_DIM).swapaxes(0, 1)

    def one_block(args):
        i, qi = args
        kw = lax.dynamic_slice_in_dim(kp, i * A_BLOCK, 3 * A_BLOCK, axis=1)
        vw = lax.dynamic_slice_in_dim(vp, i * A_BLOCK, 3 * A_BLOCK, axis=1)
        qpos = i * A_BLOCK + jnp.arange(A_BLOCK)
        kpos = (i - 1) * A_BLOCK + jnp.arange(3 * A_BLOCK)
        band = ((jnp.abs(qpos[:, None] - kpos[None, :]) <= A_WINDOW)
                & (kpos >= 0)[None, :] & (kpos < n)[None, :])
        valid = jnp.concatenate([jnp.ones((A_BLOCK, n_ctx), bool), band], axis=1)
        return sink_attention(qi, jnp.concatenate([k_ctx, kw], axis=1),
                              jnp.concatenate([v_ctx, vw], axis=1), valid, sink)

    o = lax.map(one_block, (jnp.arange(nb), qb))
    return o.swapaxes(0, 1).reshape(bsz, n, A_HEADS * A_HEAD_DIM)


def chunk_gmlp(u, v, w_s, b_s):
    bsz, n, _ = u.shape
    nc = n // B_CHUNK
    vf = v.astype(jnp.float32)
    mu = jnp.mean(vf, axis=-1, keepdims=True)
    var = jnp.mean(jnp.square(vf - mu), axis=-1, keepdims=True)
    vn = ((vf - mu) * lax.rsqrt(var + EPS)).astype(v.dtype)
    vn = vn.reshape(bsz, nc, B_CHUNK, B_GROUPS, B_GDIM)
    mixed = jnp.einsum('gpq,bcqgd->bcpgd', w_s, vn) + b_s.T[None, None, :, :, None]
    return u * mixed.reshape(bsz, n, B_WIDTH)


def dw_conv(x, w, b):
    y = lax.conv_general_dilated(x, w[:, None, :].astype(x.dtype), window_strides=(1,),
                                 padding=[(C_CONV // 2, C_CONV // 2)],
                                 dimension_numbers=('NWC', 'WIO', 'NWC'),
                                 feature_group_count=x.shape[-1])
    return y + b


def ssd_scan(x, dt, a, bmat, cmat, h0):
    f32 = jnp.float32
    bsz, n = x.shape[:2]
    nc = n // C_CHUNK
    xc = x.astype(f32).reshape(bsz, nc, C_CHUNK, C_GROUPS, C_REP, C_HEAD_DIM)
    dtc = dt.reshape(bsz, nc, C_CHUNK, C_GROUPS, C_REP)
    bc = bmat.astype(f32).reshape(bsz, nc, C_CHUNK, C_GROUPS, C_STATE)
    cc = cmat.astype(f32).reshape(bsz, nc, C_CHUNK, C_GROUPS, C_STATE)
    acs = jnp.cumsum(dtc * a.reshape(C_GROUPS, C_REP), axis=2)
    xdt = xc * dtc[..., None]
    acs_t = jnp.moveaxis(acs, 2, -1)
    tril = jnp.tril(jnp.ones((C_CHUNK, C_CHUNK), bool))
    decay = jnp.exp(jnp.where(tril, acs_t[..., :, None] - acs_t[..., None, :], -jnp.inf))
    cb = jnp.einsum('bcign,bcjgn->bcgij', cc, bc)
    y_diag = jnp.einsum('bcgrij,bcjgrp->bcigrp', cb[:, :, :, None] * decay, xdt)
    dte = jnp.exp(acs[:, :, -1:] - acs)
    states = jnp.einsum('bcjgrp,bcjgn->bcgrpn', xdt * dte[..., None], bc)
    chunk_decay = jnp.exp(acs[:, :, -1])

    def step(h, inp):
        st, dec = inp
        return h * dec[..., None, None] + st, h

    h_last, h_in = lax.scan(step,
                            h0.astype(f32).reshape(bsz, C_GROUPS, C_REP, C_HEAD_DIM, C_STATE),
                            (jnp.moveaxis(states, 1, 0), jnp.moveaxis(chunk_decay, 1, 0)))
    h_in = jnp.moveaxis(h_in, 0, 1)
    y_off = jnp.einsum('bcign,bcgrpn->bcigrp', cc, h_in) * jnp.exp(acs)[..., None]
    y = (y_diag + y_off).reshape(bsz, n, C_HEADS, C_HEAD_DIM).astype(x.dtype)
    return y, h_last.reshape(bsz, C_HEADS, C_HEAD_DIM, C_STATE)


def mamba_branch(z, xbc, dt_raw, conv_w, conv_b, dt_bias, a_log, d_skip, norm_g, h0):
    f32 = jnp.float32
    bsz, n, _ = z.shape
    xbc = jax.nn.silu(dw_conv(xbc, conv_w, conv_b))
    xs, bm, cm = jnp.split(xbc, (C_INNER, C_INNER + C_GROUPS * C_STATE), axis=-1)
    xs = xs.reshape(bsz, n, C_HEADS, C_HEAD_DIM)
    bm = bm.reshape(bsz, n, C_GROUPS, C_STATE)
    cm = cm.reshape(bsz, n, C_GROUPS, C_STATE)
    dt = jax.nn.softplus(dt_raw.astype(f32).reshape(bsz, n, 2, C_HEADS) + dt_bias.astype(f32))
    a = -jnp.exp(a_log.astype(f32))
    y_f, h_f = ssd_scan(xs, dt[:, :, 0], a[0], bm, cm, h0[:, 0])
    flip = lambda t: jnp.flip(t, axis=1)
    y_b, h_b = ssd_scan(flip(xs), flip(dt[:, :, 1]), a[1], flip(bm), flip(cm), h0[:, 1])
    y = y_f + flip(y_b) + d_skip[:, None] * xs
    y = rms_norm(y.reshape(bsz, n, C_INNER) * jax.nn.silu(z), norm_g)
    return y, jnp.stack([h_f, h_b], axis=1)


def hier_moe(h, w_rg, b_rg, w_re, b_re, w_gate, w_up, w_down):
    f32 = jnp.float32
    shp = h.shape
    t = h.reshape(-1, D_MODEL)
    rows = jnp.arange(t.shape[0])
    lg = (t @ w_rg).astype(f32) + b_rg.astype(f32)
    g_idx = jnp.argmax(lg, axis=-1)
    p_grp = jax.nn.softmax(lg, axis=-1)[rows, g_idx][:, None]
    le = ((t @ w_re).astype(f32) + b_re.astype(f32)).reshape(-1, MOE_GROUPS, MOE_PER_GROUP)
    le = le[rows, g_idx]
    top_p, top_i = lax.top_k(jax.nn.softmax(le, axis=-1), MOE_TOPK)
    wts = p_grp * top_p / jnp.sum(top_p, axis=-1, keepdims=True)
    expert = g_idx[:, None] * MOE_PER_GROUP + top_i
    combine = jnp.einsum('nk,nke->ne', wts,
                         jax.nn.one_hot(expert, MOE_EXPERTS, dtype=f32)).astype(t.dtype)
    out = jnp.zeros_like(t)
    for e in range(MOE_EXPERTS):
        he = jax.nn.silu(t @ w_gate[e]) * (t @ w_up[e])
        out = out + combine[:, e:e + 1] * (he @ w_down[e])
    return out.reshape(shp)


def block(x, mod, k_ctx, v_ctx, h0, rope, norm1_g, w_in, attn_sink, gm_w, gm_b, conv_w, conv_b,
          dt_bias, a_log, d_skip, ssm_norm_g, w_br, w_o, norm2_g, w_rg, b_rg, w_re, b_re,
          w_gate, w_up, w_down):
    shift1, scale1, gate1, shift2, scale2, gate2 = mod
    bsz, n, _ = x.shape
    h = rms_norm(x, norm1_g) * (1 + scale1) + shift1
    proj = h @ w_in
    q, k, v, gu, gv, cz, cxbc, cdt, gates = jnp.split(proj, IN_SPLIT, axis=-1)
    q = q.reshape(bsz, n, A_HEADS, A_HEAD_DIM)
    k = k.reshape(bsz, n, A_KV_HEADS, A_HEAD_DIM)
    v = v.reshape(bsz, n, A_KV_HEADS, A_HEAD_DIM)
    if k_ctx is None:
        o_a = context_attention(q, k, v, attn_sink)
        new_kv = (k, v)
    else:
        cos, sin = rope
        o_a = latent_attention(apply_rope(q, cos, sin), apply_rope(k, cos, sin), v,
                               k_ctx, v_ctx, attn_sink)
        new_kv = None
    o_b = chunk_gmlp(gu, gv, gm_w, gm_b)
    o_c, h_fin = mamba_branch(cz, cxbc, cdt, conv_w, conv_b, dt_bias, a_log, d_skip, ssm_norm_g, h0)
    g = jax.nn.sigmoid(gates.astype(jnp.float32)).astype(x.dtype).reshape(bsz, n, N_BRANCH, D_MODEL)
    merged = (g[:, :, 0] * (o_a @ w_br[0]) + g[:, :, 1] * (o_b @ w_br[1])
              + g[:, :, 2] * (o_c @ w_br[2]))
    x = x + gate1 * (merged @ w_o)
    h2 = rms_norm(x, norm2_g) * (1 + scale2) + shift2
    x = x + gate2 * hier_moe(h2, w_rg, b_rg, w_re, b_re, w_gate, w_up, w_down)
    return x, new_kv, h_fin


def setup_inputs(seed: int = 0) -> dict:
    key = jax.random.key(seed)
    k = jax.random.split(key, 40)
    f32 = jnp.float32
    D = D_MODEL

    def nrm(i, shape, scale):
        return jax.random.normal(k[i], shape, f32) * scale

    dt0 = jnp.exp(jax.random.uniform(k[20], (DEPTH, 2, C_HEADS), f32,
                                     math.log(1e-3), math.log(1e-1)))
    return {
        'x_prompt': nrm(0, (BATCH, SEQ, D), 1.0),
        'x_sample': nrm(1, (DEC_BATCH, DEC_SEQ, D), 1.0),
        'cache_k': nrm(2, (DEC_BATCH, DEPTH, PAST_LEN, A_KV_HEADS, A_HEAD_DIM), 1.0),
        'cache_v': nrm(3, (DEC_BATCH, DEPTH, PAST_LEN, A_KV_HEADS, A_HEAD_DIM), 1.0),
        'state_ssm': nrm(4, (DEC_BATCH, DEPTH, 2, C_HEADS, C_HEAD_DIM, C_STATE), 1.0),
        'c': nrm(5, (DEC_BATCH, D), 1.0),
        'c_ctx': nrm(6, (D,), 1.0),
        'w_ada': nrm(7, (DEPTH, D, 6 * D), 0.5 * D ** -0.5),
        'b_ada': nrm(8, (DEPTH, 6 * D), 0.02),
        'norm1_g': 1.0 + nrm(9, (DEPTH, D), 0.02),
        'w_in': nrm(10, (DEPTH, D, IN_WIDTH), D ** -0.5),
        'attn_sink': nrm(11, (DEPTH, A_HEADS), 1.0),
        'gm_w': nrm(12, (DEPTH, B_GROUPS, B_CHUNK, B_CHUNK), B_CHUNK ** -0.5),
        'gm_b': 1.0 + nrm(13, (DEPTH, B_GROUPS, B_CHUNK), 0.02),
        'conv_w': nrm(14, (DEPTH, C_CONV, C_XBC), C_CONV ** -0.5),
        'conv_b': nrm(15, (DEPTH, C_XBC), 0.02),
        'dt_bias': dt0 + jnp.log(-jnp.expm1(-dt0)),
        'a_log': jnp.log(jax.random.uniform(k[16], (DEPTH, 2, C_HEADS), f32, 1.0, 16.0)),
        'd_skip': 1.0 + nrm(17, (DEPTH, C_HEADS), 0.02),
        'ssm_norm_g': 1.0 + nrm(18, (DEPTH, C_INNER), 0.02),
        'w_br': nrm(19, (DEPTH, N_BRANCH, BRANCH_WIDTH, D), BRANCH_WIDTH ** -0.5),
        'w_o': nrm(21, (DEPTH, D, D), D ** -0.5),
        'norm2_g': 1.0 + nrm(22, (DEPTH, D), 0.02),
        'w_rg': nrm(23, (DEPTH, D, MOE_GROUPS), D ** -0.5),
        'b_rg': nrm(24, (DEPTH, MOE_GROUPS), 0.01),
        'w_re': nrm(25, (DEPTH, D, MOE_EXPERTS), D ** -0.5),
        'b_re': nrm(26, (DEPTH, MOE_EXPERTS), 0.01),
        'w_gate': nrm(27, (DEPTH, MOE_EXPERTS, D, MOE_HIDDEN), D ** -0.5),
        'w_up': nrm(28, (DEPTH, MOE_EXPERTS, D, MOE_HIDDEN), D ** -0.5),
        'w_down': nrm(29, (DEPTH, MOE_EXPERTS, MOE_HIDDEN, D), MOE_HIDDEN ** -0.5),
        'final_g': 1.0 + nrm(30, (D,), 0.02),
    }


def reference(x_prompt, x_sample, cache_k, cache_v, state_ssm, c, c_ctx, w_ada, b_ada, norm1_g,
              w_in, attn_sink, gm_w, gm_b, conv_w, conv_b, dt_bias, a_log, d_skip, ssm_norm_g,
              w_br, w_o, norm2_g, w_rg, b_rg, w_re, b_re, w_gate, w_up, w_down, final_g):
    rope = rope_tables(x_sample.shape[1])
    xp = x_prompt
    xs = x_sample
    ks, vs, hs = [], [], []
    for l in range(DEPTH):
        lw = (norm1_g[l], w_in[l], attn_sink[l], gm_w[l], gm_b[l], conv_w[l], conv_b[l],
              dt_bias[l], a_log[l], d_skip[l], ssm_norm_g[l], w_br[l], w_o[l], norm2_g[l],
              w_rg[l], b_rg[l], w_re[l], b_re[l], w_gate[l], w_up[l], w_down[l])
        mod_ctx = adaln(c_ctx[None, :], w_ada[l], b_ada[l])
        mod_lat = adaln(c, w_ada[l], b_ada[l])
        h0_ctx = jnp.zeros((xp.shape[0], 2, C_HEADS, C_HEAD_DIM, C_STATE), jnp.float32)
        xp, kv_l, h_l = block(xp, mod_ctx, None, None, h0_ctx, None, *lw)
        xs, _, _ = block(xs, mod_lat, cache_k[:, l], cache_v[:, l], state_ssm[:, l], rope, *lw)
        ks.append(kv_l[0])
        vs.append(kv_l[1])
        hs.append(h_l)
    y_prompt = rms_norm(xp, final_g)
    y_sample = rms_norm(xs, final_g)
    new_k = jnp.stack(ks, axis=1)
    new_v = jnp.stack(vs, axis=1)
    new_state_ssm = jnp.stack(hs, axis=1)
    return (y_prompt, y_sample, new_k, new_v, new_state_ssm)
```

```python
import functools

import numpy as np
import jax
import jax.numpy as jnp
from jax import lax
from jax.experimental import pallas as pl
from jax.experimental.pallas import tpu as pltpu

F32 = jnp.float32
BF16 = jnp.bfloat16

D_MODEL = 2048
SEQ = 256
DEC_SEQ = 4096
GRID_W = 64
EPS = 1e-6
A_HEADS = 8
A_KV_HEADS = 2
A_HEAD_DIM = 128
A_GROUP = A_HEADS // A_KV_HEADS
A_BLOCK = 128
ROPE_BASE = 10000.0
ROPE_FREQS = A_HEAD_DIM // 4
B_WIDTH = 1024
B_GROUPS = 4
B_CHUNK = 128
B_GDIM = B_WIDTH // B_GROUPS
C_INNER = 1024
C_HEAD_DIM = 64
C_HEADS = C_INNER // C_HEAD_DIM
C_GROUPS = 2
C_REP = C_HEADS // C_GROUPS
C_STATE = 64
C_CHUNK = 128
C_XBC = C_INNER + 2 * C_GROUPS * C_STATE
N_BRANCH = 3
MOE_GROUPS = 4
MOE_PER_GROUP = 4
MOE_EXPERTS = 16
MOE_HIDDEN = 512
MOE_PAIRS = 6
MOE_BUCKETS = MOE_GROUPS * MOE_PAIRS
Q_W = A_HEADS * A_HEAD_DIM
KV_W = A_KV_HEADS * A_HEAD_DIM
OFF_Q = 0
OFF_K = OFF_Q + Q_W
OFF_V = OFF_K + KV_W
OFF_GU = OFF_V + KV_W
OFF_GV = OFF_GU + B_WIDTH
OFF_CZ = OFF_GV + B_WIDTH
OFF_XBC = OFF_CZ + C_INNER
OFF_DT = OFF_XBC + C_XBC
OFF_GATES = OFF_DT + 2 * C_HEADS
LANES = 128
NEG_BIG = -1e30
V7X_VMEM_LIMIT_MB = 56


def _cparams(dims, vmem_mb=48):
    assert vmem_mb <= V7X_VMEM_LIMIT_MB
    return pltpu.CompilerParams(dimension_semantics=dims, vmem_limit_bytes=vmem_mb * 1024 * 1024)


def _silu(x):
    return x * jax.nn.sigmoid(x)


def _row_tile(n_ctx_rows, cap):
    for t in (1024, 512, 256, 128):
        if t <= cap and n_ctx_rows % t == 0 and DEC_SEQ % t == 0:
            return t
    raise ValueError("context rows must be a multiple of 128")


def _group_of_tile(i, tm, n_ctx_rows):
    nct = n_ctx_rows // tm
    per = DEC_SEQ // tm
    return jnp.where(i < nct, 0, 1 + (i - nct) // per)


def _adaln_kernel(c_ref, w_ref, b_ref, o_ref):
    s = _silu(c_ref[...])
    o_ref[0] = jnp.dot(s.astype(BF16), w_ref[0].astype(BF16), preferred_element_type=F32) + b_ref[0]


def _adaln(cond, w_ada, b_ada):
    depth = w_ada.shape[0]
    r = cond.shape[0]
    tn = 1024
    return pl.pallas_call(
        _adaln_kernel,
        grid=(depth, 6 * D_MODEL // tn),
        in_specs=[pl.BlockSpec((r, D_MODEL), lambda l, j: (0, 0)),
                  pl.BlockSpec((1, D_MODEL, tn), lambda l, j: (l, 0, j)),
                  pl.BlockSpec((1, 1, tn), lambda l, j: (l, 0, j))],
        out_specs=pl.BlockSpec((1, r, tn), lambda l, j: (l, 0, j)),
        out_shape=jax.ShapeDtypeStruct((depth, r, 6 * D_MODEL), F32),
        compiler_params=_cparams(("parallel", "parallel")),
        name="adaln",
    )(cond, w_ada, b_ada.reshape(depth, 1, 6 * D_MODEL))


def _route(logits):
    lane = lax.broadcasted_iota(jnp.int32, logits.shape, 1)
    is_g = lane < MOE_GROUPS
    lg = jnp.where(is_g, logits, NEG_BIG)
    gmax = jnp.max(lg, axis=1, keepdims=True)
    gidx = jnp.min(jnp.where(lg == gmax, lane, LANES), axis=1, keepdims=True)
    p_grp = 1.0 / jnp.sum(jnp.where(is_g, jnp.exp(lg - gmax), 0.0), axis=1, keepdims=True)
    e_id = lane - MOE_GROUPS
    in_grp = (e_id >= 0) & (e_id < MOE_EXPERTS) & ((e_id >> 2) == gidx)
    le = jnp.where(in_grp, logits, NEG_BIG)
    m1 = jnp.max(le, axis=1, keepdims=True)
    i1 = jnp.min(jnp.where(le == m1, lane, LANES), axis=1, keepdims=True)
    le2 = jnp.where(lane == i1, NEG_BIG, le)
    m2 = jnp.max(le2, axis=1, keepdims=True)
    i2 = jnp.min(jnp.where(le2 == m2, lane, LANES), axis=1, keepdims=True)
    e = jnp.exp(m2 - m1)
    w1 = p_grp / (1.0 + e)
    w2 = p_grp * e / (1.0 + e)
    first_lo = i1 < i2
    lo = jnp.where(first_lo, i1, i2) - MOE_GROUPS
    hi = jnp.where(first_lo, i2, i1) - MOE_GROUPS
    w_lo = jnp.where(first_lo, w1, w2)
    w_hi = jnp.where(first_lo, w2, w1)
    lo_l = lo & 3
    hi_l = hi & 3
    base = jnp.where(lo_l == 0, 0, jnp.where(lo_l == 1, 3, 5))
    bucket = gidx * MOE_PAIRS + base + hi_l - lo_l - 1
    return jnp.where(lane == 0, bucket.astype(F32),
                     jnp.where(lane == 1, w_lo, jnp.where(lane == 2, w_hi, 0.0)))


def _norm_kernel(*refs, has_res, has_mod, has_router, write_x):
    refs = list(refs)
    x_ref = refs.pop(0)
    x = x_ref[...]
    if has_res:
        y_ref = refs.pop(0)
        gate_ref = refs.pop(0)
        x = x + gate_ref[0] * y_ref[...].astype(F32)
    g_ref = refs.pop(0)
    if has_mod:
        sc_ref = refs.pop(0)
        sh_ref = refs.pop(0)
    if has_router:
        wr_ref = refs.pop(0)
        br_ref = refs.pop(0)
    if write_x:
        xo_ref = refs.pop(0)
        xo_ref[...] = x
    h_ref = refs.pop(0)
    h = x * lax.rsqrt(jnp.mean(x * x, axis=-1, keepdims=True) + EPS) * g_ref[...]
    if has_mod:
        h = h * sc_ref[0] + sh_ref[0]
    h_ref[...] = h.astype(h_ref.dtype)
    if has_router:
        meta_ref = refs.pop(0)
        logits = jnp.dot(h.astype(BF16), wr_ref[...], preferred_element_type=F32) + br_ref[...]
        meta_ref[...] = _route(logits)


def _norm_stage(x, norm_g, n_ctx_rows, *, res=None, mod=None, router=None, write_x=False,
                out_dtype=BF16, row_range=None):
    t_rows = x.shape[0]
    tm = _row_tile(n_ctx_rows, 512)
    first, n_rows = (0, t_rows) if row_range is None else row_range
    off = first // tm
    grp = lambda i: _group_of_tile(i + off, tm, n_ctx_rows)
    row_spec = pl.BlockSpec((tm, D_MODEL), lambda i: (i + off, 0))
    vec_spec = pl.BlockSpec((1, 1, D_MODEL), lambda i: (grp(i), 0, 0))
    args, specs = [x], [row_spec]
    if res is not None:
        args += [res[0], res[1]]
        specs += [row_spec, vec_spec]
    args.append(norm_g.reshape(1, D_MODEL))
    specs.append(pl.BlockSpec((1, D_MODEL), lambda i: (0, 0)))
    if mod is not None:
        args += [mod[0], mod[1]]
        specs += [vec_spec, vec_spec]
    if router is not None:
        args += [router[0], router[1]]
        specs += [pl.BlockSpec((D_MODEL, LANES), lambda i: (0, 0)), pl.BlockSpec((1, LANES), lambda i: (0, 0))]
    out_row = pl.BlockSpec((tm, D_MODEL), lambda i: (i, 0))
    outs, out_specs = [], []
    if write_x:
        outs.append(jax.ShapeDtypeStruct((n_rows, D_MODEL), F32))
        out_specs.append(out_row)
    outs.append(jax.ShapeDtypeStruct((n_rows, D_MODEL), out_dtype))
    out_specs.append(out_row)
    if router is not None:
        outs.append(jax.ShapeDtypeStruct((n_rows, LANES), F32))
        out_specs.append(pl.BlockSpec((tm, LANES), lambda i: (i, 0)))
    kern = functools.partial(_norm_kernel, has_res=res is not None, has_mod=mod is not None,
                             has_router=router is not None, write_x=write_x)
    return pl.pallas_call(
        kern, grid=(n_rows // tm,), in_specs=specs, out_specs=out_specs, out_shape=outs,
        compiler_params=_cparams(("parallel",)), name="norm_stage",
    )(*args)


def _rope_chunk(x, cos, sin):
    lane = lax.broadcasted_iota(jnp.int32, x.shape, 1)
    swapped = jnp.where((lane & 63) < ROPE_FREQS, pltpu.roll(x, 96, 1), pltpu.roll(x, 32, 1))
    return x * cos + swapped * sin


def _proj_kernel(a_ref, w_ref, *refs, kind, n_ctx_tiles):
    acc = jnp.dot(a_ref[...], w_ref[...], preferred_element_type=F32)
    if kind == "plain":
        (o_ref,) = refs
        o_ref[...] = acc.astype(o_ref.dtype)
    elif kind == "sigmoid":
        (o_ref,) = refs
        o_ref[...] = jax.nn.sigmoid(acc).astype(o_ref.dtype)
    else:
        cos_ref, sin_ref, o_ref = refs
        i = pl.program_id(0)
        j = pl.program_id(1)
        tn = acc.shape[1]
        n_q_tiles = Q_W // tn
        is_lat = i >= n_ctx_tiles
        cos = jnp.where(is_lat, cos_ref[...], 1.0)
        sin = jnp.where(is_lat, sin_ref[...], 0.0)

        def emit(n_rope):
            for c in range(tn // LANES):
                chunk = acc[:, c * LANES:(c + 1) * LANES]
                if c < n_rope:
                    chunk = _rope_chunk(chunk, cos, sin)
                o_ref[:, c * LANES:(c + 1) * LANES] = chunk.astype(o_ref.dtype)

        @pl.when(j < n_q_tiles)
        def _():
            emit(tn // LANES)

        @pl.when(j >= n_q_tiles)
        def _():
            emit(KV_W // LANES)


def _proj(a, w, n_ctx_rows, *, kind="plain", out_dtype=BF16, rope=None, tn=None, row_range=None):
    t_rows, k = a.shape
    n = w.shape[1]
    tm = _row_tile(n_ctx_rows, 1024)
    first, n_rows = (0, t_rows) if row_range is None else row_range
    off = first // tm
    if tn is None:
        tn = next(c for c in (1024, 768, 640, 512, 256, 128) if n % c == 0)
    args = [a, w]
    specs = [pl.BlockSpec((tm, k), lambda i, j: (i + off, 0)), pl.BlockSpec((k, tn), lambda i, j: (0, j))]
    if kind == "rope":
        assert tn == 2 * KV_W and n == Q_W + 2 * KV_W
        nct = n_ctx_rows // tm
        per = DEC_SEQ // tm
        pos_blk = lambda i, j: (jnp.where(i < nct, 0, (i - nct) % per), 0)
        args += [rope[0], rope[1]]
        specs += [pl.BlockSpec((tm, LANES), pos_blk), pl.BlockSpec((tm, LANES), pos_blk)]
    kern = functools.partial(_proj_kernel, kind=kind, n_ctx_tiles=n_ctx_rows // tm)
    return pl.pallas_call(
        kern, grid=(n_rows // tm, n // tn), in_specs=specs,
        out_specs=pl.BlockSpec((tm, tn), lambda i, j: (i, j)),
        out_shape=jax.ShapeDtypeStruct((n_rows, n), out_dtype),
        compiler_params=_cparams(("parallel", "parallel")), name="proj_" + kind,
    )(*args)


def _rope_tables():
    pos = np.arange(DEC_SEQ)
    row = (pos // GRID_W).astype(np.float32)
    col = (pos % GRID_W).astype(np.float32)
    inv = jnp.asarray(ROPE_BASE, F32) ** (-jnp.arange(ROPE_FREQS, dtype=F32) / ROPE_FREQS)
    ar = jnp.asarray(row)[:, None] * inv
    ac = jnp.asarray(col)[:, None] * inv
    cos = jnp.concatenate([jnp.cos(ar), jnp.cos(ar), jnp.cos(ac), jnp.cos(ac)], axis=1)
    sin = jnp.concatenate([-jnp.sin(ar), jnp.sin(ar), -jnp.sin(ac), jnp.sin(ac)], axis=1)
    return cos, sin


def _softmax_pv(s, sink_col, v):
    m = jnp.maximum(jnp.max(s, axis=1, keepdims=True), sink_col)
    p = jnp.exp(s - m)
    denom = jnp.sum(p, axis=1, keepdims=True) + jnp.exp(sink_col - m)
    o = jnp.dot(p.astype(BF16), v, preferred_element_type=F32)
    return o / denom


def _stack_heads(q):
    return jnp.concatenate([q[:, g * LANES:(g + 1) * LANES] for g in range(A_GROUP)], axis=0)


def _unstack_heads(o, rows):
    return jnp.concatenate([o[g * rows:(g + 1) * rows, :] for g in range(A_GROUP)], axis=1)


def _sink_col(sink_ref, h, rows):
    r = lax.broadcasted_iota(jnp.int32, (A_GROUP * rows, 1), 0)
    col = jnp.zeros((A_GROUP * rows, 1), F32)
    for g in range(A_GROUP):
        col = jnp.where((r >= g * rows) & (r < (g + 1) * rows), sink_ref[h * A_GROUP + g], col)
    return col


def _ctx_attn_kernel(sink_ref, q_ref, k_ref, v_ref, o_ref):
    h = pl.program_id(1)
    rows = q_ref.shape[0]
    q = _stack_heads(q_ref[...])
    s = lax.dot_general(q, k_ref[...], (((1,), (1,)), ((), ())), preferred_element_type=F32)
    s = s * (A_HEAD_DIM ** -0.5)
    o = _softmax_pv(s, _sink_col(sink_ref, h, rows), v_ref[...])
    o_ref[...] = _unstack_heads(o, rows).astype(o_ref.dtype)


def _ctx_attention(qkv, sink, n_seq):
    gw = A_GROUP * A_HEAD_DIM
    return pl.pallas_call(
        _ctx_attn_kernel,
        grid=(n_seq, A_KV_HEADS),
        in_specs=[pl.BlockSpec(memory_space=pltpu.MemorySpace.SMEM),
                  pl.BlockSpec((SEQ, gw), lambda s, h: (s, h)),
                  pl.BlockSpec((SEQ, A_HEAD_DIM), lambda s, h: (s, OFF_K // A_HEAD_DIM + h)),
                  pl.BlockSpec((SEQ, A_HEAD_DIM), lambda s, h: (s, OFF_V // A_HEAD_DIM + h))],
        out_specs=pl.BlockSpec((SEQ, gw), lambda s, h: (s, h)),
        out_shape=jax.ShapeDtypeStruct((n_seq * SEQ, Q_W), BF16),
        compiler_params=_cparams(("parallel", "parallel")), name="ctx_attn",
    )(sink, qkv, qkv, qkv)


def _lat_attn_kernel(sink_ref, bias_ref, q_ref, kc_ref, vc_ref, kp_ref, k0_ref, kn_ref,
                     vp_ref, v0_ref, vn_ref, o_ref):
    h = pl.program_id(2)
    q = _stack_heads(q_ref[...])
    k = jnp.concatenate([kc_ref[0], kp_ref[...], k0_ref[...], kn_ref[...]], axis=0)
    v = jnp.concatenate([vc_ref[0], vp_ref[...], v0_ref[...], vn_ref[...]], axis=0)
    s = lax.dot_general(q, k, (((1,), (1,)), ((), ())), preferred_element_type=F32)
    s = s * (A_HEAD_DIM ** -0.5) + bias_ref[0]
    o = _softmax_pv(s, _sink_col(sink_ref, h, A_BLOCK), v)
    o_ref[...] = _unstack_heads(o, A_BLOCK).astype(o_ref.dtype)


def _band_bias(n_ctx_keys):
    ii = np.arange(A_BLOCK)[:, None]
    jj = np.arange(A_BLOCK)[None, :]
    ok = np.ones((A_BLOCK, n_ctx_keys), bool)
    cur = np.ones((A_BLOCK, A_BLOCK), bool)
    prev = jj >= ii
    nxt = jj <= ii
    none = np.zeros((A_BLOCK, A_BLOCK), bool)
    variants = []
    for has_prev, has_next in ((False, True), (True, True), (True, False)):
        valid = np.concatenate([ok, prev if has_prev else none, cur, nxt if has_next else none], axis=1)
        variants.append(np.tile(np.where(valid, 0.0, NEG_BIG).astype(np.float32), (A_GROUP, 1)))
    return jnp.asarray(np.stack(variants))


def _lat_attention(qkv, k_ctx, v_ctx, sink, n_ctx_rows, n_lat):
    gw = A_GROUP * A_HEAD_DIM
    nb = DEC_SEQ // A_BLOCK
    base = n_ctx_rows // A_BLOCK
    n_ctx_keys = k_ctx.shape[1]
    kcol = OFF_K // A_HEAD_DIM
    vcol = OFF_V // A_HEAD_DIM
    row = lambda b, i: base + b * nb + i
    prev = lambda i: jnp.maximum(i - 1, 0)
    nxt = lambda i: jnp.minimum(i + 1, nb - 1)
    blk = (A_BLOCK, A_HEAD_DIM)
    ctx_blk = (1, n_ctx_keys, A_HEAD_DIM)
    bias_sel = lambda b, i, h: (jnp.where(i == 0, 0, jnp.where(i == nb - 1, 2, 1)), 0, 0)
    return pl.pallas_call(
        _lat_attn_kernel,
        grid=(n_lat, nb, A_KV_HEADS),
        in_specs=[pl.BlockSpec(memory_space=pltpu.MemorySpace.SMEM),
                  pl.BlockSpec((1, A_GROUP * A_BLOCK, n_ctx_keys + 3 * A_BLOCK), bias_sel),
                  pl.BlockSpec((A_BLOCK, gw), lambda b, i, h: (row(b, i), h)),
                  pl.BlockSpec(ctx_blk, lambda b, i, h: (b, 0, h)),
                  pl.BlockSpec(ctx_blk, lambda b, i, h: (b, 0, h)),
                  pl.BlockSpec(blk, lambda b, i, h: (row(b, prev(i)), kcol + h)),
                  pl.BlockSpec(blk, lambda b, i, h: (row(b, i), kcol + h)),
                  pl.BlockSpec(blk, lambda b, i, h: (row(b, nxt(i)), kcol + h)),
                  pl.BlockSpec(blk, lambda b, i, h: (row(b, prev(i)), vcol + h)),
                  pl.BlockSpec(blk, lambda b, i, h: (row(b, i), vcol + h)),
                  pl.BlockSpec(blk, lambda b, i, h: (row(b, nxt(i)), vcol + h))],
        out_specs=pl.BlockSpec((A_BLOCK, gw), lambda b, i, h: (b * nb + i, h)),
        out_shape=jax.ShapeDtypeStruct((n_lat * DEC_SEQ, Q_W), BF16),
        compiler_params=_cparams(("parallel", "parallel", "parallel")), name="lat_attn",
    )(sink, _band_bias(n_ctx_keys), qkv, k_ctx, v_ctx, qkv, qkv, qkv, qkv, qkv, qkv)


def _gmlp_kernel(uv_u_ref, uv_v_ref, w_ref, b_ref, o_ref):
    v = uv_v_ref[...].astype(F32)
    mu = jnp.mean(v, axis=-1, keepdims=True)
    vc = v - mu
    var = jnp.mean(vc * vc, axis=-1, keepdims=True)
    vn = (vc * lax.rsqrt(var + EPS)).astype(BF16)
    tm = v.shape[0]
    for c in range(tm // B_CHUNK):
        rows = slice(c * B_CHUNK, (c + 1) * B_CHUNK)
        for g in range(B_GROUPS):
            cols = slice(g * B_GDIM, (g + 1) * B_GDIM)
            mixed = jnp.dot(w_ref[g], vn[rows, cols], preferred_element_type=F32) + b_ref[g]
            o_ref[rows, cols] = (uv_u_ref[rows, cols].astype(F32) * mixed).astype(o_ref.dtype)


def _gmlp(uv, gm_w, gm_b, n_ctx_rows):
    t_rows = uv.shape[0]
    tm = _row_tile(n_ctx_rows, 512)
    return pl.pallas_call(
        _gmlp_kernel, grid=(t_rows // tm,),
        in_specs=[pl.BlockSpec((tm, B_WIDTH), lambda i: (i, 0)),
                  pl.BlockSpec((tm, B_WIDTH), lambda i: (i, 1)),
                  pl.BlockSpec((B_GROUPS, B_CHUNK, B_CHUNK), lambda i: (0, 0, 0)),
                  pl.BlockSpec((B_GROUPS, B_CHUNK, 1), lambda i: (0, 0, 0))],
        out_specs=pl.BlockSpec((tm, B_WIDTH), lambda i: (i, 0)),
        out_shape=jax.ShapeDtypeStruct((t_rows, B_WIDTH), BF16),
        compiler_params=_cparams(("parallel",)), name="gmlp",
    )(uv, uv, gm_w.astype(BF16), gm_b.reshape(B_GROUPS, B_CHUNK, 1))


def _split3(x):
    hi = x.astype(BF16)
    r1 = x - hi.astype(F32)
    mid = r1.astype(BF16)
    lo = (r1 - mid.astype(F32)).astype(BF16)
    return hi, mid, lo


def _dot3_rhs(sel, x):
    return sum(jnp.dot(sel, p, preferred_element_type=F32) for p in _split3(x))


def _dot3_lhs(x, sel):
    return sum(jnp.dot(p, sel, preferred_element_type=F32) for p in _split3(x))


def _tri(lower):
    i = lax.broadcasted_iota(jnp.int32, (C_CHUNK, C_CHUNK), 0)
    j = lax.broadcasted_iota(jnp.int32, (C_CHUNK, C_CHUNK), 1)
    return (j <= i) if lower else (j >= i)


def _head_expand(d):
    r = lax.broadcasted_iota(jnp.int32, (LANES, C_INNER), 0)
    c = lax.broadcasted_iota(jnp.int32, (LANES, C_INNER), 1)
    return (r == (c >> 6) + C_HEADS * d).astype(BF16)


def _chunk_cumsums(dt, a_row):
    a = dt * a_row
    lane = lax.broadcasted_iota(jnp.int32, a.shape, 1)
    pre = _dot3_rhs(_tri(True).astype(BF16), a)
    suf = _dot3_rhs(_tri(False).astype(BF16), a)
    cs = jnp.where(lane < C_HEADS, pre, suf)
    tot = jnp.where(lane[0:1, :] < C_HEADS, pre[C_CHUNK - 1:C_CHUNK, :], suf[0:1, :])
    return cs, tot


def _ssd_local_kernel(x_ref, hp_ref, hn_ref, dtr_ref, cw_ref, cb_ref, dtb_ref, alog_ref,
                      act_ref, dt_ref, st_ref, dec_ref, *, n_ctx_chunks):
    c = pl.program_id(0)
    ctx_per = SEQ // C_CHUNK
    lat_per = DEC_SEQ // C_CHUNK
    in_seq = jnp.where(c < n_ctx_chunks, c % ctx_per, (c - n_ctx_chunks) % lat_per)
    per = jnp.where(c < n_ctx_chunks, ctx_per, lat_per)
    first = in_seq == 0
    last = in_seq == per - 1
    x = x_ref[...].astype(F32)
    halo = hp_ref.shape[0]
    prev_row = jnp.where(first, 0.0, hp_ref[...].astype(F32)[halo - 1:halo, :])
    next_row = jnp.where(last, 0.0, hn_ref[...].astype(F32)[0:1, :])
    r = lax.broadcasted_iota(jnp.int32, x.shape, 0)
    x_m1 = jnp.where(r == 0, prev_row, pltpu.roll(x, 1, 0))
    x_p1 = jnp.where(r == C_CHUNK - 1, next_row, pltpu.roll(x, C_CHUNK - 1, 0))
    act = _silu(cw_ref[0:1, :] * x_m1 + cw_ref[1:2, :] * x + cw_ref[2:3, :] * x_p1 + cb_ref[...])
    act_bf = act.astype(BF16)
    act_ref[...] = act_bf
    z = dtr_ref[...] + dtb_ref[...]
    dt = jnp.maximum(z, 0.0) + jnp.log1p(jnp.exp(-jnp.abs(z)))
    dt_ref[...] = dt
    a_row = -jnp.exp(alog_ref[...])
    cs, tot = _chunk_cumsums(dt, a_row)
    w = jnp.exp(tot - cs) * dt
    xs = act_bf[:, :C_INNER].astype(F32)
    bmat_t = jnp.transpose(act_bf[:, C_INNER:C_INNER + LANES].astype(F32))
    for d in range(2):
        e = _head_expand(d)
        xw = (xs * _dot3_lhs(w, e)).astype(BF16)
        dec_ref[0, d] = jnp.exp(_dot3_lhs(tot, e))
        for g in range(C_GROUPS):
            bt = bmat_t[g * C_STATE:(g + 1) * C_STATE, :].astype(BF16)
            cols = slice(g * C_REP * C_HEAD_DIM, (g + 1) * C_REP * C_HEAD_DIM)
            st_ref[0, d, :, cols] = jnp.dot(bt, xw[:, cols], preferred_element_type=F32)


def _ssd_local(xbc, dt_raw, conv_w, conv_b, dt_bias, a_log, n_ctx_rows):
    t_rows = xbc.shape[0]
    nc = t_rows // C_CHUNK
    halo = 16
    per = C_CHUNK // halo
    pad = lambda v: jnp.pad(v.reshape(1, -1), ((0, 0), (0, LANES - v.size)))
    kern = functools.partial(_ssd_local_kernel, n_ctx_chunks=n_ctx_rows // C_CHUNK)
    row = lambda w: pl.BlockSpec((C_CHUNK, w), lambda c: (c, 0))
    const = lambda shape: pl.BlockSpec(shape, lambda c: (0,) * len(shape))
    return pl.pallas_call(
        kern, grid=(nc,),
        in_specs=[row(C_XBC),
                  pl.BlockSpec((halo, C_XBC), lambda c: (jnp.maximum(c * per - 1, 0), 0)),
                  pl.BlockSpec((halo, C_XBC), lambda c: (jnp.minimum((c + 1) * per, nc * per - 1), 0)),
                  row(LANES), const((3, C_XBC)), const((1, C_XBC)), const((1, LANES)), const((1, LANES))],
        out_specs=[row(C_XBC), row(LANES),
                   pl.BlockSpec((1, 2, C_STATE, C_INNER), lambda c: (c, 0, 0, 0)),
                   pl.BlockSpec((1, 2, 1, C_INNER), lambda c: (c, 0, 0, 0))],
        out_shape=[jax.ShapeDtypeStruct((t_rows, C_XBC), BF16),
                   jax.ShapeDtypeStruct((t_rows, LANES), F32),
                   jax.ShapeDtypeStruct((nc, 2, C_STATE, C_INNER), F32),
                   jax.ShapeDtypeStruct((nc, 2, 1, C_INNER), F32)],
        compiler_params=_cparams(("parallel",)), name="ssd_local",
    )(xbc, xbc, xbc, dt_raw, conv_w, conv_b.reshape(1, C_XBC), pad(dt_bias), pad(a_log))


def _ssd_scan_kernel(h0_ref, st_ref, dec_ref, hin_ref, hfin_ref, h_scr, *, n_chunks):
    k = pl.program_id(2)

    @pl.when(k == 0)
    def _():
        h_scr[...] = h0_ref[0, 0]

    h = h_scr[...]
    hin_ref[0, 0] = h
    h_new = h * dec_ref[0, 0] + st_ref[0, 0]
    h_scr[...] = h_new

    @pl.when(k == n_chunks - 1)
    def _():
        hfin_ref[0, 0] = h_new


def _ssd_scan(h0, st, dec, first_chunk, n_seq, n_chunks):
    def chunk(s, d, k):
        return first_chunk + s * n_chunks + jnp.where(d == 0, k, n_chunks - 1 - k)
    blk = (1, 1, C_STATE, C_INNER)
    return pl.pallas_call(
        functools.partial(_ssd_scan_kernel, n_chunks=n_chunks),
        grid=(n_seq, 2, n_chunks),
        in_specs=[pl.BlockSpec(blk, lambda s, d, k: (s, d, 0, 0)),
                  pl.BlockSpec(blk, lambda s, d, k: (chunk(s, d, k), d, 0, 0)),
                  pl.BlockSpec((1, 1, 1, C_INNER), lambda s, d, k: (chunk(s, d, k), d, 0, 0))],
        out_specs=[pl.BlockSpec(blk, lambda s, d, k: (chunk(s, d, k) - first_chunk, d, 0, 0)),
                   pl.BlockSpec(blk, lambda s, d, k: (s, d, 0, 0))],
        out_shape=[jax.ShapeDtypeStruct((n_seq * n_chunks, 2, C_STATE, C_INNER), F32),
                   jax.ShapeDtypeStruct((n_seq, 2, C_STATE, C_INNER), F32)],
        scratch_shapes=[pltpu.VMEM((C_STATE, C_INNER), F32)],
        compiler_params=_cparams(("parallel", "parallel", "arbitrary")), name="ssd_scan",
    )(h0, st, dec)


def _ssd_out_kernel(act_ref, dt_ref, hin_ref, z_ref, alog_ref, dskip_ref, ng_ref, o_ref):
    act = act_ref[...]
    dt = dt_ref[...]
    a_row = -jnp.exp(alog_ref[...])
    cs, _ = _chunk_cumsums(dt, a_row)
    cs_t = jnp.transpose(cs)
    dt_t = jnp.transpose(dt)
    x_bf = act[:, :C_INNER]
    bm = act[:, C_INNER:C_INNER + C_GROUPS * C_STATE]
    cm = act[:, C_INNER + C_GROUPS * C_STATE:]
    tril = _tri(True)
    triu = _tri(False)
    lane = lax.broadcasted_iota(jnp.int32, (C_CHUNK, LANES), 1)
    y_parts = []
    for g in range(C_GROUPS):
        ns = slice(g * C_STATE, (g + 1) * C_STATE)
        cb = lax.dot_general(cm[:, ns], bm[:, ns], (((1,), (1,)), ((), ())), preferred_element_type=F32)
        for hp in range(C_REP // 2):
            pair = []
            for h in (g * C_REP + 2 * hp, g * C_REP + 2 * hp + 1):
                hb = h + C_HEADS
                df = cs[:, h:h + 1] - cs_t[h:h + 1, :]
                db = cs[:, hb:hb + 1] - cs_t[hb:hb + 1, :]
                lf = jnp.exp(jnp.where(tril, df, NEG_BIG))
                lb = jnp.exp(jnp.where(triu, db, NEG_BIG))
                m = cb * (lf * dt_t[h:h + 1, :] + lb * dt_t[hb:hb + 1, :])
                pair.append(m.astype(BF16))
            cols = slice((g * C_REP + 2 * hp) * C_HEAD_DIM, (g * C_REP + 2 * hp + 2) * C_HEAD_DIM)
            xp = x_bf[:, cols]
            y0 = jnp.dot(pair[0], xp, preferred_element_type=F32)
            y1 = jnp.dot(pair[1], xp, preferred_element_type=F32)
            y_parts.append(jnp.where(lane < C_HEAD_DIM, y0, y1))
    y = jnp.concatenate(y_parts, axis=1)
    for d in range(2):
        grow = jnp.exp(_dot3_lhs(cs, _head_expand(d)))
        yo = []
        for g in range(C_GROUPS):
            ns = slice(g * C_STATE, (g + 1) * C_STATE)
            cols = slice(g * C_REP * C_HEAD_DIM, (g + 1) * C_REP * C_HEAD_DIM)
            yo.append(jnp.dot(cm[:, ns], hin_ref[0, d, :, cols].astype(BF16), preferred_element_type=F32))
        y = y + jnp.concatenate(yo, axis=1) * grow
    y = y + dskip_ref[...] * x_bf.astype(F32)
    o = y * _silu(z_ref[...].astype(F32))
    o = o * lax.rsqrt(jnp.mean(o * o, axis=-1, keepdims=True) + EPS) * ng_ref[...]
    o_ref[...] = o.astype(o_ref.dtype)


def _ssd_out(act, dt, hin, z, a_log, d_skip, norm_g):
    t_rows = act.shape[0]
    nc = t_rows // C_CHUNK
    pad = lambda v: jnp.pad(v.reshape(1, -1), ((0, 0), (0, LANES - v.size)))
    row = lambda w: pl.BlockSpec((C_CHUNK, w), lambda c: (c, 0))
    const = lambda shape: pl.BlockSpec(shape, lambda c: (0,) * len(shape))
    d_exp = jnp.repeat(d_skip, C_HEAD_DIM).reshape(1, C_INNER)
    return pl.pallas_call(
        _ssd_out_kernel, grid=(nc,),
        in_specs=[row(C_XBC), row(LANES),
                  pl.BlockSpec((1, 2, C_STATE, C_INNER), lambda c: (c, 0, 0, 0)),
                  row(C_INNER), const((1, LANES)), const((1, C_INNER)), const((1, C_INNER))],
        out_specs=row(C_INNER),
        out_shape=jax.ShapeDtypeStruct((t_rows, C_INNER), BF16),
        compiler_params=_cparams(("parallel",)), name="ssd_out",
    )(act, dt, hin, z, pad(a_log), d_exp, norm_g.reshape(1, C_INNER))


def _state_to_rows(h):
    b = h.shape[0]
    return jnp.transpose(h, (0, 1, 4, 2, 3)).reshape(b, 2, C_STATE, C_INNER)


def _rows_to_state(h):
    b = h.shape[0]
    return jnp.transpose(h.reshape(b, 2, C_STATE, C_HEADS, C_HEAD_DIM), (0, 1, 3, 4, 2))


def _merge_kernel(a0_ref, a1_ref, a2_ref, w_ref, g0_ref, g1_ref, g2_ref, o_ref):
    acc = None
    for k, (a_ref, g_ref) in enumerate(((a0_ref, g0_ref), (a1_ref, g1_ref), (a2_ref, g2_ref))):
        term = g_ref[...].astype(F32) * jnp.dot(a_ref[...], w_ref[k], preferred_element_type=F32)
        acc = term if acc is None else acc + term
    o_ref[...] = acc.astype(o_ref.dtype)


def _merge(o_a, o_b, o_c, w_br, gates, n_ctx_rows):
    t_rows = o_a.shape[0]
    tm = _row_tile(n_ctx_rows, 512)
    tn = 1024
    nj = D_MODEL // tn
    a_spec = pl.BlockSpec((tm, o_a.shape[1]), lambda i, j: (i, 0))
    g_spec = lambda k: pl.BlockSpec((tm, tn), lambda i, j: (i, k * nj + j))
    return pl.pallas_call(
        _merge_kernel, grid=(t_rows // tm, nj),
        in_specs=[a_spec, a_spec, a_spec,
                  pl.BlockSpec((N_BRANCH, o_a.shape[1], tn), lambda i, j: (0, 0, j)),
                  g_spec(0), g_spec(1), g_spec(2)],
        out_specs=pl.BlockSpec((tm, tn), lambda i, j: (i, j)),
        out_shape=jax.ShapeDtypeStruct((t_rows, D_MODEL), BF16),
        compiler_params=_cparams(("parallel", "parallel")), name="merge",
    )(o_a, o_b, o_c, w_br, gates, gates, gates)


def _out_proj_kernel(a_ref, w_ref, x_ref, gate_ref, o_ref):
    o_ref[...] = x_ref[...] + gate_ref[0] * jnp.dot(a_ref[...], w_ref[...], preferred_element_type=F32)


def _out_proj(merged, w_o, x, gate, n_ctx_rows):
    t_rows = x.shape[0]
    tm = _row_tile(n_ctx_rows, 1024)
    tn = 1024
    return pl.pallas_call(
        _out_proj_kernel, grid=(t_rows // tm, D_MODEL // tn),
        in_specs=[pl.BlockSpec((tm, D_MODEL), lambda i, j: (i, 0)),
                  pl.BlockSpec((D_MODEL, tn), lambda i, j: (0, j)),
                  pl.BlockSpec((tm, tn), lambda i, j: (i, j)),
                  pl.BlockSpec((1, 1, tn), lambda i, j: (_group_of_tile(i, tm, n_ctx_rows), 0, j))],
        out_specs=pl.BlockSpec((tm, tn), lambda i, j: (i, j)),
        out_shape=jax.ShapeDtypeStruct((t_rows, D_MODEL), F32),
        compiler_params=_cparams(("parallel", "parallel")), name="out_proj",
    )(merged, w_o, x, gate)


def _moe_kernel(ea_ref, eb_ref, nv_ref, x_ref, meta_ref, gua_ref, da_ref, gub_ref, db_ref, o_ref):
    t = pl.program_id(0)

    @pl.when(t < nv_ref[0])
    def _():
        x = x_ref[...]

        def expert(gu_ref, d_ref, wcol):
            gu = jnp.dot(x, gu_ref[0], preferred_element_type=F32)
            he = _silu(gu[:, :MOE_HIDDEN]) * gu[:, MOE_HIDDEN:]
            return wcol * jnp.dot(he.astype(BF16), d_ref[0], preferred_element_type=F32)

        o = expert(gua_ref, da_ref, meta_ref[:, 1:2]) + expert(gub_ref, db_ref, meta_ref[:, 2:3])
        o_ref[...] = o.astype(o_ref.dtype)

    @pl.when(t >= nv_ref[0])
    def _():
        o_ref[...] = jnp.zeros(o_ref.shape, o_ref.dtype)


def _moe_tile(n_rows):
    return 256 if n_rows >= 8192 else 128


def _moe(h2, meta, w_gu, w_down):
    t_rows = h2.shape[0]
    tm = _moe_tile(t_rows)
    n_tiles = t_rows // tm + MOE_BUCKETS
    p_rows = n_tiles * tm
    bucket = meta[:, 0].astype(jnp.int32)
    order = jnp.argsort(bucket, stable=True).astype(jnp.int32)
    counts = jnp.zeros((MOE_BUCKETS,), jnp.int32).at[bucket].add(1)
    padded = ((counts + tm - 1) // tm) * tm
    pad_end = jnp.cumsum(padded)
    pad_off = pad_end - padded
    off = jnp.cumsum(counts) - counts
    sorted_bucket = bucket[order]
    dest = pad_off[sorted_bucket] + jnp.arange(t_rows, dtype=jnp.int32) - off[sorted_bucket]
    src = jnp.zeros((p_rows,), jnp.int32).at[dest].set(order)
    pos = jnp.zeros((t_rows,), jnp.int32).at[order].set(dest)
    tile_start = jnp.arange(n_tiles, dtype=jnp.int32) * tm
    tile_bucket = jnp.searchsorted(pad_end, tile_start, side="right").astype(jnp.int32)
    n_valid = (pad_end[-1] // tm).astype(jnp.int32).reshape(1)
    last_bucket = jnp.max(jnp.where(counts > 0, jnp.arange(MOE_BUCKETS), 0)).astype(jnp.int32)
    tile_bucket = jnp.where(tile_start < pad_end[-1], tile_bucket, last_bucket)
    pair_lo = np.array([0, 0, 0, 1, 1, 2], np.int32)
    pair_hi = np.array([1, 2, 3, 2, 3, 3], np.int32)
    grp = tile_bucket // MOE_PAIRS
    ea = grp * MOE_PER_GROUP + jnp.asarray(pair_lo)[tile_bucket % MOE_PAIRS]
    eb = grp * MOE_PER_GROUP + jnp.asarray(pair_hi)[tile_bucket % MOE_PAIRS]
    x_sorted = jnp.take(h2, src, axis=0)
    meta_sorted = jnp.take(meta, src, axis=0)
    gu_spec = lambda sel: pl.BlockSpec((1, D_MODEL, 2 * MOE_HIDDEN), sel)
    dn_spec = lambda sel: pl.BlockSpec((1, MOE_HIDDEN, D_MODEL), sel)
    sel_a = lambda t, ea, eb, nv: (ea[t], 0, 0)
    sel_b = lambda t, ea, eb, nv: (eb[t], 0, 0)
    y_sorted = pl.pallas_call(
        _moe_kernel,
        grid_spec=pltpu.PrefetchScalarGridSpec(
            num_scalar_prefetch=3, grid=(n_tiles,),
            in_specs=[pl.BlockSpec((tm, D_MODEL), lambda t, ea, eb, nv: (t, 0)),
                      pl.BlockSpec((tm, LANES), lambda t, ea, eb, nv: (t, 0)),
                      gu_spec(sel_a), dn_spec(sel_a), gu_spec(sel_b), dn_spec(sel_b)],
            out_specs=pl.BlockSpec((tm, D_MODEL), lambda t, ea, eb, nv: (t, 0))),
        out_shape=jax.ShapeDtypeStruct((p_rows, D_MODEL), BF16),
        compiler_params=_cparams(("arbitrary",)), name="moe",
    )(ea, eb, n_valid, x_sorted, meta_sorted, w_gu, w_down, w_gu, w_down)
    return jnp.take(y_sorted, pos, axis=0)


def _vec(mod_l, k):
    return mod_l[:, k * D_MODEL:(k + 1) * D_MODEL].reshape(-1, 1, D_MODEL)


def kernel(x_prompt, x_sample, cache_k, cache_v, state_ssm, c, c_ctx, w_ada, b_ada, norm1_g, w_in, attn_sink,
           gm_w, gm_b, conv_w, conv_b, dt_bias, a_log, d_skip, ssm_norm_g, w_br, w_o, norm2_g, w_rg, b_rg,
           w_re, b_re, w_gate, w_up, w_down, final_g):
    depth = w_in.shape[0]
    n_ctx, n_lat = x_prompt.shape[0], x_sample.shape[0]
    assert x_prompt.shape[1:] == (SEQ, D_MODEL) and x_sample.shape[1:] == (DEC_SEQ, D_MODEL)
    n_ctx_rows = n_ctx * SEQ
    n_lat_rows = n_lat * DEC_SEQ
    x = jnp.concatenate([x_prompt.reshape(n_ctx_rows, D_MODEL), x_sample.reshape(n_lat_rows, D_MODEL)], axis=0)

    n_cond = 1 + n_lat
    cond = jnp.zeros((((n_cond + 7) // 8) * 8, D_MODEL), F32).at[0].set(c_ctx).at[1:n_cond].set(c)
    mod = _adaln(cond, w_ada, b_ada)
    rope = _rope_tables()

    ks, vs, hs = [], [], []
    pending = None
    for l in range(depth):
        shift1, scale1, gate1, shift2, scale2, gate2 = (_vec(mod[l], k) for k in range(6))
        w = w_in[l]
        bf = lambda a: a.astype(BF16)
        if pending is None:
            (h,) = _norm_stage(x, norm1_g[l], n_ctx_rows, mod=(1.0 + scale1, shift1))
        else:
            x, h = _norm_stage(x, norm1_g[l], n_ctx_rows, res=pending, mod=(1.0 + scale1, shift1), write_x=True)
        qkv = _proj(h, bf(w[:, OFF_Q:OFF_GU]), n_ctx_rows, kind="rope", rope=rope, tn=2 * KV_W)
        kv32 = _proj(h, bf(w[:, OFF_K:OFF_GU]), n_ctx_rows, out_dtype=F32, tn=2 * KV_W, row_range=(0, n_ctx_rows))
        uv = _proj(h, bf(w[:, OFF_GU:OFF_CZ]), n_ctx_rows)
        cz = _proj(h, bf(w[:, OFF_CZ:OFF_XBC]), n_ctx_rows)
        xbc = _proj(h, bf(w[:, OFF_XBC:OFF_DT]), n_ctx_rows, tn=C_XBC // 2)
        w_dt = jnp.pad(w[:, OFF_DT:OFF_GATES], ((0, 0), (0, LANES - 2 * C_HEADS)))
        dt_raw = _proj(h, bf(w_dt), n_ctx_rows, out_dtype=F32)
        gates = _proj(h, bf(w[:, OFF_GATES:]), n_ctx_rows, kind="sigmoid")

        o_a_ctx = _ctx_attention(qkv, attn_sink[l], n_ctx)
        k_ctx = bf(cache_k[:, l].reshape(n_lat, -1, KV_W))
        v_ctx = bf(cache_v[:, l].reshape(n_lat, -1, KV_W))
        o_a_lat = _lat_attention(qkv, k_ctx, v_ctx, attn_sink[l], n_ctx_rows, n_lat)
        o_a = jnp.concatenate([o_a_ctx, o_a_lat], axis=0)
        ks.append(kv32[:, :KV_W].reshape(n_ctx, SEQ, A_KV_HEADS, A_HEAD_DIM))
        vs.append(kv32[:, KV_W:].reshape(n_ctx, SEQ, A_KV_HEADS, A_HEAD_DIM))
        o_b = _gmlp(uv, gm_w[l], gm_b[l], n_ctx_rows)
        act, dt, st, dec = _ssd_local(xbc, dt_raw, conv_w[l], conv_b[l], dt_bias[l], a_log[l], n_ctx_rows)
        ncc = n_ctx_rows // C_CHUNK
        hin_c, hfin_c = _ssd_scan(jnp.zeros((n_ctx, 2, C_STATE, C_INNER), F32), st, dec, 0, n_ctx, SEQ // C_CHUNK)
        hin_l, _ = _ssd_scan(_state_to_rows(state_ssm[:, l]), st, dec, ncc, n_lat, DEC_SEQ // C_CHUNK)
        hin = jnp.concatenate([hin_c, hin_l], axis=0)
        o_c = _ssd_out(act, dt, hin, cz, a_log[l], d_skip[l], ssm_norm_g[l])
        hs.append(_rows_to_state(hfin_c))
        merged = _merge(o_a, o_b, o_c, bf(w_br[l]), gates, n_ctx_rows)
        x = _out_proj(merged, bf(w_o[l]), x, gate1, n_ctx_rows)
        w_r = bf(jnp.pad(jnp.concatenate([w_rg[l], w_re[l]], axis=1),
                         ((0, 0), (0, LANES - MOE_GROUPS - MOE_EXPERTS))))
        b_r = jnp.pad(jnp.concatenate([b_rg[l], b_re[l]]), (0, LANES - MOE_GROUPS - MOE_EXPERTS)).reshape(1, LANES)
        h2, meta = _norm_stage(x, norm2_g[l], n_ctx_rows, mod=(1.0 + scale2, shift2), router=(w_r, b_r))
        w_gu = bf(jnp.concatenate([w_gate[l], w_up[l]], axis=-1))
        y_moe = _moe(h2, meta, w_gu, bf(w_down[l]))
        pending = (y_moe, gate2)

    (y_prompt,) = _norm_stage(x, final_g, n_ctx_rows, res=pending, out_dtype=F32, row_range=(0, n_ctx_rows))
    (y_sample,) = _norm_stage(x, final_g, n_ctx_rows, res=pending, out_dtype=F32,
                              row_range=(n_ctx_rows, n_lat_rows))
    return (y_prompt.reshape(n_ctx, SEQ, D_MODEL), y_sample.reshape(n_lat, DEC_SEQ, D_MODEL),
            jnp.stack(ks, axis=1), jnp.stack(vs, axis=1), jnp.stack(hs, axis=1))
```

```python
import functools

import numpy as np
import jax
import jax.numpy as jnp
from jax import lax
from jax.experimental import pallas as pl
from jax.experimental.pallas import tpu as pltpu

F32 = jnp.float32
BF16 = jnp.bfloat16

D_MODEL = 2048
SEQ = 256
DEC_SEQ = 4096
GRID_W = 64
EPS = 1e-6
A_HEADS = 8
A_KV_HEADS = 2
A_HEAD_DIM = 128
A_GROUP = A_HEADS // A_KV_HEADS
A_BLOCK = 128
ROPE_BASE = 10000.0
ROPE_FREQS = A_HEAD_DIM // 4
B_WIDTH = 1024
B_GROUPS = 4
B_CHUNK = 128
B_GDIM = B_WIDTH // B_GROUPS
C_INNER = 1024
C_HEAD_DIM = 64
C_HEADS = C_INNER // C_HEAD_DIM
C_GROUPS = 2
C_REP = C_HEADS // C_GROUPS
C_STATE = 64
C_CHUNK = 128
C_XBC = C_INNER + 2 * C_GROUPS * C_STATE
N_BRANCH = 3
MOE_GROUPS = 4
MOE_PER_GROUP = 4
MOE_EXPERTS = 16
MOE_HIDDEN = 512
MOE_PAIRS = 6
MOE_BUCKETS = MOE_GROUPS * MOE_PAIRS
Q_W = A_HEADS * A_HEAD_DIM
KV_W = A_KV_HEADS * A_HEAD_DIM
OFF_Q = 0
OFF_K = OFF_Q + Q_W
OFF_V = OFF_K + KV_W
OFF_GU = OFF_V + KV_W
OFF_GV = OFF_GU + B_WIDTH
OFF_CZ = OFF_GV + B_WIDTH
OFF_XBC = OFF_CZ + C_INNER
OFF_DT = OFF_XBC + C_XBC
OFF_GATES = OFF_DT + 2 * C_HEADS
LANES = 128
NEG_BIG = -1e30
V7X_VMEM_LIMIT_MB = 56


def _cparams(dims, vmem_mb=48):
    assert vmem_mb <= V7X_VMEM_LIMIT_MB
    return pltpu.CompilerParams(dimension_semantics=dims, vmem_limit_bytes=vmem_mb * 1024 * 1024)


def _silu(x):
    return x * jax.nn.sigmoid(x)


def _row_tile(n_ctx_rows, cap):
    for t in (1024, 512, 256, 128):
        if t <= cap and n_ctx_rows % t == 0 and DEC_SEQ % t == 0:
            return t
    raise ValueError("context rows must be a multiple of 128")


def _group_of_tile(i, tm, n_ctx_rows):
    nct = n_ctx_rows // tm
    per = DEC_SEQ // tm
    return jnp.where(i < nct, 0, 1 + (i - nct) // per)


def _adaln_kernel(c_ref, w_ref, b_ref, o_ref):
    s = _silu(c_ref[...])
    o_ref[0] = jnp.dot(s.astype(BF16), w_ref[0].astype(BF16), preferred_element_type=F32) + b_ref[0]


def _adaln(cond, w_ada, b_ada):
    depth = w_ada.shape[0]
    r = cond.shape[0]
    tn = 1024
    return pl.pallas_call(
        _adaln_kernel,
        grid=(depth, 6 * D_MODEL // tn),
        in_specs=[pl.BlockSpec((r, D_MODEL), lambda l, j: (0, 0)),
                  pl.BlockSpec((1, D_MODEL, tn), lambda l, j: (l, 0, j)),
                  pl.BlockSpec((1, 1, tn), lambda l, j: (l, 0, j))],
        out_specs=pl.BlockSpec((1, r, tn), lambda l, j: (l, 0, j)),
        out_shape=jax.ShapeDtypeStruct((depth, r, 6 * D_MODEL), F32),
        compiler_params=_cparams(("parallel", "parallel")),
        name="adaln",
    )(cond, w_ada, b_ada.reshape(depth, 1, 6 * D_MODEL))


def _route(logits):
    lane = lax.broadcasted_iota(jnp.int32, logits.shape, 1)
    is_g = lane < MOE_GROUPS
    lg = jnp.where(is_g, logits, NEG_BIG)
    gmax = jnp.max(lg, axis=1, keepdims=True)
    gidx = jnp.min(jnp.where(lg == gmax, lane, LANES), axis=1, keepdims=True)
    p_grp = 1.0 / jnp.sum(jnp.where(is_g, jnp.exp(lg - gmax), 0.0), axis=1, keepdims=True)
    e_id = lane - MOE_GROUPS
    in_grp = (e_id >= 0) & (e_id < MOE_EXPERTS) & ((e_id >> 2) == gidx)
    le = jnp.where(in_grp, logits, NEG_BIG)
    m1 = jnp.max(le, axis=1, keepdims=True)
    i1 = jnp.min(jnp.where(le == m1, lane, LANES), axis=1, keepdims=True)
    le2 = jnp.where(lane == i1, NEG_BIG, le)
    m2 = jnp.max(le2, axis=1, keepdims=True)
    i2 = jnp.min(jnp.where(le2 == m2, lane, LANES), axis=1, keepdims=True)
    e = jnp.exp(m2 - m1)
    w1 = p_grp / (1.0 + e)
    w2 = p_grp * e / (1.0 + e)
    first_lo = i1 < i2
    lo = jnp.where(first_lo, i1, i2) - MOE_GROUPS
    hi = jnp.where(first_lo, i2, i1) - MOE_GROUPS
    w_lo = jnp.where(first_lo, w1, w2)
    w_hi = jnp.where(first_lo, w2, w1)
    lo_l = lo & 3
    hi_l = hi & 3
    base = jnp.where(lo_l == 0, 0, jnp.where(lo_l == 1, 3, 5))
    bucket = gidx * MOE_PAIRS + base + hi_l - lo_l - 1
    return jnp.where(lane == 0, bucket.astype(F32),
                     jnp.where(lane == 1, w_lo, jnp.where(lane == 2, w_hi, 0.0)))


def _norm_kernel(*refs, has_res, has_mod, has_router, write_x, n_alias):
    refs = list(refs)
    x_ref = refs.pop(0)
    x = x_ref[...]
    if has_res:
        y_ref = refs.pop(0)
        gate_ref = refs.pop(0)
        x = x + gate_ref[0] * y_ref[...].astype(F32)
    g_ref = refs.pop(0)
    if has_mod:
        sc_ref = refs.pop(0)
        sh_ref = refs.pop(0)
    if has_router:
        wr_ref = refs.pop(0)
        br_ref = refs.pop(0)
    del refs[:n_alias]
    if write_x:
        xo_ref = refs.pop(0)
        xo_ref[...] = x
    h_ref = refs.pop(0)
    h = x * lax.rsqrt(jnp.mean(x * x, axis=-1, keepdims=True) + EPS) * g_ref[...]
    if has_mod:
        h = h * sc_ref[0] + sh_ref[0]
    h_ref[...] = h.astype(h_ref.dtype)
    if has_router:
        meta_ref = refs.pop(0)
        logits = jnp.dot(h.astype(BF16), wr_ref[...], preferred_element_type=F32) + br_ref[...]
        meta_ref[...] = _route(logits)


def _alias_kwargs(args, specs, prev):
    if prev is None:
        return {}
    n_in = len(args)
    args += list(prev)
    specs += [pl.BlockSpec(memory_space=pl.ANY)] * len(prev)
    return {"input_output_aliases": {n_in + k: k for k in range(len(prev))}}


def _norm_stage(x, norm_g, n_ctx_rows, *, first=0, n_rows=None, x_first=None, out_rows=None, prev=None,
                res=None, mod=None, router=None, write_x=False, out_dtype=BF16):
    tm = _row_tile(n_ctx_rows, 512)
    n_rows = x.shape[0] if n_rows is None else n_rows
    off = first // tm
    x_off = off if x_first is None else x_first // tm
    out_off = 0 if out_rows is None else off
    out_rows = n_rows if out_rows is None else out_rows
    grp = lambda i: _group_of_tile(i + off, tm, n_ctx_rows)
    vec_spec = pl.BlockSpec((1, 1, D_MODEL), lambda i: (grp(i), 0, 0))
    args, specs = [x], [pl.BlockSpec((tm, D_MODEL), lambda i: (i + x_off, 0))]
    if res is not None:
        args += [res[0], res[1]]
        specs += [pl.BlockSpec((tm, D_MODEL), lambda i: (i + off, 0)), vec_spec]
    args.append(norm_g.reshape(1, D_MODEL))
    specs.append(pl.BlockSpec((1, D_MODEL), lambda i: (0, 0)))
    if mod is not None:
        args += [mod[0], mod[1]]
        specs += [vec_spec, vec_spec]
    if router is not None:
        args += [router[0], router[1]]
        specs += [pl.BlockSpec((D_MODEL, LANES), lambda i: (0, 0)), pl.BlockSpec((1, LANES), lambda i: (0, 0))]
    out_row = pl.BlockSpec((tm, D_MODEL), lambda i: (i + out_off, 0))
    outs, out_specs = [], []
    if write_x:
        outs.append(jax.ShapeDtypeStruct((out_rows, D_MODEL), F32))
        out_specs.append(out_row)
    outs.append(jax.ShapeDtypeStruct((out_rows, D_MODEL), out_dtype))
    out_specs.append(out_row)
    if router is not None:
        outs.append(jax.ShapeDtypeStruct((out_rows, LANES), F32))
        out_specs.append(pl.BlockSpec((tm, LANES), lambda i: (i + out_off, 0)))
    alias = _alias_kwargs(args, specs, prev)
    kern = functools.partial(_norm_kernel, has_res=res is not None, has_mod=mod is not None,
                             has_router=router is not None, write_x=write_x,
                             n_alias=0 if prev is None else len(prev))
    return pl.pallas_call(
        kern, grid=(n_rows // tm,), in_specs=specs, out_specs=out_specs, out_shape=outs,
        compiler_params=_cparams(("parallel",)), name="norm_stage", **alias,
    )(*args)


def _rope_chunk(x, cos, sin):
    lane = lax.broadcasted_iota(jnp.int32, x.shape, 1)
    swapped = jnp.where((lane & 63) < ROPE_FREQS, pltpu.roll(x, 96, 1), pltpu.roll(x, 32, 1))
    return x * cos + swapped * sin


def _proj_kernel(a_ref, w_ref, *refs, kind, n_ctx_tiles):
    acc = jnp.dot(a_ref[...], w_ref[...], preferred_element_type=F32)
    if kind == "plain":
        (o_ref,) = refs
        o_ref[...] = acc.astype(o_ref.dtype)
    elif kind == "sigmoid":
        (o_ref,) = refs
        o_ref[...] = jax.nn.sigmoid(acc).astype(o_ref.dtype)
    else:
        cos_ref, sin_ref, o_ref = refs
        i = pl.program_id(0)
        j = pl.program_id(1)
        tn = acc.shape[1]
        n_q_tiles = Q_W // tn
        is_lat = i >= n_ctx_tiles
        cos = jnp.where(is_lat, cos_ref[...], 1.0)
        sin = jnp.where(is_lat, sin_ref[...], 0.0)

        def emit(n_rope):
            for c in range(tn // LANES):
                chunk = acc[:, c * LANES:(c + 1) * LANES]
                if c < n_rope:
                    chunk = _rope_chunk(chunk, cos, sin)
                o_ref[:, c * LANES:(c + 1) * LANES] = chunk.astype(o_ref.dtype)

        @pl.when(j < n_q_tiles)
        def _():
            emit(tn // LANES)

        @pl.when(j >= n_q_tiles)
        def _():
            emit(KV_W // LANES)


def _proj(a, w, n_ctx_rows, *, kind="plain", out_dtype=BF16, rope=None, tn=None, row_range=None):
    t_rows, k = a.shape
    n = w.shape[1]
    tm = _row_tile(n_ctx_rows, 1024)
    first, n_rows = (0, t_rows) if row_range is None else row_range
    off = first // tm
    if tn is None:
        tn = next(c for c in (1024, 768, 640, 512, 256, 128) if n % c == 0)
    args = [a, w]
    specs = [pl.BlockSpec((tm, k), lambda i, j: (i + off, 0)), pl.BlockSpec((k, tn), lambda i, j: (0, j))]
    if kind == "rope":
        assert tn == 2 * KV_W and n == Q_W + 2 * KV_W
        nct = n_ctx_rows // tm
        per = DEC_SEQ // tm
        pos_blk = lambda i, j: (jnp.where(i < nct, 0, (i - nct) % per), 0)
        args += [rope[0], rope[1]]
        specs += [pl.BlockSpec((tm, LANES), pos_blk), pl.BlockSpec((tm, LANES), pos_blk)]
    kern = functools.partial(_proj_kernel, kind=kind, n_ctx_tiles=n_ctx_rows // tm)
    return pl.pallas_call(
        kern, grid=(n_rows // tm, n // tn), in_specs=specs,
        out_specs=pl.BlockSpec((tm, tn), lambda i, j: (i, j)),
        out_shape=jax.ShapeDtypeStruct((n_rows, n), out_dtype),
        compiler_params=_cparams(("parallel", "parallel")), name="proj_" + kind,
    )(*args)


def _rope_tables():
    pos = np.arange(DEC_SEQ)
    row = (pos // GRID_W).astype(np.float32)
    col = (pos % GRID_W).astype(np.float32)
    inv = jnp.asarray(ROPE_BASE, F32) ** (-jnp.arange(ROPE_FREQS, dtype=F32) / ROPE_FREQS)
    ar = jnp.asarray(row)[:, None] * inv
    ac = jnp.asarray(col)[:, None] * inv
    cos = jnp.concatenate([jnp.cos(ar), jnp.cos(ar), jnp.cos(ac), jnp.cos(ac)], axis=1)
    sin = jnp.concatenate([-jnp.sin(ar), jnp.sin(ar), -jnp.sin(ac), jnp.sin(ac)], axis=1)
    return cos, sin


def _softmax_pv(s, sink_col, v):
    m = jnp.maximum(jnp.max(s, axis=1, keepdims=True), sink_col)
    p = jnp.exp(s - m)
    denom = jnp.sum(p, axis=1, keepdims=True) + jnp.exp(sink_col - m)
    o = jnp.dot(p.astype(BF16), v, preferred_element_type=F32)
    return o / denom


def _stack_heads(q):
    return jnp.concatenate([q[:, g * LANES:(g + 1) * LANES] for g in range(A_GROUP)], axis=0)


def _unstack_heads(o, rows):
    return jnp.concatenate([o[g * rows:(g + 1) * rows, :] for g in range(A_GROUP)], axis=1)


def _sink_col(sink_ref, h, rows):
    r = lax.broadcasted_iota(jnp.int32, (A_GROUP * rows, 1), 0)
    col = jnp.zeros((A_GROUP * rows, 1), F32)
    for g in range(A_GROUP):
        col = jnp.where((r >= g * rows) & (r < (g + 1) * rows), sink_ref[h * A_GROUP + g], col)
    return col


def _ctx_attn_kernel(sink_ref, q_ref, k_ref, v_ref, _, o_ref):
    h = pl.program_id(1)
    rows = q_ref.shape[0]
    q = _stack_heads(q_ref[...])
    s = lax.dot_general(q, k_ref[...], (((1,), (1,)), ((), ())), preferred_element_type=F32)
    s = s * (A_HEAD_DIM ** -0.5)
    o = _softmax_pv(s, _sink_col(sink_ref, h, rows), v_ref[...])
    o_ref[...] = _unstack_heads(o, rows).astype(o_ref.dtype)


def _ctx_attention(qkv, sink, n_seq, o_all):
    gw = A_GROUP * A_HEAD_DIM
    args = [sink, qkv, qkv, qkv]
    specs = [pl.BlockSpec(memory_space=pltpu.MemorySpace.SMEM),
             pl.BlockSpec((SEQ, gw), lambda s, h: (s, h)),
             pl.BlockSpec((SEQ, A_HEAD_DIM), lambda s, h: (s, OFF_K // A_HEAD_DIM + h)),
             pl.BlockSpec((SEQ, A_HEAD_DIM), lambda s, h: (s, OFF_V // A_HEAD_DIM + h))]
    alias = _alias_kwargs(args, specs, (o_all,))
    return pl.pallas_call(
        _ctx_attn_kernel,
        grid=(n_seq, A_KV_HEADS),
        in_specs=specs,
        out_specs=pl.BlockSpec((SEQ, gw), lambda s, h: (s, h)),
        out_shape=jax.ShapeDtypeStruct(o_all.shape, BF16),
        compiler_params=_cparams(("parallel", "parallel")), name="ctx_attn", **alias,
    )(*args)


def _lat_attn_kernel(sink_ref, bias_ref, q_ref, kc_ref, vc_ref, kp_ref, k0_ref, kn_ref,
                     vp_ref, v0_ref, vn_ref, o_ref):
    h = pl.program_id(2)
    q = _stack_heads(q_ref[...])
    k = jnp.concatenate([kc_ref[0], kp_ref[...], k0_ref[...], kn_ref[...]], axis=0)
    v = jnp.concatenate([vc_ref[0], vp_ref[...], v0_ref[...], vn_ref[...]], axis=0)
    s = lax.dot_general(q, k, (((1,), (1,)), ((), ())), preferred_element_type=F32)
    s = s * (A_HEAD_DIM ** -0.5) + bias_ref[0]
    o = _softmax_pv(s, _sink_col(sink_ref, h, A_BLOCK), v)
    o_ref[...] = _unstack_heads(o, A_BLOCK).astype(o_ref.dtype)


def _band_bias(n_ctx_keys):
    ii = np.arange(A_BLOCK)[:, None]
    jj = np.arange(A_BLOCK)[None, :]
    ok = np.ones((A_BLOCK, n_ctx_keys), bool)
    cur = np.ones((A_BLOCK, A_BLOCK), bool)
    prev = jj >= ii
    nxt = jj <= ii
    none = np.zeros((A_BLOCK, A_BLOCK), bool)
    variants = []
    for has_prev, has_next in ((False, True), (True, True), (True, False)):
        valid = np.concatenate([ok, prev if has_prev else none, cur, nxt if has_next else none], axis=1)
        variants.append(np.tile(np.where(valid, 0.0, NEG_BIG).astype(np.float32), (A_GROUP, 1)))
    return jnp.asarray(np.stack(variants))


def _lat_attention(qkv, k_ctx, v_ctx, sink, n_ctx_rows, n_lat):
    gw = A_GROUP * A_HEAD_DIM
    nb = DEC_SEQ // A_BLOCK
    base = n_ctx_rows // A_BLOCK
    n_ctx_keys = k_ctx.shape[1]
    kcol = OFF_K // A_HEAD_DIM
    vcol = OFF_V // A_HEAD_DIM
    row = lambda b, i: base + b * nb + i
    prev = lambda i: jnp.maximum(i - 1, 0)
    nxt = lambda i: jnp.minimum(i + 1, nb - 1)
    blk = (A_BLOCK, A_HEAD_DIM)
    ctx_blk = (1, n_ctx_keys, A_HEAD_DIM)
    bias_sel = lambda b, i, h: (jnp.where(i == 0, 0, jnp.where(i == nb - 1, 2, 1)), 0, 0)
    return pl.pallas_call(
        _lat_attn_kernel,
        grid=(n_lat, nb, A_KV_HEADS),
        in_specs=[pl.BlockSpec(memory_space=pltpu.MemorySpace.SMEM),
                  pl.BlockSpec((1, A_GROUP * A_BLOCK, n_ctx_keys + 3 * A_BLOCK), bias_sel),
                  pl.BlockSpec((A_BLOCK, gw), lambda b, i, h: (row(b, i), h)),
                  pl.BlockSpec(ctx_blk, lambda b, i, h: (b, 0, h)),
                  pl.BlockSpec(ctx_blk, lambda b, i, h: (b, 0, h)),
                  pl.BlockSpec(blk, lambda b, i, h: (row(b, prev(i)), kcol + h)),
                  pl.BlockSpec(blk, lambda b, i, h: (row(b, i), kcol + h)),
                  pl.BlockSpec(blk, lambda b, i, h: (row(b, nxt(i)), kcol + h)),
                  pl.BlockSpec(blk, lambda b, i, h: (row(b, prev(i)), vcol + h)),
                  pl.BlockSpec(blk, lambda b, i, h: (row(b, i), vcol + h)),
                  pl.BlockSpec(blk, lambda b, i, h: (row(b, nxt(i)), vcol + h))],
        out_specs=pl.BlockSpec((A_BLOCK, gw), lambda b, i, h: (row(b, i), h)),
        out_shape=jax.ShapeDtypeStruct((n_ctx_rows + n_lat * DEC_SEQ, Q_W), BF16),
        compiler_params=_cparams(("parallel", "parallel", "parallel")), name="lat_attn",
    )(sink, _band_bias(n_ctx_keys), qkv, k_ctx, v_ctx, qkv, qkv, qkv, qkv, qkv, qkv)


def _gmlp_kernel(uv_u_ref, uv_v_ref, w_ref, b_ref, o_ref):
    v = uv_v_ref[...].astype(F32)
    mu = jnp.mean(v, axis=-1, keepdims=True)
    vc = v - mu
    var = jnp.mean(vc * vc, axis=-1, keepdims=True)
    vn = (vc * lax.rsqrt(var + EPS)).astype(BF16)
    tm = v.shape[0]
    for c in range(tm // B_CHUNK):
        rows = slice(c * B_CHUNK, (c + 1) * B_CHUNK)
        for g in range(B_GROUPS):
            cols = slice(g * B_GDIM, (g + 1) * B_GDIM)
            mixed = jnp.dot(w_ref[g], vn[rows, cols], preferred_element_type=F32) + b_ref[g]
            o_ref[rows, cols] = (uv_u_ref[rows, cols].astype(F32) * mixed).astype(o_ref.dtype)


def _gmlp(uv, gm_w, gm_b, n_ctx_rows):
    t_rows = uv.shape[0]
    tm = _row_tile(n_ctx_rows, 512)
    return pl.pallas_call(
        _gmlp_kernel, grid=(t_rows // tm,),
        in_specs=[pl.BlockSpec((tm, B_WIDTH), lambda i: (i, 0)),
                  pl.BlockSpec((tm, B_WIDTH), lambda i: (i, 1)),
                  pl.BlockSpec((B_GROUPS, B_CHUNK, B_CHUNK), lambda i: (0, 0, 0)),
                  pl.BlockSpec((B_GROUPS, B_CHUNK, 1), lambda i: (0, 0, 0))],
        out_specs=pl.BlockSpec((tm, B_WIDTH), lambda i: (i, 0)),
        out_shape=jax.ShapeDtypeStruct((t_rows, B_WIDTH), BF16),
        compiler_params=_cparams(("parallel",)), name="gmlp",
    )(uv, uv, gm_w.astype(BF16), gm_b.reshape(B_GROUPS, B_CHUNK, 1))


def _split3(x):
    hi = x.astype(BF16)
    r1 = x - hi.astype(F32)
    mid = r1.astype(BF16)
    lo = (r1 - mid.astype(F32)).astype(BF16)
    return hi, mid, lo


def _dot3_rhs(sel, x):
    return sum(jnp.dot(sel, p, preferred_element_type=F32) for p in _split3(x))


def _dot3_lhs(x, sel):
    return sum(jnp.dot(p, sel, preferred_element_type=F32) for p in _split3(x))


def _tri(lower):
    i = lax.broadcasted_iota(jnp.int32, (C_CHUNK, C_CHUNK), 0)
    j = lax.broadcasted_iota(jnp.int32, (C_CHUNK, C_CHUNK), 1)
    return (j <= i) if lower else (j >= i)


def _head_expand(d):
    r = lax.broadcasted_iota(jnp.int32, (LANES, C_INNER), 0)
    c = lax.broadcasted_iota(jnp.int32, (LANES, C_INNER), 1)
    return (r == (c >> 6) + C_HEADS * d).astype(BF16)


def _chunk_cumsums(dt, a_row):
    a = dt * a_row
    lane = lax.broadcasted_iota(jnp.int32, a.shape, 1)
    pre = _dot3_rhs(_tri(True).astype(BF16), a)
    suf = _dot3_rhs(_tri(False).astype(BF16), a)
    cs = jnp.where(lane < C_HEADS, pre, suf)
    tot = jnp.where(lane[0:1, :] < C_HEADS, pre[C_CHUNK - 1:C_CHUNK, :], suf[0:1, :])
    return cs, tot


def _ssd_local_kernel(x_ref, hp_ref, hn_ref, dtr_ref, cw_ref, cb_ref, dtb_ref, alog_ref,
                      act_ref, dt_ref, st_ref, dec_ref, *, n_ctx_chunks):
    c = pl.program_id(0)
    ctx_per = SEQ // C_CHUNK
    lat_per = DEC_SEQ // C_CHUNK
    in_seq = jnp.where(c < n_ctx_chunks, c % ctx_per, (c - n_ctx_chunks) % lat_per)
    per = jnp.where(c < n_ctx_chunks, ctx_per, lat_per)
    first = in_seq == 0
    last = in_seq == per - 1
    x = x_ref[...].astype(F32)
    halo = hp_ref.shape[0]
    prev_row = jnp.where(first, 0.0, hp_ref[...].astype(F32)[halo - 1:halo, :])
    next_row = jnp.where(last, 0.0, hn_ref[...].astype(F32)[0:1, :])
    r = lax.broadcasted_iota(jnp.int32, x.shape, 0)
    x_m1 = jnp.where(r == 0, prev_row, pltpu.roll(x, 1, 0))
    x_p1 = jnp.where(r == C_CHUNK - 1, next_row, pltpu.roll(x, C_CHUNK - 1, 0))
    act = _silu(cw_ref[0:1, :] * x_m1 + cw_ref[1:2, :] * x + cw_ref[2:3, :] * x_p1 + cb_ref[...])
    act_bf = act.astype(BF16)
    act_ref[...] = act_bf
    z = dtr_ref[...] + dtb_ref[...]
    dt = jnp.maximum(z, 0.0) + jnp.log1p(jnp.exp(-jnp.abs(z)))
    dt_ref[...] = dt
    a_row = -jnp.exp(alog_ref[...])
    cs, tot = _chunk_cumsums(dt, a_row)
    w = jnp.exp(tot - cs) * dt
    xs = act_bf[:, :C_INNER].astype(F32)
    bmat_t = jnp.transpose(act_bf[:, C_INNER:C_INNER + LANES].astype(F32))
    for d in range(2):
        e = _head_expand(d)
        xw = (xs * _dot3_lhs(w, e)).astype(BF16)
        dec_ref[0, d] = jnp.exp(_dot3_lhs(tot, e))
        for g in range(C_GROUPS):
            bt = bmat_t[g * C_STATE:(g + 1) * C_STATE, :].astype(BF16)
            cols = slice(g * C_REP * C_HEAD_DIM, (g + 1) * C_REP * C_HEAD_DIM)
            st_ref[0, d, :, cols] = jnp.dot(bt, xw[:, cols], preferred_element_type=F32)


def _ssd_local(xbc, dt_raw, conv_w, conv_b, dt_bias, a_log, n_ctx_rows):
    t_rows = xbc.shape[0]
    nc = t_rows // C_CHUNK
    halo = 16
    per = C_CHUNK // halo
    pad = lambda v: jnp.pad(v.reshape(1, -1), ((0, 0), (0, LANES - v.size)))
    kern = functools.partial(_ssd_local_kernel, n_ctx_chunks=n_ctx_rows // C_CHUNK)
    row = lambda w: pl.BlockSpec((C_CHUNK, w), lambda c: (c, 0))
    const = lambda shape: pl.BlockSpec(shape, lambda c: (0,) * len(shape))
    return pl.pallas_call(
        kern, grid=(nc,),
        in_specs=[row(C_XBC),
                  pl.BlockSpec((halo, C_XBC), lambda c: (jnp.maximum(c * per - 1, 0), 0)),
                  pl.BlockSpec((halo, C_XBC), lambda c: (jnp.minimum((c + 1) * per, nc * per - 1), 0)),
                  row(LANES), const((3, C_XBC)), const((1, C_XBC)), const((1, LANES)), const((1, LANES))],
        out_specs=[row(C_XBC), row(LANES),
                   pl.BlockSpec((1, 2, C_STATE, C_INNER), lambda c: (c, 0, 0, 0)),
                   pl.BlockSpec((1, 2, 1, C_INNER), lambda c: (c, 0, 0, 0))],
        out_shape=[jax.ShapeDtypeStruct((t_rows, C_XBC), BF16),
                   jax.ShapeDtypeStruct((t_rows, LANES), F32),
                   jax.ShapeDtypeStruct((nc, 2, C_STATE, C_INNER), F32),
                   jax.ShapeDtypeStruct((nc, 2, 1, C_INNER), F32)],
        compiler_params=_cparams(("parallel",)), name="ssd_local",
    )(xbc, xbc, xbc, dt_raw, conv_w, conv_b.reshape(1, C_XBC), pad(dt_bias), pad(a_log))


def _ssd_scan_kernel(h0_ref, st_ref, dec_ref, *refs, n_chunks):
    hin_ref, hfin_ref = refs[-2:]
    d = pl.program_id(1)

    def run(order):
        for j in range(C_INNER // LANES):
            cols = slice(j * LANES, (j + 1) * LANES)
            h = h0_ref[0, 0, :, cols]
            for c in order:
                hin_ref[c, 0, :, cols] = h
                h = h * dec_ref[c, 0, :, cols] + st_ref[c, 0, :, cols]
            hfin_ref[0, 0, :, cols] = h

    @pl.when(d == 0)
    def _():
        run(range(n_chunks))

    @pl.when(d == 1)
    def _():
        run(range(n_chunks - 1, -1, -1))


def _ssd_scan(h0, st, dec, first_chunk, n_seq, n_chunks, prev=None):
    assert first_chunk % n_chunks == 0
    base = first_chunk // n_chunks
    blk = (n_chunks, 1, C_STATE, C_INNER)
    one = (1, 1, C_STATE, C_INNER)
    args = [h0, st, dec]
    specs = [pl.BlockSpec(one, lambda s, d: (s, d, 0, 0)),
             pl.BlockSpec(blk, lambda s, d: (base + s, d, 0, 0)),
             pl.BlockSpec((n_chunks, 1, 1, C_INNER), lambda s, d: (base + s, d, 0, 0))]
    alias = _alias_kwargs(args, specs, None if prev is None else (prev,))
    return pl.pallas_call(
        functools.partial(_ssd_scan_kernel, n_chunks=n_chunks),
        grid=(n_seq, 2), in_specs=specs,
        out_specs=[pl.BlockSpec(blk, lambda s, d: (base + s, d, 0, 0)),
                   pl.BlockSpec(one, lambda s, d: (s, d, 0, 0))],
        out_shape=[jax.ShapeDtypeStruct(st.shape, F32),
                   jax.ShapeDtypeStruct((n_seq, 2, C_STATE, C_INNER), F32)],
        compiler_params=_cparams(("parallel", "parallel")), name="ssd_scan", **alias,
    )(*args)


def _ssd_out_kernel(act_ref, dt_ref, hin_ref, z_ref, alog_ref, dskip_ref, ng_ref, o_ref):
    act = act_ref[...]
    dt = dt_ref[...]
    a_row = -jnp.exp(alog_ref[...])
    cs, _ = _chunk_cumsums(dt, a_row)
    cs_t = jnp.transpose(cs)
    dt_t = jnp.transpose(dt)
    x_bf = act[:, :C_INNER]
    bm = act[:, C_INNER:C_INNER + C_GROUPS * C_STATE]
    cm = act[:, C_INNER + C_GROUPS * C_STATE:]
    tril = _tri(True)
    triu = _tri(False)
    lane = lax.broadcasted_iota(jnp.int32, (C_CHUNK, LANES), 1)
    y_parts = []
    for g in range(C_GROUPS):
        ns = slice(g * C_STATE, (g + 1) * C_STATE)
        cb = lax.dot_general(cm[:, ns], bm[:, ns], (((1,), (1,)), ((), ())), preferred_element_type=F32)
        for hp in range(C_REP // 2):
            pair = []
            for h in (g * C_REP + 2 * hp, g * C_REP + 2 * hp + 1):
                hb = h + C_HEADS
                df = cs[:, h:h + 1] - cs_t[h:h + 1, :]
                db = cs[:, hb:hb + 1] - cs_t[hb:hb + 1, :]
                lf = jnp.exp(jnp.where(tril, df, NEG_BIG))
                lb = jnp.exp(jnp.where(triu, db, NEG_BIG))
                m = cb * (lf * dt_t[h:h + 1, :] + lb * dt_t[hb:hb + 1, :])
                pair.append(m.astype(BF16))
            cols = slice((g * C_REP + 2 * hp) * C_HEAD_DIM, (g * C_REP + 2 * hp + 2) * C_HEAD_DIM)
            xp = x_bf[:, cols]
            y0 = jnp.dot(pair[0], xp, preferred_element_type=F32)
            y1 = jnp.dot(pair[1], xp, preferred_element_type=F32)
            y_parts.append(jnp.where(lane < C_HEAD_DIM, y0, y1))
    y = jnp.concatenate(y_parts, axis=1)
    for d in range(2):
        grow = jnp.exp(_dot3_lhs(cs, _head_expand(d)))
        yo = []
        for g in range(C_GROUPS):
            ns = slice(g * C_STATE, (g + 1) * C_STATE)
            cols = slice(g * C_REP * C_HEAD_DIM, (g + 1) * C_REP * C_HEAD_DIM)
            yo.append(jnp.dot(cm[:, ns], hin_ref[0, d, :, cols].astype(BF16), preferred_element_type=F32))
        y = y + jnp.concatenate(yo, axis=1) * grow
    y = y + dskip_ref[...] * x_bf.astype(F32)
    o = y * _silu(z_ref[...].astype(F32))
    o = o * lax.rsqrt(jnp.mean(o * o, axis=-1, keepdims=True) + EPS) * ng_ref[...]
    o_ref[...] = o.astype(o_ref.dtype)


def _ssd_out(act, dt, hin, z, a_log, d_skip, norm_g):
    t_rows = act.shape[0]
    nc = t_rows // C_CHUNK
    pad = lambda v: jnp.pad(v.reshape(1, -1), ((0, 0), (0, LANES - v.size)))
    row = lambda w: pl.BlockSpec((C_CHUNK, w), lambda c: (c, 0))
    const = lambda shape: pl.BlockSpec(shape, lambda c: (0,) * len(shape))
    d_exp = jnp.repeat(d_skip, C_HEAD_DIM).reshape(1, C_INNER)
    return pl.pallas_call(
        _ssd_out_kernel, grid=(nc,),
        in_specs=[row(C_XBC), row(LANES),
                  pl.BlockSpec((1, 2, C_STATE, C_INNER), lambda c: (c, 0, 0, 0)),
                  row(C_INNER), const((1, LANES)), const((1, C_INNER)), const((1, C_INNER))],
        out_specs=row(C_INNER),
        out_shape=jax.ShapeDtypeStruct((t_rows, C_INNER), BF16),
        compiler_params=_cparams(("parallel",)), name="ssd_out",
    )(act, dt, hin, z, pad(a_log), d_exp, norm_g.reshape(1, C_INNER))


def _state_to_rows(h):
    b = h.shape[0]
    return jnp.transpose(h, (0, 1, 4, 2, 3)).reshape(b, 2, C_STATE, C_INNER)


def _rows_to_state(h):
    b = h.shape[0]
    return jnp.transpose(h.reshape(b, 2, C_STATE, C_HEADS, C_HEAD_DIM), (0, 1, 3, 4, 2))


def _merge_kernel(a0_ref, a1_ref, a2_ref, w_ref, g0_ref, g1_ref, g2_ref, o_ref):
    acc = None
    for k, (a_ref, g_ref) in enumerate(((a0_ref, g0_ref), (a1_ref, g1_ref), (a2_ref, g2_ref))):
        term = g_ref[...].astype(F32) * jnp.dot(a_ref[...], w_ref[k], preferred_element_type=F32)
        acc = term if acc is None else acc + term
    o_ref[...] = acc.astype(o_ref.dtype)


def _merge(o_a, o_b, o_c, w_br, gates, n_ctx_rows):
    t_rows = o_a.shape[0]
    tm = _row_tile(n_ctx_rows, 512)
    tn = 1024
    nj = D_MODEL // tn
    a_spec = pl.BlockSpec((tm, o_a.shape[1]), lambda i, j: (i, 0))
    g_spec = lambda k: pl.BlockSpec((tm, tn), lambda i, j: (i, k * nj + j))
    return pl.pallas_call(
        _merge_kernel, grid=(t_rows // tm, nj),
        in_specs=[a_spec, a_spec, a_spec,
                  pl.BlockSpec((N_BRANCH, o_a.shape[1], tn), lambda i, j: (0, 0, j)),
                  g_spec(0), g_spec(1), g_spec(2)],
        out_specs=pl.BlockSpec((tm, tn), lambda i, j: (i, j)),
        out_shape=jax.ShapeDtypeStruct((t_rows, D_MODEL), BF16),
        compiler_params=_cparams(("parallel", "parallel")), name="merge",
    )(o_a, o_b, o_c, w_br, gates, gates, gates)


def _out_proj_kernel(a_ref, w_ref, x_ref, gate_ref, *refs):
    o_ref = refs[-1]
    o_ref[...] = x_ref[...] + gate_ref[0] * jnp.dot(a_ref[...], w_ref[...], preferred_element_type=F32)


def _out_proj(merged, w_o, x, gate, n_ctx_rows, *, first=0, n_rows=None, x_first=None, prev=None):
    t_rows = merged.shape[0]
    n_rows = t_rows if n_rows is None else n_rows
    tm = _row_tile(n_ctx_rows, 1024)
    tn = 1024
    off = first // tm
    x_off = off if x_first is None else x_first // tm
    args = [merged, w_o, x, gate]
    specs = [pl.BlockSpec((tm, D_MODEL), lambda i, j: (i + off, 0)),
             pl.BlockSpec((D_MODEL, tn), lambda i, j: (0, j)),
             pl.BlockSpec((tm, tn), lambda i, j: (i + x_off, j)),
             pl.BlockSpec((1, 1, tn), lambda i, j: (_group_of_tile(i + off, tm, n_ctx_rows), 0, j))]
    alias = _alias_kwargs(args, specs, None if prev is None else (prev,))
    return pl.pallas_call(
        _out_proj_kernel, grid=(n_rows // tm, D_MODEL // tn), in_specs=specs,
        out_specs=pl.BlockSpec((tm, tn), lambda i, j: (i + off, j)),
        out_shape=jax.ShapeDtypeStruct((t_rows, D_MODEL), F32),
        compiler_params=_cparams(("parallel", "parallel")), name="out_proj", **alias,
    )(*args)


def _moe_kernel(ea_ref, eb_ref, nv_ref, x_ref, meta_ref, ga_ref, ua_ref, da_ref, gb_ref, ub_ref, db_ref, o_ref):
    t = pl.program_id(0)

    @pl.when(t < nv_ref[0])
    def _():
        x = x_ref[...]

        def expert(g_ref, u_ref, d_ref, wcol):
            g = jnp.dot(x, g_ref[0, 0], preferred_element_type=F32)
            u = jnp.dot(x, u_ref[0, 0], preferred_element_type=F32)
            he = _silu(g) * u
            return wcol * jnp.dot(he.astype(BF16), d_ref[0, 0], preferred_element_type=F32)

        o = (expert(ga_ref, ua_ref, da_ref, meta_ref[:, 1:2])
             + expert(gb_ref, ub_ref, db_ref, meta_ref[:, 2:3]))
        o_ref[...] = o.astype(o_ref.dtype)

    @pl.when(t >= nv_ref[0])
    def _():
        o_ref[...] = jnp.zeros(o_ref.shape, o_ref.dtype)


def _moe_tile(n_rows):
    return 256 if n_rows >= 8192 else 128


def _moe_plan(bucket, tm):
    t_rows = bucket.shape[0]
    n_tiles = t_rows // tm + MOE_BUCKETS
    iota = jnp.arange(t_rows, dtype=jnp.int32)
    sorted_bucket, order = lax.sort((bucket, iota), num_keys=1)
    counts = jnp.sum((bucket[:, None] == jnp.arange(MOE_BUCKETS, dtype=jnp.int32)[None, :]).astype(jnp.int32), axis=0)
    padded = ((counts + tm - 1) // tm) * tm
    pad_end = jnp.cumsum(padded)
    pad_off = pad_end - padded
    off = jnp.cumsum(counts) - counts
    dest = pad_off[sorted_bucket] + iota - off[sorted_bucket]
    _, pos = lax.sort((order, dest), num_keys=1)
    tile_start = jnp.arange(n_tiles, dtype=jnp.int32) * tm
    tile_bucket = jnp.sum((tile_start[:, None] >= pad_end[None, :]).astype(jnp.int32), axis=1)
    last_bucket = jnp.max(jnp.where(counts > 0, jnp.arange(MOE_BUCKETS, dtype=jnp.int32), 0))
    tile_bucket = jnp.where(tile_start < pad_end[-1], tile_bucket, last_bucket)
    n_valid = (pad_end[-1] // tm).astype(jnp.int32).reshape(1)
    row_bucket = jnp.repeat(tile_bucket, tm)
    q = jnp.arange(n_tiles * tm, dtype=jnp.int32) - pad_off[row_bucket]
    src = order[jnp.clip(off[row_bucket] + q, 0, t_rows - 1)]
    pair_lo = jnp.asarray(np.array([0, 0, 0, 1, 1, 2], np.int32))
    pair_hi = jnp.asarray(np.array([1, 2, 3, 2, 3, 3], np.int32))
    grp = tile_bucket // MOE_PAIRS
    ea = grp * MOE_PER_GROUP + pair_lo[tile_bucket % MOE_PAIRS]
    eb = grp * MOE_PER_GROUP + pair_hi[tile_bucket % MOE_PAIRS]
    return src, pos, ea, eb, n_valid


def _moe(h2, meta, w_gate, w_up, w_down, layer):
    t_rows = h2.shape[0]
    tm = _moe_tile(t_rows)
    src, pos, ea, eb, n_valid = _moe_plan(meta[:, 0].astype(jnp.int32), tm)
    n_tiles = ea.shape[0]
    x_sorted = h2.at[src].get(mode="promise_in_bounds")
    meta_sorted = meta.at[src].get(mode="promise_in_bounds")
    up_spec = lambda sel: pl.BlockSpec((1, 1, D_MODEL, MOE_HIDDEN), sel)
    dn_spec = lambda sel: pl.BlockSpec((1, 1, MOE_HIDDEN, D_MODEL), sel)
    sel_a = lambda t, ea, eb, nv: (layer, ea[t], 0, 0)
    sel_b = lambda t, ea, eb, nv: (layer, eb[t], 0, 0)
    y_sorted = pl.pallas_call(
        _moe_kernel,
        grid_spec=pltpu.PrefetchScalarGridSpec(
            num_scalar_prefetch=3, grid=(n_tiles,),
            in_specs=[pl.BlockSpec((tm, D_MODEL), lambda t, ea, eb, nv: (t, 0)),
                      pl.BlockSpec((tm, LANES), lambda t, ea, eb, nv: (t, 0)),
                      up_spec(sel_a), up_spec(sel_a), dn_spec(sel_a),
                      up_spec(sel_b), up_spec(sel_b), dn_spec(sel_b)],
            out_specs=pl.BlockSpec((tm, D_MODEL), lambda t, ea, eb, nv: (t, 0))),
        out_shape=jax.ShapeDtypeStruct((n_tiles * tm, D_MODEL), BF16),
        compiler_params=_cparams(("arbitrary",)), name="moe",
    )(ea, eb, n_valid, x_sorted, meta_sorted, w_gate, w_up, w_down, w_gate, w_up, w_down)
    return y_sorted.at[pos].get(mode="promise_in_bounds")


def _vec(mod_l, k):
    return mod_l[:, k * D_MODEL:(k + 1) * D_MODEL].reshape(-1, 1, D_MODEL)


def kernel(x_prompt, x_sample, cache_k, cache_v, state_ssm, c, c_ctx, w_ada, b_ada, norm1_g, w_in, attn_sink,
           gm_w, gm_b, conv_w, conv_b, dt_bias, a_log, d_skip, ssm_norm_g, w_br, w_o, norm2_g, w_rg, b_rg,
           w_re, b_re, w_gate, w_up, w_down, final_g):
    depth = w_in.shape[0]
    n_ctx, n_lat = x_prompt.shape[0], x_sample.shape[0]
    assert x_prompt.shape[1:] == (SEQ, D_MODEL) and x_sample.shape[1:] == (DEC_SEQ, D_MODEL)
    n_ctx_rows = n_ctx * SEQ
    n_lat_rows = n_lat * DEC_SEQ
    t_rows = n_ctx_rows + n_lat_rows
    x_parts = ((x_prompt.reshape(n_ctx_rows, D_MODEL), 0, n_ctx_rows),
               (x_sample.reshape(n_lat_rows, D_MODEL), n_ctx_rows, n_lat_rows))
    x = None
    bf = lambda a: a.astype(BF16)
    w_gate_bf, w_up_bf, w_down_bf = bf(w_gate), bf(w_up), bf(w_down)

    n_cond = 1 + n_lat
    cond = jnp.zeros((((n_cond + 7) // 8) * 8, D_MODEL), F32).at[0].set(c_ctx).at[1:n_cond].set(c)
    mod = _adaln(cond, w_ada, b_ada)
    rope = _rope_tables()

    ks, vs, hs = [], [], []
    pending = None
    for l in range(depth):
        shift1, scale1, gate1, shift2, scale2, gate2 = (_vec(mod[l], k) for k in range(6))
        w = w_in[l]
        if x is None:
            h = None
            for part, first, n in x_parts:
                (h,) = _norm_stage(part, norm1_g[l], n_ctx_rows, first=first, n_rows=n, x_first=0,
                                   out_rows=t_rows, prev=None if h is None else (h,),
                                   mod=(1.0 + scale1, shift1))
        else:
            x, h = _norm_stage(x, norm1_g[l], n_ctx_rows, res=pending, mod=(1.0 + scale1, shift1), write_x=True)
        qkv = _proj(h, bf(w[:, OFF_Q:OFF_GU]), n_ctx_rows, kind="rope", rope=rope, tn=2 * KV_W)
        kv32 = _proj(h, bf(w[:, OFF_K:OFF_GU]), n_ctx_rows, out_dtype=F32, tn=2 * KV_W, row_range=(0, n_ctx_rows))
        uv = _proj(h, bf(w[:, OFF_GU:OFF_CZ]), n_ctx_rows)
        cz = _proj(h, bf(w[:, OFF_CZ:OFF_XBC]), n_ctx_rows)
        xbc = _proj(h, bf(w[:, OFF_XBC:OFF_DT]), n_ctx_rows, tn=C_XBC // 2)
        w_dt = jnp.pad(w[:, OFF_DT:OFF_GATES], ((0, 0), (0, LANES - 2 * C_HEADS)))
        dt_raw = _proj(h, bf(w_dt), n_ctx_rows, out_dtype=F32)
        gates = _proj(h, bf(w[:, OFF_GATES:]), n_ctx_rows, kind="sigmoid")

        k_ctx = bf(cache_k[:, l].reshape(n_lat, -1, KV_W))
        v_ctx = bf(cache_v[:, l].reshape(n_lat, -1, KV_W))
        o_a = _lat_attention(qkv, k_ctx, v_ctx, attn_sink[l], n_ctx_rows, n_lat)
        o_a = _ctx_attention(qkv, attn_sink[l], n_ctx, o_a)
        ks.append(kv32[:, :KV_W].reshape(n_ctx, SEQ, A_KV_HEADS, A_HEAD_DIM))
        vs.append(kv32[:, KV_W:].reshape(n_ctx, SEQ, A_KV_HEADS, A_HEAD_DIM))
        o_b = _gmlp(uv, gm_w[l], gm_b[l], n_ctx_rows)
        act, dt, st, dec = _ssd_local(xbc, dt_raw, conv_w[l], conv_b[l], dt_bias[l], a_log[l], n_ctx_rows)
        ncc = n_ctx_rows // C_CHUNK
        hin, _ = _ssd_scan(_state_to_rows(state_ssm[:, l]), st, dec, ncc, n_lat, DEC_SEQ // C_CHUNK)
        hin, hfin_c = _ssd_scan(jnp.zeros((n_ctx, 2, C_STATE, C_INNER), F32), st, dec, 0, n_ctx, SEQ // C_CHUNK,
                                prev=hin)
        o_c = _ssd_out(act, dt, hin, cz, a_log[l], d_skip[l], ssm_norm_g[l])
        hs.append(_rows_to_state(hfin_c))
        merged = _merge(o_a, o_b, o_c, bf(w_br[l]), gates, n_ctx_rows)
        if x is None:
            for part, first, n in x_parts:
                x = _out_proj(merged, bf(w_o[l]), part, gate1, n_ctx_rows, first=first, n_rows=n, x_first=0,
                              prev=x)
        else:
            x = _out_proj(merged, bf(w_o[l]), x, gate1, n_ctx_rows)
        w_r = bf(jnp.pad(jnp.concatenate([w_rg[l], w_re[l]], axis=1),
                         ((0, 0), (0, LANES - MOE_GROUPS - MOE_EXPERTS))))
        b_r = jnp.pad(jnp.concatenate([b_rg[l], b_re[l]]), (0, LANES - MOE_GROUPS - MOE_EXPERTS)).reshape(1, LANES)
        h2, meta = _norm_stage(x, norm2_g[l], n_ctx_rows, mod=(1.0 + scale2, shift2), router=(w_r, b_r))
        y_moe = _moe(h2, meta, w_gate_bf, w_up_bf, w_down_bf, l)
        pending = (y_moe, gate2)

    (y_prompt,) = _norm_stage(x, final_g, n_ctx_rows, n_rows=n_ctx_rows, res=pending, out_dtype=F32)
    (y_sample,) = _norm_stage(x, final_g, n_ctx_rows, first=n_ctx_rows, n_rows=n_lat_rows, res=pending,
                              out_dtype=F32)
    return (y_prompt.reshape(n_ctx, SEQ, D_MODEL), y_sample.reshape(n_lat, DEC_SEQ, D_MODEL),
            jnp.stack(ks, axis=1), jnp.stack(vs, axis=1), jnp.stack(hs, axis=1))
```

```python
import functools

import numpy as np
import jax
import jax.numpy as jnp
from jax import lax
from jax.experimental import pallas as pl
from jax.experimental.pallas import tpu as pltpu

F32 = jnp.float32
BF16 = jnp.bfloat16

D_MODEL = 2048
SEQ = 256
DEC_SEQ = 4096
GRID_W = 64
EPS = 1e-6
A_HEADS = 8
A_KV_HEADS = 2
A_HEAD_DIM = 128
A_GROUP = A_HEADS // A_KV_HEADS
A_BLOCK = 128
ROPE_BASE = 10000.0
ROPE_FREQS = A_HEAD_DIM // 4
B_WIDTH = 1024
B_GROUPS = 4
B_CHUNK = 128
B_GDIM = B_WIDTH // B_GROUPS
C_INNER = 1024
C_HEAD_DIM = 64
C_HEADS = C_INNER // C_HEAD_DIM
C_GROUPS = 2
C_REP = C_HEADS // C_GROUPS
C_STATE = 64
C_CHUNK = 128
C_XBC = C_INNER + 2 * C_GROUPS * C_STATE
N_BRANCH = 3
MOE_GROUPS = 4
MOE_PER_GROUP = 4
MOE_EXPERTS = 16
MOE_HIDDEN = 512
MOE_PAIRS = 6
MOE_BUCKETS = MOE_GROUPS * MOE_PAIRS
Q_W = A_HEADS * A_HEAD_DIM
KV_W = A_KV_HEADS * A_HEAD_DIM
OFF_Q = 0
OFF_K = OFF_Q + Q_W
OFF_V = OFF_K + KV_W
OFF_GU = OFF_V + KV_W
OFF_GV = OFF_GU + B_WIDTH
OFF_CZ = OFF_GV + B_WIDTH
OFF_XBC = OFF_CZ + C_INNER
OFF_DT = OFF_XBC + C_XBC
OFF_GATES = OFF_DT + 2 * C_HEADS
LANES = 128
SUBLANES = 8
LOG2_E = 1.4426950408889634
NEG_BIG = -1e30
V7X_VMEM_LIMIT_MB = 56


def _cparams(dims, vmem_mb=48):
    assert vmem_mb <= V7X_VMEM_LIMIT_MB
    return pltpu.CompilerParams(dimension_semantics=dims, vmem_limit_bytes=vmem_mb * 1024 * 1024)


def _silu(x):
    return x * jax.nn.sigmoid(x)


def _row_tile(n_ctx_rows, cap):
    for t in (1024, 512, 256, 128):
        if t <= cap and n_ctx_rows % t == 0 and DEC_SEQ % t == 0:
            return t
    raise ValueError("context rows must be a multiple of 128")


def _group_of_tile(i, tm, n_ctx_rows):
    nct = n_ctx_rows // tm
    per = DEC_SEQ // tm
    return jnp.where(i < nct, 0, 1 + (i - nct) // per)


def _adaln_kernel(c_ref, w_ref, b_ref, o_ref):
    s = _silu(c_ref[...])
    o_ref[0] = jnp.dot(s.astype(BF16), w_ref[0].astype(BF16), preferred_element_type=F32) + b_ref[0]


def _adaln(cond, w_ada, b_ada):
    depth = w_ada.shape[0]
    r = cond.shape[0]
    tn = 1024
    return pl.pallas_call(
        _adaln_kernel,
        grid=(depth, 6 * D_MODEL // tn),
        in_specs=[pl.BlockSpec((r, D_MODEL), lambda l, j: (0, 0)),
                  pl.BlockSpec((1, D_MODEL, tn), lambda l, j: (l, 0, j)),
                  pl.BlockSpec((1, 1, tn), lambda l, j: (l, 0, j))],
        out_specs=pl.BlockSpec((1, r, tn), lambda l, j: (l, 0, j)),
        out_shape=jax.ShapeDtypeStruct((depth, r, 6 * D_MODEL), F32),
        compiler_params=_cparams(("parallel", "parallel")),
        name="adaln",
    )(cond, w_ada, b_ada.reshape(depth, 1, 6 * D_MODEL))


def _route(logits):
    lane = lax.broadcasted_iota(jnp.int32, logits.shape, 1)
    is_g = lane < MOE_GROUPS
    lg = jnp.where(is_g, logits, NEG_BIG)
    gmax = jnp.max(lg, axis=1, keepdims=True)
    gidx = jnp.min(jnp.where(lg == gmax, lane, LANES), axis=1, keepdims=True)
    p_grp = 1.0 / jnp.sum(jnp.where(is_g, jnp.exp(lg - gmax), 0.0), axis=1, keepdims=True)
    e_id = lane - MOE_GROUPS
    in_grp = (e_id >= 0) & (e_id < MOE_EXPERTS) & ((e_id >> 2) == gidx)
    le = jnp.where(in_grp, logits, NEG_BIG)
    m1 = jnp.max(le, axis=1, keepdims=True)
    i1 = jnp.min(jnp.where(le == m1, lane, LANES), axis=1, keepdims=True)
    le2 = jnp.where(lane == i1, NEG_BIG, le)
    m2 = jnp.max(le2, axis=1, keepdims=True)
    i2 = jnp.min(jnp.where(le2 == m2, lane, LANES), axis=1, keepdims=True)
    e = jnp.exp(m2 - m1)
    w1 = p_grp / (1.0 + e)
    w2 = p_grp * e / (1.0 + e)
    first_lo = i1 < i2
    lo = jnp.where(first_lo, i1, i2) - MOE_GROUPS
    hi = jnp.where(first_lo, i2, i1) - MOE_GROUPS
    w_lo = jnp.where(first_lo, w1, w2)
    w_hi = jnp.where(first_lo, w2, w1)
    lo_l = lo & 3
    hi_l = hi & 3
    base = jnp.where(lo_l == 0, 0, jnp.where(lo_l == 1, 3, 5))
    bucket = gidx * MOE_PAIRS + base + hi_l - lo_l - 1
    return jnp.where(lane == 0, bucket.astype(F32),
                     jnp.where(lane == 1, w_lo, jnp.where(lane == 2, w_hi, 0.0)))


def _pack_bf16_pairs(x):
    w = x.shape[1] // 2
    lo = lax.bitcast_convert_type(x[:, :w].astype(F32), jnp.uint32)
    hi = lax.bitcast_convert_type(x[:, w:].astype(F32), jnp.uint32)
    return (lo >> 16) | hi


def _unpack_bf16_pairs(u):
    lo = lax.bitcast_convert_type(u << 16, F32)
    hi = lax.bitcast_convert_type(u & jnp.uint32(0xFFFF0000), F32)
    return jnp.concatenate([lo.astype(BF16), hi.astype(BF16)], axis=1)


def _pick_rows(i, n_ctx_tiles, ctx_ref, lat_ref):
    return jnp.where(i < n_ctx_tiles, ctx_ref[...], lat_ref[...])


def _pair_specs(block, n_ctx_tiles, tile=lambda *g: g[0], col=lambda *g: 0):
    return [pl.BlockSpec(block, lambda *g: (jnp.minimum(tile(*g), n_ctx_tiles - 1), col(*g))),
            pl.BlockSpec(block, lambda *g: (jnp.maximum(tile(*g) - n_ctx_tiles, 0), col(*g)))]


def _norm_kernel(*refs, has_res, has_mod, has_router, write_x, split_x, n_ctx_tiles):
    refs = list(refs)
    if split_x:
        x = _pick_rows(pl.program_id(0), n_ctx_tiles, refs.pop(0), refs.pop(0))
    else:
        x = refs.pop(0)[...]
    if has_res:
        y_ref = refs.pop(0)
        gate_ref = refs.pop(0)
        x = x + gate_ref[0] * y_ref[...].astype(F32)
    g_ref = refs.pop(0)
    if has_mod:
        sc_ref = refs.pop(0)
        sh_ref = refs.pop(0)
    if has_router:
        wr_ref = refs.pop(0)
        br_ref = refs.pop(0)
    if write_x:
        xo_ref = refs.pop(0)
        xo_ref[...] = x
    h_ref = refs.pop(0)
    h = x * lax.rsqrt(jnp.mean(x * x, axis=-1, keepdims=True) + EPS) * g_ref[...]
    if has_mod:
        h = h * sc_ref[0] + sh_ref[0]
    if has_router:
        h_bf = h.astype(BF16)
        packed = _pack_bf16_pairs(h_bf)
        n_tok = packed.shape[0]
        for k in range(SUBLANES):
            h_ref[pl.ds(k, n_tok, stride=SUBLANES), :] = packed[:, k * LANES:(k + 1) * LANES]
        meta_ref = refs.pop(0)
        logits = jnp.dot(h_bf, wr_ref[...], preferred_element_type=F32) + br_ref[...]
        meta_ref[...] = _route(logits)
    else:
        h_ref[...] = h.astype(h_ref.dtype)


def _norm_stage(x, norm_g, n_ctx_rows, *, first=0, n_rows=None, res=None, mod=None, router=None,
                write_x=False, out_dtype=BF16):
    tm = _row_tile(n_ctx_rows, 512)
    split_x = isinstance(x, tuple)
    if n_rows is None:
        n_rows = sum(a.shape[0] for a in x) if split_x else x.shape[0]
    off = first // tm
    grp = lambda i: _group_of_tile(i + off, tm, n_ctx_rows)
    vec_spec = pl.BlockSpec((1, 1, D_MODEL), lambda i: (grp(i), 0, 0))
    if split_x:
        assert first == 0
        args, specs = list(x), _pair_specs((tm, D_MODEL), n_ctx_rows // tm)
    else:
        args, specs = [x], [pl.BlockSpec((tm, D_MODEL), lambda i: (i + off, 0))]
    if res is not None:
        args += [res[0], res[1]]
        specs += [pl.BlockSpec((tm, D_MODEL), lambda i: (i + off, 0)), vec_spec]
    args.append(norm_g.reshape(1, D_MODEL))
    specs.append(pl.BlockSpec((1, D_MODEL), lambda i: (0, 0)))
    if mod is not None:
        args += [mod[0], mod[1]]
        specs += [vec_spec, vec_spec]
    if router is not None:
        args += [router[0], router[1]]
        specs += [pl.BlockSpec((D_MODEL, LANES), lambda i: (0, 0)), pl.BlockSpec((1, LANES), lambda i: (0, 0))]
    out_row = pl.BlockSpec((tm, D_MODEL), lambda i: (i, 0))
    outs, out_specs = [], []
    if write_x:
        outs.append(jax.ShapeDtypeStruct((n_rows, D_MODEL), F32))
        out_specs.append(out_row)
    if router is None:
        outs.append(jax.ShapeDtypeStruct((n_rows, D_MODEL), out_dtype))
        out_specs.append(out_row)
    else:
        assert D_MODEL // 2 == SUBLANES * LANES
        outs.append(jax.ShapeDtypeStruct((n_rows * SUBLANES, LANES), jnp.uint32))
        out_specs.append(pl.BlockSpec((tm * SUBLANES, LANES), lambda i: (i, 0)))
        outs.append(jax.ShapeDtypeStruct((n_rows, LANES), F32))
        out_specs.append(pl.BlockSpec((tm, LANES), lambda i: (i, 0)))
    kern = functools.partial(_norm_kernel, has_res=res is not None, has_mod=mod is not None,
                             has_router=router is not None, write_x=write_x, split_x=split_x,
                             n_ctx_tiles=n_ctx_rows // tm)
    return pl.pallas_call(
        kern, grid=(n_rows // tm,), in_specs=specs, out_specs=out_specs, out_shape=outs,
        compiler_params=_cparams(("parallel",)), name="norm_stage",
    )(*args)


def _rope_chunk(x, cos, sin):
    lane = lax.broadcasted_iota(jnp.int32, x.shape, 1)
    swapped = jnp.where((lane & 63) < ROPE_FREQS, pltpu.roll(x, 96, 1), pltpu.roll(x, 32, 1))
    return x * cos + swapped * sin


def _proj_kernel(a_ref, w_ref, *refs, kind, n_ctx_tiles):
    acc = jnp.dot(a_ref[...], w_ref[...], preferred_element_type=F32)
    if kind == "plain":
        (o_ref,) = refs
        o_ref[...] = acc.astype(o_ref.dtype)
    elif kind == "sigmoid":
        (o_ref,) = refs
        o_ref[...] = jax.nn.sigmoid(acc).astype(o_ref.dtype)
    else:
        cos_ref, sin_ref, o_ref = refs
        i = pl.program_id(0)
        j = pl.program_id(1)
        tn = acc.shape[1]
        n_q_tiles = Q_W // tn
        is_lat = i >= n_ctx_tiles
        cos = jnp.where(is_lat, cos_ref[...], 1.0)
        sin = jnp.where(is_lat, sin_ref[...], 0.0)

        def emit(n_rope):
            for c in range(tn // LANES):
                chunk = acc[:, c * LANES:(c + 1) * LANES]
                if c < n_rope:
                    chunk = _rope_chunk(chunk, cos, sin)
                o_ref[:, c * LANES:(c + 1) * LANES] = chunk.astype(o_ref.dtype)

        @pl.when(j < n_q_tiles)
        def _():
            emit(tn // LANES)

        @pl.when(j >= n_q_tiles)
        def _():
            emit(KV_W // LANES)


def _proj(a, w, n_ctx_rows, *, kind="plain", out_dtype=BF16, rope=None, tn=None, row_range=None):
    t_rows, k = a.shape
    n = w.shape[1]
    tm = _row_tile(n_ctx_rows, 1024)
    first, n_rows = (0, t_rows) if row_range is None else row_range
    off = first // tm
    if tn is None:
        tn = next(c for c in (1024, 768, 640, 512, 256, 128) if n % c == 0)
    args = [a, w]
    specs = [pl.BlockSpec((tm, k), lambda i, j: (i + off, 0)), pl.BlockSpec((k, tn), lambda i, j: (0, j))]
    if kind == "rope":
        assert tn == 2 * KV_W and n == Q_W + 2 * KV_W
        nct = n_ctx_rows // tm
        per = DEC_SEQ // tm
        pos_blk = lambda i, j: (jnp.where(i < nct, 0, (i - nct) % per), 0)
        args += [rope[0], rope[1]]
        specs += [pl.BlockSpec((tm, LANES), pos_blk), pl.BlockSpec((tm, LANES), pos_blk)]
    kern = functools.partial(_proj_kernel, kind=kind, n_ctx_tiles=n_ctx_rows // tm)
    return pl.pallas_call(
        kern, grid=(n_rows // tm, n // tn), in_specs=specs,
        out_specs=pl.BlockSpec((tm, tn), lambda i, j: (i, j)),
        out_shape=jax.ShapeDtypeStruct((n_rows, n), out_dtype),
        compiler_params=_cparams(("parallel", "parallel")), name="proj_" + kind,
    )(*args)


def _rope_tables():
    pos = np.arange(DEC_SEQ)
    row = (pos // GRID_W).astype(np.float32)
    col = (pos % GRID_W).astype(np.float32)
    inv = jnp.asarray(ROPE_BASE, F32) ** (-jnp.arange(ROPE_FREQS, dtype=F32) / ROPE_FREQS)
    ar = jnp.asarray(row)[:, None] * inv
    ac = jnp.asarray(col)[:, None] * inv
    cos = jnp.concatenate([jnp.cos(ar), jnp.cos(ar), jnp.cos(ac), jnp.cos(ac)], axis=1)
    sin = jnp.concatenate([-jnp.sin(ar), jnp.sin(ar), -jnp.sin(ac), jnp.sin(ac)], axis=1)
    return cos, sin


def _softmax_pv(s, sink_col, v):
    m = jnp.maximum(jnp.max(s, axis=1, keepdims=True), sink_col)
    p = jnp.exp(s - m)
    denom = jnp.sum(p, axis=1, keepdims=True) + jnp.exp(sink_col - m)
    o = jnp.dot(p.astype(BF16), v, preferred_element_type=F32)
    return o / denom


def _stack_heads(q):
    return jnp.concatenate([q[:, g * LANES:(g + 1) * LANES] for g in range(A_GROUP)], axis=0)


def _unstack_heads(o, rows):
    return jnp.concatenate([o[g * rows:(g + 1) * rows, :] for g in range(A_GROUP)], axis=1)


def _sink_col(sink_ref, h, rows):
    r = lax.broadcasted_iota(jnp.int32, (A_GROUP * rows, 1), 0)
    col = jnp.zeros((A_GROUP * rows, 1), F32)
    for g in range(A_GROUP):
        col = jnp.where((r >= g * rows) & (r < (g + 1) * rows), sink_ref[h * A_GROUP + g], col)
    return col


def _ctx_attn_kernel(sink_ref, q_ref, k_ref, v_ref, o_ref):
    h = pl.program_id(1)
    rows = q_ref.shape[0]
    q = _stack_heads(q_ref[...])
    s = lax.dot_general(q, k_ref[...], (((1,), (1,)), ((), ())), preferred_element_type=F32)
    s = s * (A_HEAD_DIM ** -0.5)
    o = _softmax_pv(s, _sink_col(sink_ref, h, rows), v_ref[...])
    o_ref[...] = _unstack_heads(o, rows).astype(o_ref.dtype)


def _ctx_attention(qkv, sink, n_seq):
    gw = A_GROUP * A_HEAD_DIM
    return pl.pallas_call(
        _ctx_attn_kernel,
        grid=(n_seq, A_KV_HEADS),
        in_specs=[pl.BlockSpec(memory_space=pltpu.MemorySpace.SMEM),
                  pl.BlockSpec((SEQ, gw), lambda s, h: (s, h)),
                  pl.BlockSpec((SEQ, A_HEAD_DIM), lambda s, h: (s, OFF_K // A_HEAD_DIM + h)),
                  pl.BlockSpec((SEQ, A_HEAD_DIM), lambda s, h: (s, OFF_V // A_HEAD_DIM + h))],
        out_specs=pl.BlockSpec((SEQ, gw), lambda s, h: (s, h)),
        out_shape=jax.ShapeDtypeStruct((n_seq * SEQ, Q_W), BF16),
        compiler_params=_cparams(("parallel", "parallel")), name="ctx_attn",
    )(sink, qkv, qkv, qkv)


def _lat_attn_kernel(sink_ref, bias_ref, q_ref, kc_ref, vc_ref, kp_ref, k0_ref, kn_ref,
                     vp_ref, v0_ref, vn_ref, o_ref):
    gw = A_GROUP * A_HEAD_DIM
    scale = A_HEAD_DIM ** -0.5
    nt = (((1,), (1,)), ((), ()))
    for h in range(A_KV_HEADS):
        hd = slice(h * A_HEAD_DIM, (h + 1) * A_HEAD_DIM)
        q = _stack_heads(q_ref[:, h * gw:(h + 1) * gw])
        keys = (kc_ref[0, :, hd], kp_ref[:, hd], k0_ref[:, hd], kn_ref[:, hd])
        vals = (vc_ref[0, :, hd], vp_ref[:, hd], v0_ref[:, hd], vn_ref[:, hd])
        s = [lax.dot_general(q, k, nt, preferred_element_type=F32) * scale for k in keys]
        s[1] = s[1] + bias_ref[0, :, :A_BLOCK]
        s[3] = s[3] + bias_ref[0, :, A_BLOCK:]
        sink = _sink_col(sink_ref, h, A_BLOCK)
        m = sink
        for part in s:
            m = jnp.maximum(m, jnp.max(part, axis=1, keepdims=True))
        denom = jnp.exp(sink - m)
        o = None
        for part, v in zip(s, vals):
            p = jnp.exp(part - m)
            denom = denom + jnp.sum(p, axis=1, keepdims=True)
            pv = jnp.dot(p.astype(BF16), v, preferred_element_type=F32)
            o = pv if o is None else o + pv
        o_ref[:, h * gw:(h + 1) * gw] = _unstack_heads(o / denom, A_BLOCK).astype(o_ref.dtype)


def _band_bias():
    ii = np.arange(A_BLOCK)[:, None]
    jj = np.arange(A_BLOCK)[None, :]
    prev = jj >= ii
    nxt = jj <= ii
    none = np.zeros((A_BLOCK, A_BLOCK), bool)
    variants = []
    for has_prev, has_next in ((False, True), (True, True), (True, False)):
        valid = np.concatenate([prev if has_prev else none, nxt if has_next else none], axis=1)
        variants.append(np.tile(np.where(valid, 0.0, NEG_BIG).astype(np.float32), (A_GROUP, 1)))
    return jnp.asarray(np.stack(variants))


def _lat_attention(qkv, k_ctx, v_ctx, sink, n_ctx_rows, n_lat):
    nb = DEC_SEQ // A_BLOCK
    base = n_ctx_rows // A_BLOCK
    n_ctx_keys = k_ctx.shape[1]
    kcol = OFF_K // KV_W
    vcol = OFF_V // KV_W
    row = lambda b, i: base + b * nb + i
    prev = lambda i: jnp.maximum(i - 1, 0)
    nxt = lambda i: jnp.minimum(i + 1, nb - 1)
    blk = (A_BLOCK, KV_W)
    ctx_blk = (1, n_ctx_keys, KV_W)
    bias_sel = lambda b, i: (jnp.where(i == 0, 0, jnp.where(i == nb - 1, 2, 1)), 0, 0)
    return pl.pallas_call(
        _lat_attn_kernel,
        grid=(n_lat, nb),
        in_specs=[pl.BlockSpec(memory_space=pltpu.MemorySpace.SMEM),
                  pl.BlockSpec((1, A_GROUP * A_BLOCK, 2 * A_BLOCK), bias_sel),
                  pl.BlockSpec((A_BLOCK, Q_W), lambda b, i: (row(b, i), 0)),
                  pl.BlockSpec(ctx_blk, lambda b, i: (b, 0, 0)),
                  pl.BlockSpec(ctx_blk, lambda b, i: (b, 0, 0)),
                  pl.BlockSpec(blk, lambda b, i: (row(b, prev(i)), kcol)),
                  pl.BlockSpec(blk, lambda b, i: (row(b, i), kcol)),
                  pl.BlockSpec(blk, lambda b, i: (row(b, nxt(i)), kcol)),
                  pl.BlockSpec(blk, lambda b, i: (row(b, prev(i)), vcol)),
                  pl.BlockSpec(blk, lambda b, i: (row(b, i), vcol)),
                  pl.BlockSpec(blk, lambda b, i: (row(b, nxt(i)), vcol))],
        out_specs=pl.BlockSpec((A_BLOCK, Q_W), lambda b, i: (b * nb + i, 0)),
        out_shape=jax.ShapeDtypeStruct((n_lat * DEC_SEQ, Q_W), BF16),
        compiler_params=_cparams(("parallel", "parallel")), name="lat_attn",
    )(sink, _band_bias(), qkv, k_ctx, v_ctx, qkv, qkv, qkv, qkv, qkv, qkv)


def _gmlp_kernel(uv_u_ref, uv_v_ref, w_ref, b_ref, o_ref):
    v = uv_v_ref[...].astype(F32)
    mu = jnp.mean(v, axis=-1, keepdims=True)
    vc = v - mu
    var = jnp.mean(vc * vc, axis=-1, keepdims=True)
    vn = (vc * lax.rsqrt(var + EPS)).astype(BF16)
    tm = v.shape[0]
    for c in range(tm // B_CHUNK):
        rows = slice(c * B_CHUNK, (c + 1) * B_CHUNK)
        for g in range(B_GROUPS):
            cols = slice(g * B_GDIM, (g + 1) * B_GDIM)
            mixed = jnp.dot(w_ref[g], vn[rows, cols], preferred_element_type=F32) + b_ref[g]
            o_ref[rows, cols] = (uv_u_ref[rows, cols].astype(F32) * mixed).astype(o_ref.dtype)


def _gmlp(uv, gm_w, gm_b, n_ctx_rows):
    t_rows = uv.shape[0]
    tm = _row_tile(n_ctx_rows, 512)
    return pl.pallas_call(
        _gmlp_kernel, grid=(t_rows // tm,),
        in_specs=[pl.BlockSpec((tm, B_WIDTH), lambda i: (i, 0)),
                  pl.BlockSpec((tm, B_WIDTH), lambda i: (i, 1)),
                  pl.BlockSpec((B_GROUPS, B_CHUNK, B_CHUNK), lambda i: (0, 0, 0)),
                  pl.BlockSpec((B_GROUPS, B_CHUNK, 1), lambda i: (0, 0, 0))],
        out_specs=pl.BlockSpec((tm, B_WIDTH), lambda i: (i, 0)),
        out_shape=jax.ShapeDtypeStruct((t_rows, B_WIDTH), BF16),
        compiler_params=_cparams(("parallel",)), name="gmlp",
    )(uv, uv, gm_w.astype(BF16), gm_b.reshape(B_GROUPS, B_CHUNK, 1))


def _split3(x):
    hi = x.astype(BF16)
    r1 = x - hi.astype(F32)
    mid = r1.astype(BF16)
    lo = (r1 - mid.astype(F32)).astype(BF16)
    return hi, mid, lo


def _dot3_rhs(sel, x):
    return sum(jnp.dot(sel, p, preferred_element_type=F32) for p in _split3(x))


def _dot3_lhs(x, sel):
    return sum(jnp.dot(p, sel, preferred_element_type=F32) for p in _split3(x))


def _dot2_lhs(x, sel):
    return sum(jnp.dot(p, sel, preferred_element_type=F32) for p in _split3(x)[:2])


def _tri(lower):
    i = lax.broadcasted_iota(jnp.int32, (C_CHUNK, C_CHUNK), 0)
    j = lax.broadcasted_iota(jnp.int32, (C_CHUNK, C_CHUNK), 1)
    return (j <= i) if lower else (j >= i)


def _head_expand(d):
    r = lax.broadcasted_iota(jnp.int32, (LANES, C_INNER), 0)
    c = lax.broadcasted_iota(jnp.int32, (LANES, C_INNER), 1)
    return (r == (c >> 6) + C_HEADS * d).astype(BF16)


def _chunk_cumsums(dt, a_row):
    a = dt * a_row
    lane = lax.broadcasted_iota(jnp.int32, a.shape, 1)
    pre = _dot3_rhs(_tri(True).astype(BF16), a)
    suf = _dot3_rhs(_tri(False).astype(BF16), a)
    cs = jnp.where(lane < C_HEADS, pre, suf)
    tot = jnp.where(lane[0:1, :] < C_HEADS, pre[C_CHUNK - 1:C_CHUNK, :], suf[0:1, :])
    return cs, tot


SSD_STEP_ROWS = 2 * C_CHUNK
assert SEQ % SSD_STEP_ROWS == 0 and DEC_SEQ % SSD_STEP_ROWS == 0


def _ssd_local_kernel(x_ref, hp_ref, hn_ref, dtr_ref, cw_ref, cb_ref, dtb_ref, alog_ref,
                      act_ref, dt_ref, st_ref, dec_ref, *, n_ctx_blocks):
    b = pl.program_id(0)
    rows = x_ref.shape[0]
    ctx_per = SEQ // rows
    lat_per = DEC_SEQ // rows
    in_seq = jnp.where(b < n_ctx_blocks, b % ctx_per, (b - n_ctx_blocks) % lat_per)
    per = jnp.where(b < n_ctx_blocks, ctx_per, lat_per)
    first = in_seq == 0
    last = in_seq == per - 1
    x = x_ref[...].astype(F32)
    halo = hp_ref.shape[0]
    prev_row = jnp.where(first, 0.0, hp_ref[...].astype(F32)[halo - 1:halo, :])
    next_row = jnp.where(last, 0.0, hn_ref[...].astype(F32)[0:1, :])
    r = lax.broadcasted_iota(jnp.int32, (SUBLANES, x.shape[1]), 0)
    down = pltpu.roll(x, 1, 0)
    up = pltpu.roll(x, rows - 1, 0)
    x_m1 = jnp.concatenate([jnp.where(r == 0, prev_row, down[:SUBLANES]), down[SUBLANES:]], axis=0)
    x_p1 = jnp.concatenate([up[:rows - SUBLANES], jnp.where(r == SUBLANES - 1, next_row, up[rows - SUBLANES:])],
                           axis=0)
    act = _silu(cw_ref[0:1, :] * x_m1 + cw_ref[1:2, :] * x + cw_ref[2:3, :] * x_p1 + cb_ref[...])
    act_bf = act.astype(BF16)
    act_ref[...] = act_bf
    z = dtr_ref[...] + dtb_ref[...]
    dt_all = jnp.maximum(z, 0.0) + jnp.log1p(jnp.exp(-jnp.abs(z)))
    dt_ref[...] = dt_all
    a_row = -jnp.exp(alog_ref[...])
    for s in range(rows // C_CHUNK):
        tok = slice(s * C_CHUNK, (s + 1) * C_CHUNK)
        dt = dt_all[tok]
        cs, tot = _chunk_cumsums(dt, a_row)
        w = jnp.exp(tot - cs) * dt
        xs = act_bf[tok, :C_INNER].astype(F32)
        bmat_t = jnp.transpose(act_bf[tok, C_INNER:C_INNER + LANES].astype(F32))
        for d in range(2):
            e = _head_expand(d)
            xw = (xs * _dot2_lhs(w, e)).astype(BF16)
            dec_ref[s, d] = jnp.exp(_dot3_lhs(tot, e))
            for g in range(C_GROUPS):
                bt = bmat_t[g * C_STATE:(g + 1) * C_STATE, :].astype(BF16)
                cols = slice(g * C_REP * C_HEAD_DIM, (g + 1) * C_REP * C_HEAD_DIM)
                st_ref[s, d, :, cols] = jnp.dot(bt, xw[:, cols], preferred_element_type=F32)


def _ssd_local(xbc, dt_raw, conv_w, conv_b, dt_bias, a_log, n_ctx_rows):
    t_rows = xbc.shape[0]
    rows = SSD_STEP_ROWS
    n_sub = rows // C_CHUNK
    nb = t_rows // rows
    nc = t_rows // C_CHUNK
    halo = 16
    per = rows // halo
    pad = lambda v: jnp.pad(v.reshape(1, -1), ((0, 0), (0, LANES - v.size)))
    kern = functools.partial(_ssd_local_kernel, n_ctx_blocks=n_ctx_rows // rows)
    row = lambda w: pl.BlockSpec((rows, w), lambda c: (c, 0))
    const = lambda shape: pl.BlockSpec(shape, lambda c: (0,) * len(shape))
    return pl.pallas_call(
        kern, grid=(nb,),
        in_specs=[row(C_XBC),
                  pl.BlockSpec((halo, C_XBC), lambda c: (jnp.maximum(c * per - 1, 0), 0)),
                  pl.BlockSpec((halo, C_XBC), lambda c: (jnp.minimum((c + 1) * per, nb * per - 1), 0)),
                  row(LANES), const((3, C_XBC)), const((1, C_XBC)), const((1, LANES)), const((1, LANES))],
        out_specs=[row(C_XBC), row(LANES),
                   pl.BlockSpec((n_sub, 2, C_STATE, C_INNER), lambda c: (c, 0, 0, 0)),
                   pl.BlockSpec((n_sub, 2, 1, C_INNER), lambda c: (c, 0, 0, 0))],
        out_shape=[jax.ShapeDtypeStruct((t_rows, C_XBC), BF16),
                   jax.ShapeDtypeStruct((t_rows, LANES), F32),
                   jax.ShapeDtypeStruct((nc, 2, C_STATE, C_INNER), F32),
                   jax.ShapeDtypeStruct((nc, 2, 1, C_INNER), F32)],
        compiler_params=_cparams(("parallel",)), name="ssd_local",
    )(xbc, xbc, xbc, dt_raw, conv_w, conv_b.reshape(1, C_XBC), pad(dt_bias), pad(a_log))


def _ssd_scan_kernel(h0_ref, st_ref, dec_ref, *refs, n_chunks):
    hin_ref, hfin_ref = refs[-2:]
    d = pl.program_id(1)

    def run(order):
        for j in range(C_INNER // LANES):
            cols = slice(j * LANES, (j + 1) * LANES)
            h = h0_ref[0, 0, :, cols]
            for c in order:
                hin_ref[c, 0, :, cols] = h
                h = h * dec_ref[c, 0, :, cols] + st_ref[c, 0, :, cols]
            hfin_ref[0, 0, :, cols] = h

    @pl.when(d == 0)
    def _():
        run(range(n_chunks))

    @pl.when(d == 1)
    def _():
        run(range(n_chunks - 1, -1, -1))


def _ssd_scan(h0, st, dec, first_chunk, n_seq, n_chunks):
    assert first_chunk % n_chunks == 0
    base = first_chunk // n_chunks
    blk = (n_chunks, 1, C_STATE, C_INNER)
    one = (1, 1, C_STATE, C_INNER)
    return pl.pallas_call(
        functools.partial(_ssd_scan_kernel, n_chunks=n_chunks),
        grid=(n_seq, 2),
        in_specs=[pl.BlockSpec(one, lambda s, d: (s, d, 0, 0)),
                  pl.BlockSpec(blk, lambda s, d: (base + s, d, 0, 0)),
                  pl.BlockSpec((n_chunks, 1, 1, C_INNER), lambda s, d: (base + s, d, 0, 0))],
        out_specs=[pl.BlockSpec(blk, lambda s, d: (s, d, 0, 0)),
                   pl.BlockSpec(one, lambda s, d: (s, d, 0, 0))],
        out_shape=[jax.ShapeDtypeStruct((n_seq * n_chunks, 2, C_STATE, C_INNER), F32),
                   jax.ShapeDtypeStruct((n_seq, 2, C_STATE, C_INNER), F32)],
        compiler_params=_cparams(("parallel", "parallel")), name="ssd_scan",
    )(h0, st, dec)


def _ssd_out_kernel(act_ref, dt_ref, hin_ctx_ref, hin_lat_ref, z_ref, alog_ref, dskip_ref, ng_ref, o_ref, *,
                    n_ctx_blocks):
    a_row = -jnp.exp(alog_ref[...])
    is_ctx = pl.program_id(0) < n_ctx_blocks
    for s in range(act_ref.shape[0] // C_CHUNK):
        tok = slice(s * C_CHUNK, (s + 1) * C_CHUNK)
        hin = lambda d, cols, s=s: jnp.where(is_ctx, hin_ctx_ref[s, d, :, cols], hin_lat_ref[s, d, :, cols])
        o = _ssd_out_chunk(act_ref[tok, :], dt_ref[tok, :], hin, z_ref[tok, :], a_row,
                           dskip_ref[...], ng_ref[...])
        o_ref[tok, :] = o.astype(o_ref.dtype)


def _ssd_out_chunk(act, dt, hin, z, a_row, dskip, norm_g):
    cs, _ = _chunk_cumsums(dt, a_row)
    cs2 = cs * LOG2_E
    cl_t = jnp.transpose(cs2 - jnp.log2(dt))
    x_bf = act[:, :C_INNER]
    bm = act[:, C_INNER:C_INNER + C_GROUPS * C_STATE]
    cm = act[:, C_INNER + C_GROUPS * C_STATE:]
    tril = _tri(True)
    triu = _tri(False)
    lane = lax.broadcasted_iota(jnp.int32, (C_CHUNK, LANES), 1)
    y_parts = []
    for g in range(C_GROUPS):
        ns = slice(g * C_STATE, (g + 1) * C_STATE)
        cb = lax.dot_general(cm[:, ns], bm[:, ns], (((1,), (1,)), ((), ())), preferred_element_type=F32)
        for hp in range(C_REP // 2):
            pair = []
            for h in (g * C_REP + 2 * hp, g * C_REP + 2 * hp + 1):
                hb = h + C_HEADS
                lf = jnp.exp2(jnp.where(tril, cs2[:, h:h + 1] - cl_t[h:h + 1, :], NEG_BIG))
                lb = jnp.exp2(jnp.where(triu, cs2[:, hb:hb + 1] - cl_t[hb:hb + 1, :], NEG_BIG))
                pair.append((cb * (lf + lb)).astype(BF16))
            cols = slice((g * C_REP + 2 * hp) * C_HEAD_DIM, (g * C_REP + 2 * hp + 2) * C_HEAD_DIM)
            xp = x_bf[:, cols]
            y0 = jnp.dot(pair[0], xp, preferred_element_type=F32)
            y1 = jnp.dot(pair[1], xp, preferred_element_type=F32)
            y_parts.append(jnp.where(lane < C_HEAD_DIM, y0, y1))
    y = jnp.concatenate(y_parts, axis=1)
    grow_small = jnp.exp2(cs2)
    for d in range(2):
        grow = _dot2_lhs(grow_small, _head_expand(d))
        yo = []
        for g in range(C_GROUPS):
            ns = slice(g * C_STATE, (g + 1) * C_STATE)
            cols = slice(g * C_REP * C_HEAD_DIM, (g + 1) * C_REP * C_HEAD_DIM)
            yo.append(jnp.dot(cm[:, ns], hin(d, cols).astype(BF16), preferred_element_type=F32))
        y = y + jnp.concatenate(yo, axis=1) * grow
    y = y + dskip * x_bf.astype(F32)
    o = y * _silu(z.astype(F32))
    return o * lax.rsqrt(jnp.mean(o * o, axis=-1, keepdims=True) + EPS) * norm_g


def _ssd_out(act, dt, hin_ctx, hin_lat, z, a_log, d_skip, norm_g, n_ctx_rows):
    t_rows = act.shape[0]
    rows = SSD_STEP_ROWS
    ncb = n_ctx_rows // rows
    pad = lambda v: jnp.pad(v.reshape(1, -1), ((0, 0), (0, LANES - v.size)))
    row = lambda w: pl.BlockSpec((rows, w), lambda c: (c, 0))
    const = lambda shape: pl.BlockSpec(shape, lambda c: (0,) * len(shape))
    hin_blk = (rows // C_CHUNK, 2, C_STATE, C_INNER)
    d_exp = jnp.repeat(d_skip, C_HEAD_DIM).reshape(1, C_INNER)
    return pl.pallas_call(
        functools.partial(_ssd_out_kernel, n_ctx_blocks=ncb), grid=(t_rows // rows,),
        in_specs=[row(C_XBC), row(LANES),
                  pl.BlockSpec(hin_blk, lambda c: (jnp.minimum(c, ncb - 1), 0, 0, 0)),
                  pl.BlockSpec(hin_blk, lambda c: (jnp.maximum(c - ncb, 0), 0, 0, 0)),
                  row(C_INNER), const((1, LANES)), const((1, C_INNER)), const((1, C_INNER))],
        out_specs=row(C_INNER),
        out_shape=jax.ShapeDtypeStruct((t_rows, C_INNER), BF16),
        compiler_params=_cparams(("parallel",)), name="ssd_out",
    )(act, dt, hin_ctx, hin_lat, z, pad(a_log), d_exp, norm_g.reshape(1, C_INNER))


def _state_to_rows(h):
    b = h.shape[0]
    return jnp.transpose(h, (0, 1, 4, 2, 3)).reshape(b, 2, C_STATE, C_INNER)


def _rows_to_state(h):
    b = h.shape[0]
    return jnp.transpose(h.reshape(b, 2, C_STATE, C_HEADS, C_HEAD_DIM), (0, 1, 3, 4, 2))


def _merge_kernel(a0c_ref, a0l_ref, a1_ref, a2_ref, w_ref, g0_ref, g1_ref, g2_ref, o_ref, *, n_ctx_tiles):
    a0 = _pick_rows(pl.program_id(0), n_ctx_tiles, a0c_ref, a0l_ref)
    acc = None
    for k, (a, g_ref) in enumerate(((a0, g0_ref), (a1_ref[...], g1_ref), (a2_ref[...], g2_ref))):
        term = g_ref[...].astype(F32) * jnp.dot(a, w_ref[k], preferred_element_type=F32)
        acc = term if acc is None else acc + term
    o_ref[...] = acc.astype(o_ref.dtype)


def _merge(o_a_ctx, o_a_lat, o_b, o_c, w_br, gates, n_ctx_rows):
    t_rows, width = o_b.shape
    tm = _row_tile(n_ctx_rows, 512)
    tn = 1024
    nj = D_MODEL // tn
    a_spec = pl.BlockSpec((tm, width), lambda i, j: (i, 0))
    g_spec = lambda k: pl.BlockSpec((tm, tn), lambda i, j: (i, k * nj + j))
    return pl.pallas_call(
        functools.partial(_merge_kernel, n_ctx_tiles=n_ctx_rows // tm), grid=(t_rows // tm, nj),
        in_specs=_pair_specs((tm, width), n_ctx_rows // tm) + [
            a_spec, a_spec, pl.BlockSpec((N_BRANCH, width, tn), lambda i, j: (0, 0, j)),
            g_spec(0), g_spec(1), g_spec(2)],
        out_specs=pl.BlockSpec((tm, tn), lambda i, j: (i, j)),
        out_shape=jax.ShapeDtypeStruct((t_rows, D_MODEL), BF16),
        compiler_params=_cparams(("parallel", "parallel")), name="merge",
    )(o_a_ctx, o_a_lat, o_b, o_c, w_br, gates, gates, gates)


def _out_proj_kernel(a_ref, w_ref, *refs, split_x, n_ctx_tiles):
    refs = list(refs)
    if split_x:
        x = _pick_rows(pl.program_id(0), n_ctx_tiles, refs.pop(0), refs.pop(0))
    else:
        x = refs.pop(0)[...]
    gate_ref, o_ref = refs
    o_ref[...] = x + gate_ref[0] * jnp.dot(a_ref[...], w_ref[...], preferred_element_type=F32)


def _out_proj(merged, w_o, x, gate, n_ctx_rows):
    t_rows = merged.shape[0]
    tm = _row_tile(n_ctx_rows, 1024)
    tn = 1024
    nct = n_ctx_rows // tm
    split_x = isinstance(x, tuple)
    if split_x:
        x_args = list(x)
        x_specs = _pair_specs((tm, tn), nct, col=lambda i, j: j)
    else:
        x_args, x_specs = [x], [pl.BlockSpec((tm, tn), lambda i, j: (i, j))]
    return pl.pallas_call(
        functools.partial(_out_proj_kernel, split_x=split_x, n_ctx_tiles=nct),
        grid=(t_rows // tm, D_MODEL // tn),
        in_specs=[pl.BlockSpec((tm, D_MODEL), lambda i, j: (i, 0)),
                  pl.BlockSpec((D_MODEL, tn), lambda i, j: (0, j))] + x_specs + [
                  pl.BlockSpec((1, 1, tn), lambda i, j: (_group_of_tile(i, tm, n_ctx_rows), 0, j))],
        out_specs=pl.BlockSpec((tm, tn), lambda i, j: (i, j)),
        out_shape=jax.ShapeDtypeStruct((t_rows, D_MODEL), F32),
        compiler_params=_cparams(("parallel", "parallel")), name="out_proj",
    )(merged, w_o, *x_args, gate)


def _moe_kernel(src_ref, ea_ref, eb_ref, nv_ref, h_hbm, meta_ref, ga_ref, ua_ref, da_ref, gb_ref, ub_ref, db_ref,
                o_ref, xbuf, sem):
    t = pl.program_id(0)
    tm = xbuf.shape[1] // SUBLANES
    slot = t % 2

    def gather(tile, dst_slot):
        for r in range(tm):
            row = pl.multiple_of(src_ref[tile * tm + r], SUBLANES)
            pltpu.make_async_copy(h_hbm.at[pl.ds(row, SUBLANES), :],
                                  xbuf.at[dst_slot, pl.ds(r * SUBLANES, SUBLANES), :],
                                  sem.at[dst_slot]).start()

    @pl.when(t == 0)
    def _():
        gather(0, 0)

    @pl.when(t + 1 < nv_ref[0])
    def _():
        gather(t + 1, 1 - slot)

    @pl.when(t < nv_ref[0])
    def _():
        pltpu.make_async_copy(h_hbm.at[pl.ds(0, tm * SUBLANES), :], xbuf.at[slot], sem.at[slot]).wait()
        x = _unpack_bf16_pairs(jnp.concatenate(
            [xbuf[slot, pl.ds(k, tm, stride=SUBLANES), :] for k in range(SUBLANES)], axis=1))

        def expert(g_ref, u_ref, d_ref, wcol):
            g = jnp.dot(x, g_ref[0, 0], preferred_element_type=F32)
            u = jnp.dot(x, u_ref[0, 0], preferred_element_type=F32)
            he = _silu(g) * u
            return wcol * jnp.dot(he.astype(BF16), d_ref[0, 0], preferred_element_type=F32)

        o = (expert(ga_ref, ua_ref, da_ref, meta_ref[:, 1:2])
             + expert(gb_ref, ub_ref, db_ref, meta_ref[:, 2:3]))
        o_ref[...] = o.astype(o_ref.dtype)

    @pl.when(t >= nv_ref[0])
    def _():
        o_ref[...] = jnp.zeros(o_ref.shape, o_ref.dtype)


def _moe_tile(n_rows):
    return 256 if n_rows >= 8192 else 128


def _moe_plan(bucket, tm):
    t_rows = bucket.shape[0]
    n_tiles = t_rows // tm + MOE_BUCKETS
    iota = jnp.arange(t_rows, dtype=jnp.int32)
    sorted_bucket, order = lax.sort((bucket, iota), num_keys=1)
    counts = jnp.sum((bucket[:, None] == jnp.arange(MOE_BUCKETS, dtype=jnp.int32)[None, :]).astype(jnp.int32), axis=0)
    padded = ((counts + tm - 1) // tm) * tm
    pad_end = jnp.cumsum(padded)
    pad_off = pad_end - padded
    off = jnp.cumsum(counts) - counts
    dest = pad_off[sorted_bucket] + iota - off[sorted_bucket]
    _, pos = lax.sort((order, dest), num_keys=1)
    tile_start = jnp.arange(n_tiles, dtype=jnp.int32) * tm
    tile_bucket = jnp.sum((tile_start[:, None] >= pad_end[None, :]).astype(jnp.int32), axis=1)
    last_bucket = jnp.max(jnp.where(counts > 0, jnp.arange(MOE_BUCKETS, dtype=jnp.int32), 0))
    tile_bucket = jnp.where(tile_start < pad_end[-1], tile_bucket, last_bucket)
    n_valid = (pad_end[-1] // tm).astype(jnp.int32).reshape(1)
    row_bucket = jnp.repeat(tile_bucket, tm)
    q = jnp.arange(n_tiles * tm, dtype=jnp.int32) - pad_off[row_bucket]
    src = order[jnp.clip(off[row_bucket] + q, 0, t_rows - 1)]
    pair_lo = jnp.asarray(np.array([0, 0, 0, 1, 1, 2], np.int32))
    pair_hi = jnp.asarray(np.array([1, 2, 3, 2, 3, 3], np.int32))
    grp = tile_bucket // MOE_PAIRS
    ea = grp * MOE_PER_GROUP + pair_lo[tile_bucket % MOE_PAIRS]
    eb = grp * MOE_PER_GROUP + pair_hi[tile_bucket % MOE_PAIRS]
    return src, pos, ea, eb, n_valid


def _moe(h2, meta, w_gate, w_up, w_down, layer):
    t_rows = meta.shape[0]
    tm = _moe_tile(t_rows)
    src, pos, ea, eb, n_valid = _moe_plan(meta[:, 0].astype(jnp.int32), tm)
    n_tiles = ea.shape[0]
    meta_sorted = meta.at[src].get(mode="promise_in_bounds")
    up_spec = lambda sel: pl.BlockSpec((1, 1, D_MODEL, MOE_HIDDEN), sel)
    dn_spec = lambda sel: pl.BlockSpec((1, 1, MOE_HIDDEN, D_MODEL), sel)
    sel_a = lambda t, src, ea, eb, nv: (layer, ea[t], 0, 0)
    sel_b = lambda t, src, ea, eb, nv: (layer, eb[t], 0, 0)
    y_sorted = pl.pallas_call(
        _moe_kernel,
        grid_spec=pltpu.PrefetchScalarGridSpec(
            num_scalar_prefetch=4, grid=(n_tiles,),
            in_specs=[pl.BlockSpec(memory_space=pl.ANY),
                      pl.BlockSpec((tm, LANES), lambda t, src, ea, eb, nv: (t, 0)),
                      up_spec(sel_a), up_spec(sel_a), dn_spec(sel_a),
                      up_spec(sel_b), up_spec(sel_b), dn_spec(sel_b)],
            out_specs=pl.BlockSpec((tm, D_MODEL), lambda t, src, ea, eb, nv: (t, 0)),
            scratch_shapes=[pltpu.VMEM((2, tm * SUBLANES, LANES), jnp.uint32), pltpu.SemaphoreType.DMA((2,))]),
        out_shape=jax.ShapeDtypeStruct((n_tiles * tm, D_MODEL), BF16),
        compiler_params=_cparams(("arbitrary",)), name="moe",
    )(src * SUBLANES, ea, eb, n_valid, h2, meta_sorted, w_gate, w_up, w_down, w_gate, w_up, w_down)
    return y_sorted.at[pos].get(mode="promise_in_bounds")


def _vec(mod_l, k):
    return mod_l[:, k * D_MODEL:(k + 1) * D_MODEL].reshape(-1, 1, D_MODEL)


def kernel(x_prompt, x_sample, cache_k, cache_v, state_ssm, c, c_ctx, w_ada, b_ada, norm1_g, w_in, attn_sink,
           gm_w, gm_b, conv_w, conv_b, dt_bias, a_log, d_skip, ssm_norm_g, w_br, w_o, norm2_g, w_rg, b_rg,
           w_re, b_re, w_gate, w_up, w_down, final_g):
    depth = w_in.shape[0]
    n_ctx, n_lat = x_prompt.shape[0], x_sample.shape[0]
    assert x_prompt.shape[1:] == (SEQ, D_MODEL) and x_sample.shape[1:] == (DEC_SEQ, D_MODEL)
    n_ctx_rows = n_ctx * SEQ
    n_lat_rows = n_lat * DEC_SEQ
    x_parts = (x_prompt.reshape(n_ctx_rows, D_MODEL), x_sample.reshape(n_lat_rows, D_MODEL))
    x = None
    bf = lambda a: a.astype(BF16)
    w_gate_bf, w_up_bf, w_down_bf = bf(w_gate), bf(w_up), bf(w_down)

    n_cond = 1 + n_lat
    cond = jnp.zeros((((n_cond + 7) // 8) * 8, D_MODEL), F32).at[0].set(c_ctx).at[1:n_cond].set(c)
    mod = _adaln(cond, w_ada, b_ada)
    rope = _rope_tables()

    ks, vs, hs = [], [], []
    pending = None
    for l in range(depth):
        shift1, scale1, gate1, shift2, scale2, gate2 = (_vec(mod[l], k) for k in range(6))
        w = w_in[l]
        if x is None:
            (h,) = _norm_stage(x_parts, norm1_g[l], n_ctx_rows, mod=(1.0 + scale1, shift1))
        else:
            x, h = _norm_stage(x, norm1_g[l], n_ctx_rows, res=pending, mod=(1.0 + scale1, shift1), write_x=True)
        qkv = _proj(h, bf(w[:, OFF_Q:OFF_GU]), n_ctx_rows, kind="rope", rope=rope, tn=2 * KV_W)
        kv32 = _proj(h, bf(w[:, OFF_K:OFF_GU]), n_ctx_rows, out_dtype=F32, tn=2 * KV_W, row_range=(0, n_ctx_rows))
        uv = _proj(h, bf(w[:, OFF_GU:OFF_CZ]), n_ctx_rows)
        cz = _proj(h, bf(w[:, OFF_CZ:OFF_XBC]), n_ctx_rows)
        xbc = _proj(h, bf(w[:, OFF_XBC:OFF_DT]), n_ctx_rows, tn=C_XBC // 2)
        w_dt = jnp.pad(w[:, OFF_DT:OFF_GATES], ((0, 0), (0, LANES - 2 * C_HEADS)))
        dt_raw = _proj(h, bf(w_dt), n_ctx_rows, out_dtype=F32)
        gates = _proj(h, bf(w[:, OFF_GATES:]), n_ctx_rows, kind="sigmoid")

        k_ctx = bf(cache_k[:, l].reshape(n_lat, -1, KV_W))
        v_ctx = bf(cache_v[:, l].reshape(n_lat, -1, KV_W))
        o_a_lat = _lat_attention(qkv, k_ctx, v_ctx, attn_sink[l], n_ctx_rows, n_lat)
        o_a_ctx = _ctx_attention(qkv, attn_sink[l], n_ctx)
        ks.append(kv32[:, :KV_W].reshape(n_ctx, SEQ, A_KV_HEADS, A_HEAD_DIM))
        vs.append(kv32[:, KV_W:].reshape(n_ctx, SEQ, A_KV_HEADS, A_HEAD_DIM))
        o_b = _gmlp(uv, gm_w[l], gm_b[l], n_ctx_rows)
        act, dt, st, dec = _ssd_local(xbc, dt_raw, conv_w[l], conv_b[l], dt_bias[l], a_log[l], n_ctx_rows)
        ncc = n_ctx_rows // C_CHUNK
        hin_l, _ = _ssd_scan(_state_to_rows(state_ssm[:, l]), st, dec, ncc, n_lat, DEC_SEQ // C_CHUNK)
        hin_c, hfin_c = _ssd_scan(jnp.zeros((n_ctx, 2, C_STATE, C_INNER), F32), st, dec, 0, n_ctx, SEQ // C_CHUNK)
        o_c = _ssd_out(act, dt, hin_c, hin_l, cz, a_log[l], d_skip[l], ssm_norm_g[l], n_ctx_rows)
        hs.append(_rows_to_state(hfin_c))
        merged = _merge(o_a_ctx, o_a_lat, o_b, o_c, bf(w_br[l]), gates, n_ctx_rows)
        x = _out_proj(merged, bf(w_o[l]), x_parts if x is None else x, gate1, n_ctx_rows)
        w_r = bf(jnp.pad(jnp.concatenate([w_rg[l], w_re[l]], axis=1),
                         ((0, 0), (0, LANES - MOE_GROUPS - MOE_EXPERTS))))
        b_r = jnp.pad(jnp.concatenate([b_rg[l], b_re[l]]), (0, LANES - MOE_GROUPS - MOE_EXPERTS)).reshape(1, LANES)
        h2, meta = _norm_stage(x, norm2_g[l], n_ctx_rows, mod=(1.0 + scale2, shift2), router=(w_r, b_r))
        y_moe = _moe(h2, meta, w_gate_bf, w_up_bf, w_down_bf, l)
        pending = (y_moe, gate2)

    (y_prompt,) = _norm_stage(x, final_g, n_ctx_rows, n_rows=n_ctx_rows, res=pending, out_dtype=F32)
    (y_sample,) = _norm_stage(x, final_g, n_ctx_rows, first=n_ctx_rows, n_rows=n_lat_rows, res=pending,
                              out_dtype=F32)
    return (y_prompt.reshape(n_ctx, SEQ, D_MODEL), y_sample.reshape(n_lat, DEC_SEQ, D_MODEL),
            jnp.stack(ks, axis=1), jnp.stack(vs, axis=1), jnp.stack(hs, axis=1))
```

```python
import functools

import numpy as np
import jax
import jax.numpy as jnp
from jax import lax
from jax.experimental import pallas as pl
from jax.experimental.pallas import tpu as pltpu

F32 = jnp.float32
BF16 = jnp.bfloat16

D_MODEL = 2048
SEQ = 256
DEC_SEQ = 4096
GRID_W = 64
EPS = 1e-6
A_HEADS = 8
A_KV_HEADS = 2
A_HEAD_DIM = 128
A_GROUP = A_HEADS // A_KV_HEADS
A_BLOCK = 128
ROPE_BASE = 10000.0
ROPE_FREQS = A_HEAD_DIM // 4
B_WIDTH = 1024
B_GROUPS = 4
B_CHUNK = 128
B_GDIM = B_WIDTH // B_GROUPS
C_INNER = 1024
C_HEAD_DIM = 64
C_HEADS = C_INNER // C_HEAD_DIM
C_GROUPS = 2
C_REP = C_HEADS // C_GROUPS
C_STATE = 64
C_CHUNK = 128
C_XBC = C_INNER + 2 * C_GROUPS * C_STATE
N_BRANCH = 3
MOE_GROUPS = 4
MOE_PER_GROUP = 4
MOE_EXPERTS = 16
MOE_HIDDEN = 512
MOE_PAIRS = 6
MOE_BUCKETS = MOE_GROUPS * MOE_PAIRS
Q_W = A_HEADS * A_HEAD_DIM
KV_W = A_KV_HEADS * A_HEAD_DIM
OFF_Q = 0
OFF_K = OFF_Q + Q_W
OFF_V = OFF_K + KV_W
OFF_GU = OFF_V + KV_W
OFF_GV = OFF_GU + B_WIDTH
OFF_CZ = OFF_GV + B_WIDTH
OFF_XBC = OFF_CZ + C_INNER
OFF_DT = OFF_XBC + C_XBC
OFF_GATES = OFF_DT + 2 * C_HEADS
LANES = 128
SUBLANES = 8
LOG2_E = 1.4426950408889634
NEG_BIG = -1e30
V7X_VMEM_LIMIT_MB = 56


def _cparams(dims, vmem_mb=48):
    assert vmem_mb <= V7X_VMEM_LIMIT_MB
    return pltpu.CompilerParams(dimension_semantics=dims, vmem_limit_bytes=vmem_mb * 1024 * 1024)


def _silu(x):
    return x * jax.nn.sigmoid(x)


def _row_tile(n_ctx_rows, cap):
    for t in (1024, 512, 256, 128):
        if t <= cap and n_ctx_rows % t == 0 and DEC_SEQ % t == 0:
            return t
    raise ValueError("context rows must be a multiple of 128")


def _group_of_tile(i, tm, n_ctx_rows):
    nct = n_ctx_rows // tm
    per = DEC_SEQ // tm
    return jnp.where(i < nct, 0, 1 + (i - nct) // per)


def _adaln_kernel(c_ref, w_ref, b_ref, o_ref):
    s = _silu(c_ref[...])
    o_ref[0] = jnp.dot(s.astype(BF16), w_ref[0].astype(BF16), preferred_element_type=F32) + b_ref[0]


def _adaln(cond, w_ada, b_ada):
    depth = w_ada.shape[0]
    r = cond.shape[0]
    tn = 1024
    return pl.pallas_call(
        _adaln_kernel,
        grid=(depth, 6 * D_MODEL // tn),
        in_specs=[pl.BlockSpec((r, D_MODEL), lambda l, j: (0, 0)),
                  pl.BlockSpec((1, D_MODEL, tn), lambda l, j: (l, 0, j)),
                  pl.BlockSpec((1, 1, tn), lambda l, j: (l, 0, j))],
        out_specs=pl.BlockSpec((1, r, tn), lambda l, j: (l, 0, j)),
        out_shape=jax.ShapeDtypeStruct((depth, r, 6 * D_MODEL), F32),
        compiler_params=_cparams(("parallel", "parallel")),
        name="adaln",
    )(cond, w_ada, b_ada.reshape(depth, 1, 6 * D_MODEL))


def _route(logits):
    lane = lax.broadcasted_iota(jnp.int32, logits.shape, 1)
    is_g = lane < MOE_GROUPS
    lg = jnp.where(is_g, logits, NEG_BIG)
    gmax = jnp.max(lg, axis=1, keepdims=True)
    gidx = jnp.min(jnp.where(lg == gmax, lane, LANES), axis=1, keepdims=True)
    p_grp = 1.0 / jnp.sum(jnp.where(is_g, jnp.exp(lg - gmax), 0.0), axis=1, keepdims=True)
    e_id = lane - MOE_GROUPS
    in_grp = (e_id >= 0) & (e_id < MOE_EXPERTS) & ((e_id >> 2) == gidx)
    le = jnp.where(in_grp, logits, NEG_BIG)
    m1 = jnp.max(le, axis=1, keepdims=True)
    i1 = jnp.min(jnp.where(le == m1, lane, LANES), axis=1, keepdims=True)
    le2 = jnp.where(lane == i1, NEG_BIG, le)
    m2 = jnp.max(le2, axis=1, keepdims=True)
    i2 = jnp.min(jnp.where(le2 == m2, lane, LANES), axis=1, keepdims=True)
    e = jnp.exp(m2 - m1)
    w1 = p_grp / (1.0 + e)
    w2 = p_grp * e / (1.0 + e)
    first_lo = i1 < i2
    lo = jnp.where(first_lo, i1, i2) - MOE_GROUPS
    hi = jnp.where(first_lo, i2, i1) - MOE_GROUPS
    w_lo = jnp.where(first_lo, w1, w2)
    w_hi = jnp.where(first_lo, w2, w1)
    lo_l = lo & 3
    hi_l = hi & 3
    base = jnp.where(lo_l == 0, 0, jnp.where(lo_l == 1, 3, 5))
    bucket = gidx * MOE_PAIRS + base + hi_l - lo_l - 1
    return jnp.where(lane == 0, bucket.astype(F32),
                     jnp.where(lane == 1, w_lo, jnp.where(lane == 2, w_hi, 0.0)))


def _pack_bf16_pairs(x):
    w = x.shape[1] // 2
    lo = lax.bitcast_convert_type(x[:, :w].astype(F32), jnp.uint32)
    hi = lax.bitcast_convert_type(x[:, w:].astype(F32), jnp.uint32)
    return (lo >> 16) | hi


def _unpack_bf16_pairs(u):
    lo = lax.bitcast_convert_type(u << 16, F32)
    hi = lax.bitcast_convert_type(u & jnp.uint32(0xFFFF0000), F32)
    return jnp.concatenate([lo.astype(BF16), hi.astype(BF16)], axis=1)


def _pick_rows(i, n_ctx_tiles, ctx_ref, lat_ref):
    return jnp.where(i < n_ctx_tiles, ctx_ref[...], lat_ref[...])


def _pair_specs(block, n_ctx_tiles, tile=lambda *g: g[0], col=lambda *g: 0):
    return [pl.BlockSpec(block, lambda *g: (jnp.minimum(tile(*g), n_ctx_tiles - 1), col(*g))),
            pl.BlockSpec(block, lambda *g: (jnp.maximum(tile(*g) - n_ctx_tiles, 0), col(*g)))]


def _norm_kernel(*refs, has_res, has_mod, write_x, split_x, n_ctx_tiles):
    refs = list(refs)
    if split_x:
        x = _pick_rows(pl.program_id(0), n_ctx_tiles, refs.pop(0), refs.pop(0))
    else:
        x = refs.pop(0)[...]
    if has_res:
        y_ref = refs.pop(0)
        gate_ref = refs.pop(0)
        x = x + gate_ref[0] * y_ref[...].astype(F32)
    g_ref = refs.pop(0)
    if has_mod:
        sc_ref = refs.pop(0)
        sh_ref = refs.pop(0)
    if write_x:
        xo_ref = refs.pop(0)
        xo_ref[...] = x
    h_ref = refs.pop(0)
    h = x * lax.rsqrt(jnp.mean(x * x, axis=-1, keepdims=True) + EPS) * g_ref[...]
    if has_mod:
        h = h * sc_ref[0] + sh_ref[0]
    h_ref[...] = h.astype(h_ref.dtype)


def _route_and_pack(h, wr_ref, br_ref, h_ref, meta_ref):
    h_bf = h.astype(BF16)
    packed = _pack_bf16_pairs(h_bf)
    n_tok = packed.shape[0]
    for k in range(SUBLANES):
        h_ref[pl.ds(k, n_tok, stride=SUBLANES), :] = packed[:, k * LANES:(k + 1) * LANES]
    logits = jnp.dot(h_bf, wr_ref[...], preferred_element_type=F32) + br_ref[...]
    meta_ref[...] = _route(logits)


def _norm_stage(x, norm_g, n_ctx_rows, *, first=0, n_rows=None, res=None, mod=None, write_x=False,
                out_dtype=BF16):
    tm = _row_tile(n_ctx_rows, 512)
    split_x = isinstance(x, tuple)
    if n_rows is None:
        n_rows = sum(a.shape[0] for a in x) if split_x else x.shape[0]
    off = first // tm
    grp = lambda i: _group_of_tile(i + off, tm, n_ctx_rows)
    vec_spec = pl.BlockSpec((1, 1, D_MODEL), lambda i: (grp(i), 0, 0))
    if split_x:
        assert first == 0
        args, specs = list(x), _pair_specs((tm, D_MODEL), n_ctx_rows // tm)
    else:
        args, specs = [x], [pl.BlockSpec((tm, D_MODEL), lambda i: (i + off, 0))]
    if res is not None:
        args += [res[0], res[1]]
        specs += [pl.BlockSpec((tm, D_MODEL), lambda i: (i + off, 0)), vec_spec]
    args.append(norm_g.reshape(1, D_MODEL))
    specs.append(pl.BlockSpec((1, D_MODEL), lambda i: (0, 0)))
    if mod is not None:
        args += [mod[0], mod[1]]
        specs += [vec_spec, vec_spec]
    out_row = pl.BlockSpec((tm, D_MODEL), lambda i: (i, 0))
    outs, out_specs = [], []
    if write_x:
        outs.append(jax.ShapeDtypeStruct((n_rows, D_MODEL), F32))
        out_specs.append(out_row)
    outs.append(jax.ShapeDtypeStruct((n_rows, D_MODEL), out_dtype))
    out_specs.append(out_row)
    kern = functools.partial(_norm_kernel, has_res=res is not None, has_mod=mod is not None,
                             write_x=write_x, split_x=split_x, n_ctx_tiles=n_ctx_rows // tm)
    return pl.pallas_call(
        kern, grid=(n_rows // tm,), in_specs=specs, out_specs=out_specs, out_shape=outs,
        compiler_params=_cparams(("parallel",)), name="norm_stage",
    )(*args)


def _rope_head_perm():
    f = ROPE_FREQS
    return np.concatenate([np.arange(0, f), np.arange(2 * f, 3 * f), np.arange(f, 2 * f), np.arange(3 * f, 4 * f)])


def _rope_chunk(x, cos, sin):
    return x * cos + pltpu.roll(x, A_HEAD_DIM // 2, 1) * sin


def _proj_kernel(a_ref, w_ref, *refs, kind, n_ctx_tiles):
    acc = jnp.dot(a_ref[...], w_ref[...], preferred_element_type=F32)
    if kind == "plain":
        (o_ref,) = refs
        o_ref[...] = acc.astype(o_ref.dtype)
    elif kind == "sigmoid":
        (o_ref,) = refs
        o_ref[...] = jax.nn.sigmoid(acc).astype(o_ref.dtype)
    else:
        cos_ref, sin_ref, o_ref = refs
        i = pl.program_id(0)
        j = pl.program_id(1)
        tn = acc.shape[1]
        n_q_tiles = Q_W // tn
        is_lat = i >= n_ctx_tiles
        cos = jnp.where(is_lat, cos_ref[...], 1.0)
        sin = jnp.where(is_lat, sin_ref[...], 0.0)

        def emit(n_rope):
            for c in range(tn // LANES):
                chunk = acc[:, c * LANES:(c + 1) * LANES]
                if c < n_rope:
                    chunk = _rope_chunk(chunk, cos, sin)
                o_ref[:, c * LANES:(c + 1) * LANES] = chunk.astype(o_ref.dtype)

        @pl.when(j < n_q_tiles)
        def _():
            emit(tn // LANES)

        @pl.when(j >= n_q_tiles)
        def _():
            emit(KV_W // LANES)


def _proj(a, w, n_ctx_rows, *, kind="plain", out_dtype=BF16, rope=None, tn=None, row_range=None):
    t_rows, k = a.shape
    n = w.shape[1]
    tm = _row_tile(n_ctx_rows, 1024)
    first, n_rows = (0, t_rows) if row_range is None else row_range
    off = first // tm
    if tn is None:
        tn = next(c for c in (1024, 768, 640, 512, 256, 128) if n % c == 0)
    args = [a, w]
    specs = [pl.BlockSpec((tm, k), lambda i, j: (i + off, 0)), pl.BlockSpec((k, tn), lambda i, j: (0, j))]
    if kind == "rope":
        assert tn == 2 * KV_W and n == Q_W + 2 * KV_W
        nct = n_ctx_rows // tm
        per = DEC_SEQ // tm
        pos_blk = lambda i, j: (jnp.where(i < nct, 0, (i - nct) % per), 0)
        args += [rope[0], rope[1]]
        specs += [pl.BlockSpec((tm, LANES), pos_blk), pl.BlockSpec((tm, LANES), pos_blk)]
    kern = functools.partial(_proj_kernel, kind=kind, n_ctx_tiles=n_ctx_rows // tm)
    return pl.pallas_call(
        kern, grid=(n_rows // tm, n // tn), in_specs=specs,
        out_specs=pl.BlockSpec((tm, tn), lambda i, j: (i, j)),
        out_shape=jax.ShapeDtypeStruct((n_rows, n), out_dtype),
        compiler_params=_cparams(("parallel", "parallel")), name="proj_" + kind,
    )(*args)


def _proj_split_kernel(a_ref, w_ref, o0_ref, o1_ref):
    acc = jnp.dot(a_ref[...], w_ref[...], preferred_element_type=F32)
    n0 = o0_ref.shape[1]
    o0_ref[...] = acc[:, :n0].astype(o0_ref.dtype)
    o1_ref[...] = acc[:, n0:].astype(o1_ref.dtype)


def _proj_split(a, w, n0, dtypes, n_ctx_rows):
    t_rows, k = a.shape
    n = w.shape[1]
    tm = _row_tile(n_ctx_rows, 1024)
    return pl.pallas_call(
        _proj_split_kernel, grid=(t_rows // tm,),
        in_specs=[pl.BlockSpec((tm, k), lambda i: (i, 0)), pl.BlockSpec((k, n), lambda i: (0, 0))],
        out_specs=[pl.BlockSpec((tm, n0), lambda i: (i, 0)), pl.BlockSpec((tm, n - n0), lambda i: (i, 0))],
        out_shape=[jax.ShapeDtypeStruct((t_rows, n0), dtypes[0]), jax.ShapeDtypeStruct((t_rows, n - n0), dtypes[1])],
        compiler_params=_cparams(("parallel",)), name="proj_split",
    )(a, w)


def _rope_tables():
    pos = np.arange(DEC_SEQ)
    row = (pos // GRID_W).astype(np.float32)
    col = (pos % GRID_W).astype(np.float32)
    inv = jnp.asarray(ROPE_BASE, F32) ** (-jnp.arange(ROPE_FREQS, dtype=F32) / ROPE_FREQS)
    ar = jnp.asarray(row)[:, None] * inv
    ac = jnp.asarray(col)[:, None] * inv
    cos = jnp.concatenate([jnp.cos(ar), jnp.cos(ac), jnp.cos(ar), jnp.cos(ac)], axis=1)
    sin = jnp.concatenate([-jnp.sin(ar), -jnp.sin(ac), jnp.sin(ar), jnp.sin(ac)], axis=1)
    return cos, sin


def _softmax_pv(s, sink_col, v):
    m = jnp.maximum(jnp.max(s, axis=1, keepdims=True), sink_col)
    p = jnp.exp(s - m)
    denom = jnp.sum(p, axis=1, keepdims=True) + jnp.exp(sink_col - m)
    o = jnp.dot(p.astype(BF16), v, preferred_element_type=F32)
    return o / denom


def _stack_heads(q):
    return jnp.concatenate([q[:, g * LANES:(g + 1) * LANES] for g in range(A_GROUP)], axis=0)


def _unstack_heads(o, rows):
    return jnp.concatenate([o[g * rows:(g + 1) * rows, :] for g in range(A_GROUP)], axis=1)


def _sink_col(sink_ref, h, rows):
    r = lax.broadcasted_iota(jnp.int32, (A_GROUP * rows, 1), 0)
    col = jnp.zeros((A_GROUP * rows, 1), F32)
    for g in range(A_GROUP):
        col = jnp.where((r >= g * rows) & (r < (g + 1) * rows), sink_ref[h * A_GROUP + g], col)
    return col


def _ctx_attn_kernel(sink_ref, q_ref, k_ref, v_ref, o_ref):
    h = pl.program_id(1)
    rows = q_ref.shape[0]
    q = _stack_heads(q_ref[...])
    s = lax.dot_general(q, k_ref[...], (((1,), (1,)), ((), ())), preferred_element_type=F32)
    s = s * (A_HEAD_DIM ** -0.5)
    o = _softmax_pv(s, _sink_col(sink_ref, h, rows), v_ref[...])
    o_ref[...] = _unstack_heads(o, rows).astype(o_ref.dtype)


def _ctx_attention(qkv, sink, n_seq):
    gw = A_GROUP * A_HEAD_DIM
    return pl.pallas_call(
        _ctx_attn_kernel,
        grid=(n_seq, A_KV_HEADS),
        in_specs=[pl.BlockSpec(memory_space=pltpu.MemorySpace.SMEM),
                  pl.BlockSpec((SEQ, gw), lambda s, h: (s, h)),
                  pl.BlockSpec((SEQ, A_HEAD_DIM), lambda s, h: (s, OFF_K // A_HEAD_DIM + h)),
                  pl.BlockSpec((SEQ, A_HEAD_DIM), lambda s, h: (s, OFF_V // A_HEAD_DIM + h))],
        out_specs=pl.BlockSpec((SEQ, gw), lambda s, h: (s, h)),
        out_shape=jax.ShapeDtypeStruct((n_seq * SEQ, Q_W), BF16),
        compiler_params=_cparams(("parallel", "parallel")), name="ctx_attn",
    )(sink, qkv, qkv, qkv)


def _lat_attn_kernel(sink_ref, bias_ref, q_ref, kc_ref, vc_ref, kp_ref, k0_ref, kn_ref,
                     vp_ref, v0_ref, vn_ref, o_ref):
    gw = A_GROUP * A_HEAD_DIM
    scale = A_HEAD_DIM ** -0.5
    nt = (((1,), (1,)), ((), ()))
    for h in range(A_KV_HEADS):
        hd = slice(h * A_HEAD_DIM, (h + 1) * A_HEAD_DIM)
        q = _stack_heads(q_ref[:, h * gw:(h + 1) * gw])
        keys = (kc_ref[0, :, hd], kp_ref[:, hd], k0_ref[:, hd], kn_ref[:, hd])
        vals = (vc_ref[0, :, hd], vp_ref[:, hd], v0_ref[:, hd], vn_ref[:, hd])
        s = [lax.dot_general(q, k, nt, preferred_element_type=F32) * scale for k in keys]
        s[1] = s[1] + bias_ref[0, :, :A_BLOCK]
        s[3] = s[3] + bias_ref[0, :, A_BLOCK:]
        sink = _sink_col(sink_ref, h, A_BLOCK)
        m = sink
        for part in s:
            m = jnp.maximum(m, jnp.max(part, axis=1, keepdims=True))
        denom = jnp.exp(sink - m)
        o = None
        for part, v in zip(s, vals):
            p = jnp.exp(part - m)
            denom = denom + jnp.sum(p, axis=1, keepdims=True)
            pv = jnp.dot(p.astype(BF16), v, preferred_element_type=F32)
            o = pv if o is None else o + pv
        o_ref[:, h * gw:(h + 1) * gw] = _unstack_heads(o / denom, A_BLOCK).astype(o_ref.dtype)


def _band_bias():
    ii = np.arange(A_BLOCK)[:, None]
    jj = np.arange(A_BLOCK)[None, :]
    prev = jj >= ii
    nxt = jj <= ii
    none = np.zeros((A_BLOCK, A_BLOCK), bool)
    variants = []
    for has_prev, has_next in ((False, True), (True, True), (True, False)):
        valid = np.concatenate([prev if has_prev else none, nxt if has_next else none], axis=1)
        variants.append(np.tile(np.where(valid, 0.0, NEG_BIG).astype(np.float32), (A_GROUP, 1)))
    return jnp.asarray(np.stack(variants))


def _lat_attention(qkv, k_ctx, v_ctx, sink, n_ctx_rows, n_lat):
    nb = DEC_SEQ // A_BLOCK
    base = n_ctx_rows // A_BLOCK
    n_ctx_keys = k_ctx.shape[1]
    kcol = OFF_K // KV_W
    vcol = OFF_V // KV_W
    row = lambda b, i: base + b * nb + i
    prev = lambda i: jnp.maximum(i - 1, 0)
    nxt = lambda i: jnp.minimum(i + 1, nb - 1)
    blk = (A_BLOCK, KV_W)
    ctx_blk = (1, n_ctx_keys, KV_W)
    bias_sel = lambda b, i: (jnp.where(i == 0, 0, jnp.where(i == nb - 1, 2, 1)), 0, 0)
    return pl.pallas_call(
        _lat_attn_kernel,
        grid=(n_lat, nb),
        in_specs=[pl.BlockSpec(memory_space=pltpu.MemorySpace.SMEM),
                  pl.BlockSpec((1, A_GROUP * A_BLOCK, 2 * A_BLOCK), bias_sel),
                  pl.BlockSpec((A_BLOCK, Q_W), lambda b, i: (row(b, i), 0)),
                  pl.BlockSpec(ctx_blk, lambda b, i: (b, 0, 0)),
                  pl.BlockSpec(ctx_blk, lambda b, i: (b, 0, 0)),
                  pl.BlockSpec(blk, lambda b, i: (row(b, prev(i)), kcol)),
                  pl.BlockSpec(blk, lambda b, i: (row(b, i), kcol)),
                  pl.BlockSpec(blk, lambda b, i: (row(b, nxt(i)), kcol)),
                  pl.BlockSpec(blk, lambda b, i: (row(b, prev(i)), vcol)),
                  pl.BlockSpec(blk, lambda b, i: (row(b, i), vcol)),
                  pl.BlockSpec(blk, lambda b, i: (row(b, nxt(i)), vcol))],
        out_specs=pl.BlockSpec((A_BLOCK, Q_W), lambda b, i: (b * nb + i, 0)),
        out_shape=jax.ShapeDtypeStruct((n_lat * DEC_SEQ, Q_W), BF16),
        compiler_params=_cparams(("parallel", "parallel")), name="lat_attn",
    )(sink, _band_bias(), qkv, k_ctx, v_ctx, qkv, qkv, qkv, qkv, qkv, qkv)


def _gmlp_kernel(uv_u_ref, uv_v_ref, w_ref, b_ref, o_ref):
    v = uv_v_ref[...].astype(F32)
    mu = jnp.mean(v, axis=-1, keepdims=True)
    vc = v - mu
    var = jnp.mean(vc * vc, axis=-1, keepdims=True)
    vn = (vc * lax.rsqrt(var + EPS)).astype(BF16)
    tm = v.shape[0]
    for c in range(tm // B_CHUNK):
        rows = slice(c * B_CHUNK, (c + 1) * B_CHUNK)
        for g in range(B_GROUPS):
            cols = slice(g * B_GDIM, (g + 1) * B_GDIM)
            mixed = jnp.dot(w_ref[g], vn[rows, cols], preferred_element_type=F32) + b_ref[g]
            o_ref[rows, cols] = (uv_u_ref[rows, cols].astype(F32) * mixed).astype(o_ref.dtype)


def _gmlp(uv, gm_w, gm_b, n_ctx_rows):
    t_rows = uv.shape[0]
    tm = _row_tile(n_ctx_rows, 512)
    return pl.pallas_call(
        _gmlp_kernel, grid=(t_rows // tm,),
        in_specs=[pl.BlockSpec((tm, B_WIDTH), lambda i: (i, 0)),
                  pl.BlockSpec((tm, B_WIDTH), lambda i: (i, 1)),
                  pl.BlockSpec((B_GROUPS, B_CHUNK, B_CHUNK), lambda i: (0, 0, 0)),
                  pl.BlockSpec((B_GROUPS, B_CHUNK, 1), lambda i: (0, 0, 0))],
        out_specs=pl.BlockSpec((tm, B_WIDTH), lambda i: (i, 0)),
        out_shape=jax.ShapeDtypeStruct((t_rows, B_WIDTH), BF16),
        compiler_params=_cparams(("parallel",)), name="gmlp",
    )(uv, uv, gm_w.astype(BF16), gm_b.reshape(B_GROUPS, B_CHUNK, 1))


def _split3(x):
    hi = x.astype(BF16)
    r1 = x - hi.astype(F32)
    mid = r1.astype(BF16)
    lo = (r1 - mid.astype(F32)).astype(BF16)
    return hi, mid, lo


def _dot3_rhs(sel, x):
    return sum(jnp.dot(sel, p, preferred_element_type=F32) for p in _split3(x))


def _dot3_lhs(x, sel):
    return sum(jnp.dot(p, sel, preferred_element_type=F32) for p in _split3(x))


def _dot2_lhs(x, sel):
    return sum(jnp.dot(p, sel, preferred_element_type=F32) for p in _split3(x)[:2])


def _tri(lower):
    i = lax.broadcasted_iota(jnp.int32, (C_CHUNK, C_CHUNK), 0)
    j = lax.broadcasted_iota(jnp.int32, (C_CHUNK, C_CHUNK), 1)
    return (j <= i) if lower else (j >= i)


def _head_expand(d):
    r = lax.broadcasted_iota(jnp.int32, (LANES, C_INNER), 0)
    c = lax.broadcasted_iota(jnp.int32, (LANES, C_INNER), 1)
    return (r == (c >> 6) + C_HEADS * d).astype(BF16)


def _chunk_cumsums(dt, a_row):
    a = dt * a_row
    lane = lax.broadcasted_iota(jnp.int32, a.shape, 1)
    pre = _dot3_rhs(_tri(True).astype(BF16), a)
    suf = _dot3_rhs(_tri(False).astype(BF16), a)
    cs = jnp.where(lane < C_HEADS, pre, suf)
    tot = jnp.where(lane[0:1, :] < C_HEADS, pre[C_CHUNK - 1:C_CHUNK, :], suf[0:1, :])
    return cs, tot


SSD_STEP_ROWS = 2 * C_CHUNK
assert SEQ % SSD_STEP_ROWS == 0 and DEC_SEQ % SSD_STEP_ROWS == 0


def _ssd_local_kernel(x_ref, hp_ref, hn_ref, dtr_ref, cw_ref, cb_ref, dtb_ref, alog_ref,
                      act_ref, dt_ref, st_ref, dec_ref, *, n_ctx_blocks):
    b = pl.program_id(0)
    rows = x_ref.shape[0]
    ctx_per = SEQ // rows
    lat_per = DEC_SEQ // rows
    in_seq = jnp.where(b < n_ctx_blocks, b % ctx_per, (b - n_ctx_blocks) % lat_per)
    per = jnp.where(b < n_ctx_blocks, ctx_per, lat_per)
    first = in_seq == 0
    last = in_seq == per - 1
    x = x_ref[...].astype(F32)
    halo = hp_ref.shape[0]
    prev_row = jnp.where(first, 0.0, hp_ref[...].astype(F32)[halo - 1:halo, :])
    next_row = jnp.where(last, 0.0, hn_ref[...].astype(F32)[0:1, :])
    r = lax.broadcasted_iota(jnp.int32, (SUBLANES, x.shape[1]), 0)
    down = pltpu.roll(x, 1, 0)
    up = pltpu.roll(x, rows - 1, 0)
    x_m1 = jnp.concatenate([jnp.where(r == 0, prev_row, down[:SUBLANES]), down[SUBLANES:]], axis=0)
    x_p1 = jnp.concatenate([up[:rows - SUBLANES], jnp.where(r == SUBLANES - 1, next_row, up[rows - SUBLANES:])],
                           axis=0)
    act = _silu(cw_ref[0:1, :] * x_m1 + cw_ref[1:2, :] * x + cw_ref[2:3, :] * x_p1 + cb_ref[...])
    act_bf = act.astype(BF16)
    act_ref[...] = act_bf
    z = dtr_ref[...] + dtb_ref[...]
    dt_all = jnp.maximum(z, 0.0) + jnp.log1p(jnp.exp(-jnp.abs(z)))
    dt_ref[...] = dt_all
    a_row = -jnp.exp(alog_ref[...])
    for s in range(rows // C_CHUNK):
        tok = slice(s * C_CHUNK, (s + 1) * C_CHUNK)
        dt = dt_all[tok]
        cs, tot = _chunk_cumsums(dt, a_row)
        w = jnp.exp(tot - cs) * dt
        xs = act_bf[tok, :C_INNER].astype(F32)
        bmat_t = jnp.transpose(act_bf[tok, C_INNER:C_INNER + LANES].astype(F32))
        for d in range(2):
            e = _head_expand(d)
            xw = (xs * _dot2_lhs(w, e)).astype(BF16)
            dec_ref[s, d] = jnp.exp(_dot3_lhs(tot, e))
            for g in range(C_GROUPS):
                bt = bmat_t[g * C_STATE:(g + 1) * C_STATE, :].astype(BF16)
                cols = slice(g * C_REP * C_HEAD_DIM, (g + 1) * C_REP * C_HEAD_DIM)
                st_ref[s, d, :, cols] = jnp.dot(bt, xw[:, cols], preferred_element_type=F32)


def _ssd_local(xbc, dt_raw, conv_w, conv_b, dt_bias, a_log, n_ctx_rows):
    t_rows = xbc.shape[0]
    rows = SSD_STEP_ROWS
    n_sub = rows // C_CHUNK
    nb = t_rows // rows
    nc = t_rows // C_CHUNK
    halo = 16
    per = rows // halo
    pad = lambda v: jnp.pad(v.reshape(1, -1), ((0, 0), (0, LANES - v.size)))
    kern = functools.partial(_ssd_local_kernel, n_ctx_blocks=n_ctx_rows // rows)
    row = lambda w: pl.BlockSpec((rows, w), lambda c: (c, 0))
    const = lambda shape: pl.BlockSpec(shape, lambda c: (0,) * len(shape))
    return pl.pallas_call(
        kern, grid=(nb,),
        in_specs=[row(C_XBC),
                  pl.BlockSpec((halo, C_XBC), lambda c: (jnp.maximum(c * per - 1, 0), 0)),
                  pl.BlockSpec((halo, C_XBC), lambda c: (jnp.minimum((c + 1) * per, nb * per - 1), 0)),
                  row(LANES), const((3, C_XBC)), const((1, C_XBC)), const((1, LANES)), const((1, LANES))],
        out_specs=[row(C_XBC), row(LANES),
                   pl.BlockSpec((n_sub, 2, C_STATE, C_INNER), lambda c: (c, 0, 0, 0)),
                   pl.BlockSpec((n_sub, 2, 1, C_INNER), lambda c: (c, 0, 0, 0))],
        out_shape=[jax.ShapeDtypeStruct((t_rows, C_XBC), BF16),
                   jax.ShapeDtypeStruct((t_rows, LANES), F32),
                   jax.ShapeDtypeStruct((nc, 2, C_STATE, C_INNER), F32),
                   jax.ShapeDtypeStruct((nc, 2, 1, C_INNER), F32)],
        compiler_params=_cparams(("parallel",)), name="ssd_local",
    )(xbc, xbc, xbc, dt_raw, conv_w, conv_b.reshape(1, C_XBC), pad(dt_bias), pad(a_log))


def _ssd_scan_kernel(h0_ref, st_ref, dec_ref, *refs, n_chunks):
    hin_ref, hfin_ref = refs[-2:]
    d = pl.program_id(1)

    def run(order):
        for j in range(C_INNER // LANES):
            cols = slice(j * LANES, (j + 1) * LANES)
            h = h0_ref[0, 0, :, cols]
            for c in order:
                hin_ref[c, 0, :, cols] = h
                h = h * dec_ref[c, 0, :, cols] + st_ref[c, 0, :, cols]
            hfin_ref[0, 0, :, cols] = h

    @pl.when(d == 0)
    def _():
        run(range(n_chunks))

    @pl.when(d == 1)
    def _():
        run(range(n_chunks - 1, -1, -1))


def _ssd_scan(h0, st, dec, first_chunk, n_seq, n_chunks):
    assert first_chunk % n_chunks == 0
    base = first_chunk // n_chunks
    blk = (n_chunks, 1, C_STATE, C_INNER)
    one = (1, 1, C_STATE, C_INNER)
    return pl.pallas_call(
        functools.partial(_ssd_scan_kernel, n_chunks=n_chunks),
        grid=(n_seq, 2),
        in_specs=[pl.BlockSpec(one, lambda s, d: (s, d, 0, 0)),
                  pl.BlockSpec(blk, lambda s, d: (base + s, d, 0, 0)),
                  pl.BlockSpec((n_chunks, 1, 1, C_INNER), lambda s, d: (base + s, d, 0, 0))],
        out_specs=[pl.BlockSpec(blk, lambda s, d: (s, d, 0, 0)),
                   pl.BlockSpec(one, lambda s, d: (s, d, 0, 0))],
        out_shape=[jax.ShapeDtypeStruct((n_seq * n_chunks, 2, C_STATE, C_INNER), F32),
                   jax.ShapeDtypeStruct((n_seq, 2, C_STATE, C_INNER), F32)],
        compiler_params=_cparams(("parallel", "parallel")), name="ssd_scan",
    )(h0, st, dec)


def _ssd_out_chunk(act, dt, hin, z, a_row, dskip, norm_g):
    cs, _ = _chunk_cumsums(dt, a_row)
    cs2 = cs * LOG2_E
    cl_t = jnp.transpose(cs2 - jnp.log2(dt))
    x_bf = act[:, :C_INNER]
    bm = act[:, C_INNER:C_INNER + C_GROUPS * C_STATE]
    cm = act[:, C_INNER + C_GROUPS * C_STATE:]
    tril = _tri(True)
    triu = _tri(False)
    lane = lax.broadcasted_iota(jnp.int32, (C_CHUNK, LANES), 1)
    y_parts = []
    for g in range(C_GROUPS):
        ns = slice(g * C_STATE, (g + 1) * C_STATE)
        cb = lax.dot_general(cm[:, ns], bm[:, ns], (((1,), (1,)), ((), ())), preferred_element_type=F32)
        for hp in range(C_REP // 2):
            pair = []
            for h in (g * C_REP + 2 * hp, g * C_REP + 2 * hp + 1):
                hb = h + C_HEADS
                lf = jnp.exp2(jnp.where(tril, cs2[:, h:h + 1] - cl_t[h:h + 1, :], NEG_BIG))
                lb = jnp.exp2(jnp.where(triu, cs2[:, hb:hb + 1] - cl_t[hb:hb + 1, :], NEG_BIG))
                pair.append((cb * (lf + lb)).astype(BF16))
            cols = slice((g * C_REP + 2 * hp) * C_HEAD_DIM, (g * C_REP + 2 * hp + 2) * C_HEAD_DIM)
            xp = x_bf[:, cols]
            y0 = jnp.dot(pair[0], xp, preferred_element_type=F32)
            y1 = jnp.dot(pair[1], xp, preferred_element_type=F32)
            y_parts.append(jnp.where(lane < C_HEAD_DIM, y0, y1))
    y = jnp.concatenate(y_parts, axis=1)
    grow_small = jnp.exp2(cs2)
    for d in range(2):
        grow = _dot2_lhs(grow_small, _head_expand(d))
        yo = []
        for g in range(C_GROUPS):
            ns = slice(g * C_STATE, (g + 1) * C_STATE)
            cols = slice(g * C_REP * C_HEAD_DIM, (g + 1) * C_REP * C_HEAD_DIM)
            yo.append(jnp.dot(cm[:, ns], hin(d, cols).astype(BF16), preferred_element_type=F32))
        y = y + jnp.concatenate(yo, axis=1) * grow
    y = y + dskip * x_bf.astype(F32)
    o = y * _silu(z.astype(F32))
    return o * lax.rsqrt(jnp.mean(o * o, axis=-1, keepdims=True) + EPS) * norm_g


def _state_to_rows(h):
    b = h.shape[0]
    return jnp.transpose(h, (0, 1, 4, 2, 3)).reshape(b, 2, C_STATE, C_INNER)


def _rows_to_state(h):
    b = h.shape[0]
    return jnp.transpose(h.reshape(b, 2, C_STATE, C_HEADS, C_HEAD_DIM), (0, 1, 3, 4, 2))


def _ssd_merge_kernel(act_ref, dt_ref, hin_ctx_ref, hin_lat_ref, z_ref, alog_ref, dskip_ref, ng_ref,
                      a0c_ref, a0l_ref, a1_ref, w_ref, g0_ref, g1_ref, g2_ref, o_ref, *, n_ctx_blocks):
    is_ctx = pl.program_id(0) < n_ctx_blocks
    a0 = jnp.where(is_ctx, a0c_ref[...], a0l_ref[...])
    acc = g0_ref[...].astype(F32) * jnp.dot(a0, w_ref[0], preferred_element_type=F32)
    acc = acc + g1_ref[...].astype(F32) * jnp.dot(a1_ref[...], w_ref[1], preferred_element_type=F32)
    a_row = -jnp.exp(alog_ref[...])
    o_c = []
    for s in range(act_ref.shape[0] // C_CHUNK):
        tok = slice(s * C_CHUNK, (s + 1) * C_CHUNK)
        hin = lambda d, cols, s=s: jnp.where(is_ctx, hin_ctx_ref[s, d, :, cols], hin_lat_ref[s, d, :, cols])
        o_c.append(_ssd_out_chunk(act_ref[tok, :], dt_ref[tok, :], hin, z_ref[tok, :], a_row,
                                  dskip_ref[...], ng_ref[...]).astype(BF16))
    acc = acc + g2_ref[...].astype(F32) * jnp.dot(jnp.concatenate(o_c, axis=0), w_ref[2],
                                                  preferred_element_type=F32)
    o_ref[...] = acc.astype(o_ref.dtype)


def _ssd_merge(act, dt, hin_ctx, hin_lat, z, a_log, d_skip, norm_g, o_a_ctx, o_a_lat, o_b, w_br, gates,
               n_ctx_rows):
    t_rows = act.shape[0]
    rows = SSD_STEP_ROWS
    ncb = n_ctx_rows // rows
    pad = lambda v: jnp.pad(v.reshape(1, -1), ((0, 0), (0, LANES - v.size)))
    row = lambda w, col=0: pl.BlockSpec((rows, w), lambda c: (c, col))
    const = lambda shape, **kw: pl.BlockSpec(shape, lambda c: (0,) * len(shape), **kw)
    hin_blk = (rows // C_CHUNK, 2, C_STATE, C_INNER)
    d_exp = jnp.repeat(d_skip, C_HEAD_DIM).reshape(1, C_INNER)
    return pl.pallas_call(
        functools.partial(_ssd_merge_kernel, n_ctx_blocks=ncb), grid=(t_rows // rows,),
        in_specs=[row(C_XBC), row(LANES),
                  pl.BlockSpec(hin_blk, lambda c: (jnp.minimum(c, ncb - 1), 0, 0, 0)),
                  pl.BlockSpec(hin_blk, lambda c: (jnp.maximum(c - ncb, 0), 0, 0, 0)),
                  row(C_INNER), const((1, LANES)), const((1, C_INNER)), const((1, C_INNER))]
                 + _pair_specs((rows, Q_W), ncb)
                 + [row(B_WIDTH), const((N_BRANCH, C_INNER, D_MODEL), pipeline_mode=pl.Buffered(1)),
                    row(D_MODEL, 0), row(D_MODEL, 1), row(D_MODEL, 2)],
        out_specs=row(D_MODEL),
        out_shape=jax.ShapeDtypeStruct((t_rows, D_MODEL), BF16),
        compiler_params=_cparams(("parallel",)), name="ssd_merge",
    )(act, dt, hin_ctx, hin_lat, z, pad(a_log), d_exp, norm_g.reshape(1, C_INNER),
      o_a_ctx, o_a_lat, o_b, w_br, gates, gates, gates)


def _out_proj_kernel(a_ref, w_ref, *refs, split_x, n_ctx_tiles):
    refs = list(refs)
    if split_x:
        x = _pick_rows(pl.program_id(0), n_ctx_tiles, refs.pop(0), refs.pop(0))
    else:
        x = refs.pop(0)[...]
    gate_ref, g_ref, sc_ref, sh_ref, wr_ref, br_ref, xo_ref, h_ref, meta_ref = refs
    x = x + gate_ref[0] * jnp.dot(a_ref[...], w_ref[...], preferred_element_type=F32)
    xo_ref[...] = x
    h = x * lax.rsqrt(jnp.mean(x * x, axis=-1, keepdims=True) + EPS) * g_ref[...]
    _route_and_pack(h * sc_ref[0] + sh_ref[0], wr_ref, br_ref, h_ref, meta_ref)


def _out_proj(merged, w_o, x, gate, norm_g, mod, router, n_ctx_rows):
    assert D_MODEL // 2 == SUBLANES * LANES
    t_rows = merged.shape[0]
    tm = _row_tile(n_ctx_rows, 512)
    nct = n_ctx_rows // tm
    split_x = isinstance(x, tuple)
    if split_x:
        x_args, x_specs = list(x), _pair_specs((tm, D_MODEL), nct)
    else:
        x_args, x_specs = [x], [pl.BlockSpec((tm, D_MODEL), lambda i: (i, 0))]
    vec_spec = pl.BlockSpec((1, 1, D_MODEL), lambda i: (_group_of_tile(i, tm, n_ctx_rows), 0, 0))
    const = lambda shape, **kw: pl.BlockSpec(shape, lambda i: (0,) * len(shape), **kw)
    return pl.pallas_call(
        functools.partial(_out_proj_kernel, split_x=split_x, n_ctx_tiles=nct),
        grid=(t_rows // tm,),
        in_specs=[pl.BlockSpec((tm, D_MODEL), lambda i: (i, 0)),
                  const((D_MODEL, D_MODEL), pipeline_mode=pl.Buffered(1))] + x_specs + [
                  vec_spec, const((1, D_MODEL)), vec_spec, vec_spec, const((D_MODEL, LANES)), const((1, LANES))],
        out_specs=[pl.BlockSpec((tm, D_MODEL), lambda i: (i, 0)),
                   pl.BlockSpec((tm * SUBLANES, LANES), lambda i: (i, 0)),
                   pl.BlockSpec((tm, LANES), lambda i: (i, 0))],
        out_shape=[jax.ShapeDtypeStruct((t_rows, D_MODEL), F32),
                   jax.ShapeDtypeStruct((t_rows * SUBLANES, LANES), jnp.uint32),
                   jax.ShapeDtypeStruct((t_rows, LANES), F32)],
        compiler_params=_cparams(("parallel",)), name="out_proj",
    )(merged, w_o, *x_args, gate, norm_g.reshape(1, D_MODEL), mod[0], mod[1], router[0], router[1])


def _moe_kernel(src_ref, ea_ref, eb_ref, nv_ref, h_hbm, meta_ref, ga_ref, ua_ref, da_ref, gb_ref, ub_ref, db_ref,
                o_ref, xbuf, sem):
    t = pl.program_id(0)
    tm = xbuf.shape[1] // SUBLANES
    slot = t % 2

    def gather(tile, dst_slot):
        for r in range(tm):
            row = pl.multiple_of(src_ref[tile * tm + r], SUBLANES)
            pltpu.make_async_copy(h_hbm.at[pl.ds(row, SUBLANES), :],
                                  xbuf.at[dst_slot, pl.ds(r * SUBLANES, SUBLANES), :],
                                  sem.at[dst_slot]).start()

    @pl.when(t == 0)
    def _():
        gather(0, 0)

    @pl.when(t + 1 < nv_ref[0])
    def _():
        gather(t + 1, 1 - slot)

    @pl.when(t < nv_ref[0])
    def _():
        pltpu.make_async_copy(h_hbm.at[pl.ds(0, tm * SUBLANES), :], xbuf.at[slot], sem.at[slot]).wait()
        x = _unpack_bf16_pairs(jnp.concatenate(
            [xbuf[slot, pl.ds(k, tm, stride=SUBLANES), :] for k in range(SUBLANES)], axis=1))

        def expert(g_ref, u_ref, d_ref, wcol):
            g = jnp.dot(x, g_ref[0, 0], preferred_element_type=F32)
            u = jnp.dot(x, u_ref[0, 0], preferred_element_type=F32)
            he = _silu(g) * u
            return wcol * jnp.dot(he.astype(BF16), d_ref[0, 0], preferred_element_type=F32)

        o = (expert(ga_ref, ua_ref, da_ref, meta_ref[:, 1:2])
             + expert(gb_ref, ub_ref, db_ref, meta_ref[:, 2:3]))
        o_ref[...] = o.astype(o_ref.dtype)

    @pl.when(t >= nv_ref[0])
    def _():
        o_ref[...] = jnp.zeros(o_ref.shape, o_ref.dtype)


def _moe_tile(n_rows):
    return 256 if n_rows >= 8192 else 128


def _moe_plan(bucket, tm):
    t_rows = bucket.shape[0]
    n_tiles = t_rows // tm + MOE_BUCKETS
    iota = jnp.arange(t_rows, dtype=jnp.int32)
    sorted_bucket, order = lax.sort((bucket, iota), num_keys=1)
    counts = jnp.sum((bucket[:, None] == jnp.arange(MOE_BUCKETS, dtype=jnp.int32)[None, :]).astype(jnp.int32), axis=0)
    padded = ((counts + tm - 1) // tm) * tm
    pad_end = jnp.cumsum(padded)
    pad_off = pad_end - padded
    off = jnp.cumsum(counts) - counts
    dest = pad_off[sorted_bucket] + iota - off[sorted_bucket]
    _, pos = lax.sort((order, dest), num_keys=1)
    tile_start = jnp.arange(n_tiles, dtype=jnp.int32) * tm
    tile_bucket = jnp.sum((tile_start[:, None] >= pad_end[None, :]).astype(jnp.int32), axis=1)
    last_bucket = jnp.max(jnp.where(counts > 0, jnp.arange(MOE_BUCKETS, dtype=jnp.int32), 0))
    tile_bucket = jnp.where(tile_start < pad_end[-1], tile_bucket, last_bucket)
    n_valid = (pad_end[-1] // tm).astype(jnp.int32).reshape(1)
    row_bucket = jnp.repeat(tile_bucket, tm)
    q = jnp.arange(n_tiles * tm, dtype=jnp.int32) - pad_off[row_bucket]
    src = order[jnp.clip(off[row_bucket] + q, 0, t_rows - 1)]
    pair_lo = jnp.asarray(np.array([0, 0, 0, 1, 1, 2], np.int32))
    pair_hi = jnp.asarray(np.array([1, 2, 3, 2, 3, 3], np.int32))
    grp = tile_bucket // MOE_PAIRS
    ea = grp * MOE_PER_GROUP + pair_lo[tile_bucket % MOE_PAIRS]
    eb = grp * MOE_PER_GROUP + pair_hi[tile_bucket % MOE_PAIRS]
    return src, pos, ea, eb, n_valid


def _moe(h2, meta, w_gate, w_up, w_down, layer):
    t_rows = meta.shape[0]
    tm = _moe_tile(t_rows)
    src, pos, ea, eb, n_valid = _moe_plan(meta[:, 0].astype(jnp.int32), tm)
    n_tiles = ea.shape[0]
    meta_sorted = meta.at[src].get(mode="promise_in_bounds")
    up_spec = lambda sel: pl.BlockSpec((1, 1, D_MODEL, MOE_HIDDEN), sel)
    dn_spec = lambda sel: pl.BlockSpec((1, 1, MOE_HIDDEN, D_MODEL), sel)
    sel_a = lambda t, src, ea, eb, nv: (layer, ea[t], 0, 0)
    sel_b = lambda t, src, ea, eb, nv: (layer, eb[t], 0, 0)
    y_sorted = pl.pallas_call(
        _moe_kernel,
        grid_spec=pltpu.PrefetchScalarGridSpec(
            num_scalar_prefetch=4, grid=(n_tiles,),
            in_specs=[pl.BlockSpec(memory_space=pl.ANY),
                      pl.BlockSpec((tm, LANES), lambda t, src, ea, eb, nv: (t, 0)),
                      up_spec(sel_a), up_spec(sel_a), dn_spec(sel_a),
                      up_spec(sel_b), up_spec(sel_b), dn_spec(sel_b)],
            out_specs=pl.BlockSpec((tm, D_MODEL), lambda t, src, ea, eb, nv: (t, 0)),
            scratch_shapes=[pltpu.VMEM((2, tm * SUBLANES, LANES), jnp.uint32), pltpu.SemaphoreType.DMA((2,))]),
        out_shape=jax.ShapeDtypeStruct((n_tiles * tm, D_MODEL), BF16),
        compiler_params=_cparams(("arbitrary",)), name="moe",
    )(src * SUBLANES, ea, eb, n_valid, h2, meta_sorted, w_gate, w_up, w_down, w_gate, w_up, w_down)
    return y_sorted.at[pos].get(mode="promise_in_bounds")


def _vec(mod_l, k):
    return mod_l[:, k * D_MODEL:(k + 1) * D_MODEL].reshape(-1, 1, D_MODEL)


def kernel(x_prompt, x_sample, cache_k, cache_v, state_ssm, c, c_ctx, w_ada, b_ada, norm1_g, w_in, attn_sink,
           gm_w, gm_b, conv_w, conv_b, dt_bias, a_log, d_skip, ssm_norm_g, w_br, w_o, norm2_g, w_rg, b_rg,
           w_re, b_re, w_gate, w_up, w_down, final_g):
    depth = w_in.shape[0]
    n_ctx, n_lat = x_prompt.shape[0], x_sample.shape[0]
    assert x_prompt.shape[1:] == (SEQ, D_MODEL) and x_sample.shape[1:] == (DEC_SEQ, D_MODEL)
    n_ctx_rows = n_ctx * SEQ
    n_lat_rows = n_lat * DEC_SEQ
    x_parts = (x_prompt.reshape(n_ctx_rows, D_MODEL), x_sample.reshape(n_lat_rows, D_MODEL))
    x = None
    bf = lambda a: a.astype(BF16)
    w_gate_bf, w_up_bf, w_down_bf = bf(w_gate), bf(w_up), bf(w_down)

    n_cond = 1 + n_lat
    cond = jnp.zeros((((n_cond + 7) // 8) * 8, D_MODEL), F32).at[0].set(c_ctx).at[1:n_cond].set(c)
    mod = _adaln(cond, w_ada, b_ada)
    rope = _rope_tables()
    head_cols = (np.arange(A_HEADS + A_KV_HEADS)[:, None] * A_HEAD_DIM + _rope_head_perm()[None, :]).reshape(-1)
    qkv_cols = np.concatenate([head_cols, np.arange(Q_W + KV_W, Q_W + 2 * KV_W)])

    ks, vs, hs = [], [], []
    pending = None
    for l in range(depth):
        shift1, scale1, gate1, shift2, scale2, gate2 = (_vec(mod[l], k) for k in range(6))
        w = w_in[l]
        if x is None:
            (h,) = _norm_stage(x_parts, norm1_g[l], n_ctx_rows, mod=(1.0 + scale1, shift1))
        else:
            x, h = _norm_stage(x, norm1_g[l], n_ctx_rows, res=pending, mod=(1.0 + scale1, shift1), write_x=True)
        qkv = _proj(h, bf(w[:, OFF_Q:OFF_GU][:, qkv_cols]), n_ctx_rows, kind="rope", rope=rope, tn=2 * KV_W)
        kv32 = _proj(h, bf(w[:, OFF_K:OFF_GU]), n_ctx_rows, out_dtype=F32, tn=2 * KV_W, row_range=(0, n_ctx_rows))
        uv = _proj(h, bf(w[:, OFF_GU:OFF_CZ]), n_ctx_rows)
        cz = _proj(h, bf(w[:, OFF_CZ:OFF_XBC]), n_ctx_rows)
        w_xbc_dt = jnp.pad(w[:, OFF_XBC:OFF_GATES], ((0, 0), (0, LANES - 2 * C_HEADS)))
        xbc, dt_raw = _proj_split(h, bf(w_xbc_dt), C_XBC, (BF16, F32), n_ctx_rows)
        gates = _proj(h, bf(w[:, OFF_GATES:]), n_ctx_rows, kind="sigmoid")

        k_ctx = bf(cache_k[:, l].reshape(n_lat, -1, KV_W)[:, :, qkv_cols[:KV_W]])
        v_ctx = bf(cache_v[:, l].reshape(n_lat, -1, KV_W))
        o_a_lat = _lat_attention(qkv, k_ctx, v_ctx, attn_sink[l], n_ctx_rows, n_lat)
        o_a_ctx = _ctx_attention(qkv, attn_sink[l], n_ctx)
        ks.append(kv32[:, :KV_W].reshape(n_ctx, SEQ, A_KV_HEADS, A_HEAD_DIM))
        vs.append(kv32[:, KV_W:].reshape(n_ctx, SEQ, A_KV_HEADS, A_HEAD_DIM))
        o_b = _gmlp(uv, gm_w[l], gm_b[l], n_ctx_rows)
        act, dt, st, dec = _ssd_local(xbc, dt_raw, conv_w[l], conv_b[l], dt_bias[l], a_log[l], n_ctx_rows)
        ncc = n_ctx_rows // C_CHUNK
        hin_l, _ = _ssd_scan(_state_to_rows(state_ssm[:, l]), st, dec, ncc, n_lat, DEC_SEQ // C_CHUNK)
        hin_c, hfin_c = _ssd_scan(jnp.zeros((n_ctx, 2, C_STATE, C_INNER), F32), st, dec, 0, n_ctx, SEQ // C_CHUNK)
        hs.append(_rows_to_state(hfin_c))
        merged = _ssd_merge(act, dt, hin_c, hin_l, cz, a_log[l], d_skip[l], ssm_norm_g[l],
                            o_a_ctx, o_a_lat, o_b, bf(w_br[l]), gates, n_ctx_rows)
        w_r = bf(jnp.pad(jnp.concatenate([w_rg[l], w_re[l]], axis=1),
                         ((0, 0), (0, LANES - MOE_GROUPS - MOE_EXPERTS))))
        b_r = jnp.pad(jnp.concatenate([b_rg[l], b_re[l]]), (0, LANES - MOE_GROUPS - MOE_EXPERTS)).reshape(1, LANES)
        x, h2, meta = _out_proj(merged, bf(w_o[l]), x_parts if x is None else x, gate1, norm2_g[l],
                                (1.0 + scale2, shift2), (w_r, b_r), n_ctx_rows)
        y_moe = _moe(h2, meta, w_gate_bf, w_up_bf, w_down_bf, l)
        pending = (y_moe, gate2)

    (y_prompt,) = _norm_stage(x, final_g, n_ctx_rows, n_rows=n_ctx_rows, res=pending, out_dtype=F32)
    (y_sample,) = _norm_stage(x, final_g, n_ctx_rows, first=n_ctx_rows, n_rows=n_lat_rows, res=pending,
                              out_dtype=F32)
    return (y_prompt.reshape(n_ctx, SEQ, D_MODEL), y_sample.reshape(n_lat, DEC_SEQ, D_MODEL),
            jnp.stack(ks, axis=1), jnp.stack(vs, axis=1), jnp.stack(hs, axis=1))
```

```python
import functools
import itertools

import numpy as np
import jax
import jax.numpy as jnp
from jax import lax
from jax.experimental import pallas as pl
from jax.experimental.pallas import tpu as pltpu

F32 = jnp.float32
BF16 = jnp.bfloat16

D_MODEL = 2048
SEQ = 256
DEC_SEQ = 4096
GRID_W = 64
EPS = 1e-6
A_HEADS = 8
A_KV_HEADS = 2
A_HEAD_DIM = 128
A_GROUP = A_HEADS // A_KV_HEADS
A_BLOCK = 128
ROPE_BASE = 10000.0
ROPE_FREQS = A_HEAD_DIM // 4
B_WIDTH = 1024
B_GROUPS = 4
B_CHUNK = 128
B_GDIM = B_WIDTH // B_GROUPS
C_INNER = 1024
C_HEAD_DIM = 64
C_HEADS = C_INNER // C_HEAD_DIM
C_GROUPS = 2
C_REP = C_HEADS // C_GROUPS
C_STATE = 64
C_CHUNK = 128
C_XBC = C_INNER + 2 * C_GROUPS * C_STATE
N_BRANCH = 3
MOE_GROUPS = 4
MOE_PER_GROUP = 4
MOE_EXPERTS = 16
MOE_HIDDEN = 512
MOE_PAIRS = 6
MOE_BUCKETS = MOE_GROUPS * MOE_PAIRS
Q_W = A_HEADS * A_HEAD_DIM
KV_W = A_KV_HEADS * A_HEAD_DIM
OFF_Q = 0
OFF_K = OFF_Q + Q_W
OFF_V = OFF_K + KV_W
OFF_GU = OFF_V + KV_W
OFF_GV = OFF_GU + B_WIDTH
OFF_CZ = OFF_GV + B_WIDTH
OFF_XBC = OFF_CZ + C_INNER
OFF_DT = OFF_XBC + C_XBC
OFF_GATES = OFF_DT + 2 * C_HEADS
LANES = 128
SUBLANES = 8
LOG2_E = 1.4426950408889634
NEG_BIG = -1e30
V7X_VMEM_LIMIT_MB = 56


def _cparams(dims, vmem_mb=48):
    assert vmem_mb <= V7X_VMEM_LIMIT_MB
    return pltpu.CompilerParams(dimension_semantics=dims, vmem_limit_bytes=vmem_mb * 1024 * 1024)


def _silu(x):
    return x * jax.nn.sigmoid(x)


def _row_tile(n_ctx_rows, cap):
    for t in (1024, 512, 256, 128):
        if t <= cap and n_ctx_rows % t == 0 and DEC_SEQ % t == 0:
            return t
    raise ValueError("context rows must be a multiple of 128")


def _group_of_tile(i, tm, n_ctx_rows):
    nct = n_ctx_rows // tm
    per = DEC_SEQ // tm
    return jnp.where(i < nct, 0, 1 + (i - nct) // per)


def _adaln_kernel(c_ref, w_ref, b_ref, o_ref):
    s = _silu(c_ref[...])
    o_ref[0] = jnp.dot(s.astype(BF16), w_ref[0].astype(BF16), preferred_element_type=F32) + b_ref[0]


def _adaln(cond, w_ada, b_ada):
    depth = w_ada.shape[0]
    r = cond.shape[0]
    tn = 1024
    return pl.pallas_call(
        _adaln_kernel,
        grid=(depth, 6 * D_MODEL // tn),
        in_specs=[pl.BlockSpec((r, D_MODEL), lambda l, j: (0, 0)),
                  pl.BlockSpec((1, D_MODEL, tn), lambda l, j: (l, 0, j)),
                  pl.BlockSpec((1, 1, tn), lambda l, j: (l, 0, j))],
        out_specs=pl.BlockSpec((1, r, tn), lambda l, j: (l, 0, j)),
        out_shape=jax.ShapeDtypeStruct((depth, r, 6 * D_MODEL), F32),
        compiler_params=_cparams(("parallel", "parallel")),
        name="adaln",
    )(cond, w_ada, b_ada.reshape(depth, 1, 6 * D_MODEL))


def _route(logits):
    lane = lax.broadcasted_iota(jnp.int32, logits.shape, 1)
    is_g = lane < MOE_GROUPS
    lg = jnp.where(is_g, logits, NEG_BIG)
    gmax = jnp.max(lg, axis=1, keepdims=True)
    gidx = jnp.min(jnp.where(lg == gmax, lane, LANES), axis=1, keepdims=True)
    p_grp = 1.0 / jnp.sum(jnp.where(is_g, jnp.exp(lg - gmax), 0.0), axis=1, keepdims=True)
    e_id = lane - MOE_GROUPS
    in_grp = (e_id >= 0) & (e_id < MOE_EXPERTS) & ((e_id >> 2) == gidx)
    le = jnp.where(in_grp, logits, NEG_BIG)
    m1 = jnp.max(le, axis=1, keepdims=True)
    i1 = jnp.min(jnp.where(le == m1, lane, LANES), axis=1, keepdims=True)
    le2 = jnp.where(lane == i1, NEG_BIG, le)
    m2 = jnp.max(le2, axis=1, keepdims=True)
    i2 = jnp.min(jnp.where(le2 == m2, lane, LANES), axis=1, keepdims=True)
    e = jnp.exp(m2 - m1)
    w1 = p_grp / (1.0 + e)
    w2 = p_grp * e / (1.0 + e)
    first_lo = i1 < i2
    lo = jnp.where(first_lo, i1, i2) - MOE_GROUPS
    hi = jnp.where(first_lo, i2, i1) - MOE_GROUPS
    w_lo = jnp.where(first_lo, w1, w2)
    w_hi = jnp.where(first_lo, w2, w1)
    lo_l = lo & 3
    hi_l = hi & 3
    base = jnp.where(lo_l == 0, 0, jnp.where(lo_l == 1, 3, 5))
    bucket = gidx * MOE_PAIRS + base + hi_l - lo_l - 1
    return jnp.where(lane == 0, bucket.astype(F32),
                     jnp.where(lane == 1, w_lo, jnp.where(lane == 2, w_hi, 0.0)))


def _pack_bf16_pairs(x):
    w = x.shape[1] // 2
    lo = lax.bitcast_convert_type(x[:, :w].astype(F32), jnp.uint32)
    hi = lax.bitcast_convert_type(x[:, w:].astype(F32), jnp.uint32)
    return (lo >> 16) | hi


def _unpack_bf16_pairs(u):
    lo = lax.bitcast_convert_type(u << 16, F32)
    hi = lax.bitcast_convert_type(u & jnp.uint32(0xFFFF0000), F32)
    return jnp.concatenate([lo.astype(BF16), hi.astype(BF16)], axis=1)


def _pick_rows(i, n_ctx_tiles, ctx_ref, lat_ref):
    return jnp.where(i < n_ctx_tiles, ctx_ref[...], lat_ref[...])


def _pair_specs(block, n_ctx_tiles, tile=lambda *g: g[0], col=lambda *g: 0):
    return [pl.BlockSpec(block, lambda *g: (jnp.minimum(tile(*g), n_ctx_tiles - 1), col(*g))),
            pl.BlockSpec(block, lambda *g: (jnp.maximum(tile(*g) - n_ctx_tiles, 0), col(*g)))]


def _norm_kernel(*refs, has_res, has_mod, write_x, split_x, n_ctx_tiles):
    refs = list(refs)
    if split_x:
        x = _pick_rows(pl.program_id(0), n_ctx_tiles, refs.pop(0), refs.pop(0))
    else:
        x = refs.pop(0)[...]
    if has_res:
        y_ref = refs.pop(0)
        gate_ref = refs.pop(0)
        x = x + gate_ref[0] * y_ref[...].astype(F32)
    g_ref = refs.pop(0)
    if has_mod:
        sc_ref = refs.pop(0)
        sh_ref = refs.pop(0)
    if write_x:
        xo_ref = refs.pop(0)
        xo_ref[...] = x
    h_ref = refs.pop(0)
    h = x * lax.rsqrt(jnp.mean(x * x, axis=-1, keepdims=True) + EPS) * g_ref[...]
    if has_mod:
        h = h * sc_ref[0] + sh_ref[0]
    h_ref[...] = h.astype(h_ref.dtype)


def _route_and_pack(h, wr_ref, br_ref, h_ref, meta_ref):
    h_bf = h.astype(BF16)
    packed = _pack_bf16_pairs(h_bf)
    n_tok = packed.shape[0]
    for k in range(SUBLANES):
        h_ref[pl.ds(k, n_tok, stride=SUBLANES), :] = packed[:, k * LANES:(k + 1) * LANES]
    logits = jnp.dot(h_bf, wr_ref[...], preferred_element_type=F32) + br_ref[...]
    meta_ref[...] = _route(logits)


def _norm_stage(x, norm_g, n_ctx_rows, *, first=0, n_rows=None, res=None, mod=None, write_x=False,
                out_dtype=BF16):
    tm = _row_tile(n_ctx_rows, 512)
    split_x = isinstance(x, tuple)
    if n_rows is None:
        n_rows = sum(a.shape[0] for a in x) if split_x else x.shape[0]
    off = first // tm
    grp = lambda i: _group_of_tile(i + off, tm, n_ctx_rows)
    vec_spec = pl.BlockSpec((1, 1, D_MODEL), lambda i: (grp(i), 0, 0))
    if split_x:
        assert first == 0
        args, specs = list(x), _pair_specs((tm, D_MODEL), n_ctx_rows // tm)
    else:
        args, specs = [x], [pl.BlockSpec((tm, D_MODEL), lambda i: (i + off, 0))]
    if res is not None:
        args += [res[0], res[1]]
        specs += [pl.BlockSpec((tm, D_MODEL), lambda i: (i + off, 0)), vec_spec]
    args.append(norm_g.reshape(1, D_MODEL))
    specs.append(pl.BlockSpec((1, D_MODEL), lambda i: (0, 0)))
    if mod is not None:
        args += [mod[0], mod[1]]
        specs += [vec_spec, vec_spec]
    out_row = pl.BlockSpec((tm, D_MODEL), lambda i: (i, 0))
    outs, out_specs = [], []
    if write_x:
        outs.append(jax.ShapeDtypeStruct((n_rows, D_MODEL), F32))
        out_specs.append(out_row)
    outs.append(jax.ShapeDtypeStruct((n_rows, D_MODEL), out_dtype))
    out_specs.append(out_row)
    kern = functools.partial(_norm_kernel, has_res=res is not None, has_mod=mod is not None,
                             write_x=write_x, split_x=split_x, n_ctx_tiles=n_ctx_rows // tm)
    return pl.pallas_call(
        kern, grid=(n_rows // tm,), in_specs=specs, out_specs=out_specs, out_shape=outs,
        compiler_params=_cparams(("parallel",)), name="norm_stage",
    )(*args)


def _rope_head_perm():
    f = ROPE_FREQS
    return np.concatenate([np.arange(0, f), np.arange(2 * f, 3 * f), np.arange(f, 2 * f), np.arange(3 * f, 4 * f)])


def _rope_chunk(x, cos, sin):
    return x * cos + pltpu.roll(x, A_HEAD_DIM // 2, 1) * sin


PROJ_CHUNK = 256


def _dot_then_epilogue(a, w_ref, n_cols, epilogue):
    prev = None
    for c in range(n_cols // PROJ_CHUNK):
        acc = jnp.dot(a, w_ref[:, c * PROJ_CHUNK:(c + 1) * PROJ_CHUNK], preferred_element_type=F32)
        if prev is not None:
            epilogue(*prev)
        prev = (c, acc)
    epilogue(*prev)


def _proj_kernel(a_ref, w_ref, o_ref, *, kind):
    def epilogue(c, acc):
        if kind == "sigmoid":
            acc = jax.nn.sigmoid(acc)
        o_ref[:, c * PROJ_CHUNK:(c + 1) * PROJ_CHUNK] = acc.astype(o_ref.dtype)

    _dot_then_epilogue(a_ref[...], w_ref, o_ref.shape[1], epilogue)


def _proj(a, w, n_ctx_rows, *, kind="plain", out_dtype=BF16, tn=None, row_range=None):
    t_rows, k = a.shape
    n = w.shape[1]
    tm = _row_tile(n_ctx_rows, 1024)
    first, n_rows = (0, t_rows) if row_range is None else row_range
    off = first // tm
    if tn is None:
        tn = next(c for c in (1024, 768, 512, 256) if n % c == 0)
    assert tn % PROJ_CHUNK == 0
    return pl.pallas_call(
        functools.partial(_proj_kernel, kind=kind), grid=(n_rows // tm, n // tn),
        in_specs=[pl.BlockSpec((tm, k), lambda i, j: (i + off, 0)), pl.BlockSpec((k, tn), lambda i, j: (0, j))],
        out_specs=pl.BlockSpec((tm, tn), lambda i, j: (i, j)),
        out_shape=jax.ShapeDtypeStruct((n_rows, n), out_dtype),
        compiler_params=_cparams(("parallel", "parallel")), name="proj_" + kind,
    )(a, w)


def _proj_qkv_kernel(a_ref, w_ref, cos_ref, sin_ref, o_ref, *, n_ctx_tiles):
    is_lat = pl.program_id(0) >= n_ctx_tiles
    cos = jnp.where(is_lat, cos_ref[...], 1.0)
    sin = jnp.where(is_lat, sin_ref[...], 0.0)

    def epilogue(c, acc):
        for hh in range(PROJ_CHUNK // A_HEAD_DIM):
            head = c * (PROJ_CHUNK // A_HEAD_DIM) + hh
            chunk = acc[:, hh * A_HEAD_DIM:(hh + 1) * A_HEAD_DIM]
            if head < A_HEADS + A_KV_HEADS:
                chunk = _rope_chunk(chunk, cos, sin)
            o_ref[:, head * A_HEAD_DIM:(head + 1) * A_HEAD_DIM] = chunk.astype(o_ref.dtype)

    _dot_then_epilogue(a_ref[...], w_ref, o_ref.shape[1], epilogue)


def _proj_qkv(a, w, rope, n_ctx_rows):
    t_rows, k = a.shape
    n = w.shape[1]
    assert n == Q_W + 2 * KV_W
    tm = _row_tile(n_ctx_rows, 1024)
    nct = n_ctx_rows // tm
    per = DEC_SEQ // tm
    pos_blk = lambda i: (jnp.where(i < nct, 0, (i - nct) % per), 0)
    return pl.pallas_call(
        functools.partial(_proj_qkv_kernel, n_ctx_tiles=nct), grid=(t_rows // tm,),
        in_specs=[pl.BlockSpec((tm, k), lambda i: (i, 0)),
                  pl.BlockSpec((k, n), lambda i: (0, 0), pipeline_mode=pl.Buffered(1)),
                  pl.BlockSpec((tm, LANES), pos_blk), pl.BlockSpec((tm, LANES), pos_blk)],
        out_specs=pl.BlockSpec((tm, n), lambda i: (i, 0)),
        out_shape=jax.ShapeDtypeStruct((t_rows, n), BF16),
        compiler_params=_cparams(("parallel",)), name="proj_qkv",
    )(a, w, rope[0], rope[1])


def _proj_split_kernel(a_ref, w_ref, o0_ref, o1_ref):
    acc = jnp.dot(a_ref[...], w_ref[...], preferred_element_type=F32)
    n0 = o0_ref.shape[1]
    o0_ref[...] = acc[:, :n0].astype(o0_ref.dtype)
    o1_ref[...] = acc[:, n0:].astype(o1_ref.dtype)


def _proj_split(a, w, n0, dtypes, n_ctx_rows):
    t_rows, k = a.shape
    n = w.shape[1]
    tm = _row_tile(n_ctx_rows, 1024)
    return pl.pallas_call(
        _proj_split_kernel, grid=(t_rows // tm,),
        in_specs=[pl.BlockSpec((tm, k), lambda i: (i, 0)), pl.BlockSpec((k, n), lambda i: (0, 0))],
        out_specs=[pl.BlockSpec((tm, n0), lambda i: (i, 0)), pl.BlockSpec((tm, n - n0), lambda i: (i, 0))],
        out_shape=[jax.ShapeDtypeStruct((t_rows, n0), dtypes[0]), jax.ShapeDtypeStruct((t_rows, n - n0), dtypes[1])],
        compiler_params=_cparams(("parallel",)), name="proj_split",
    )(a, w)


def _rope_tables():
    pos = np.arange(DEC_SEQ)
    row = (pos // GRID_W).astype(np.float32)
    col = (pos % GRID_W).astype(np.float32)
    inv = jnp.asarray(ROPE_BASE, F32) ** (-jnp.arange(ROPE_FREQS, dtype=F32) / ROPE_FREQS)
    ar = jnp.asarray(row)[:, None] * inv
    ac = jnp.asarray(col)[:, None] * inv
    cos = jnp.concatenate([jnp.cos(ar), jnp.cos(ac), jnp.cos(ar), jnp.cos(ac)], axis=1)
    sin = jnp.concatenate([-jnp.sin(ar), -jnp.sin(ac), jnp.sin(ar), jnp.sin(ac)], axis=1)
    return cos, sin


def _softmax_pv(s, sink_col, v):
    m = jnp.maximum(jnp.max(s, axis=1, keepdims=True), sink_col)
    p = jnp.exp(s - m)
    denom = jnp.sum(p, axis=1, keepdims=True) + jnp.exp(sink_col - m)
    o = jnp.dot(p.astype(BF16), v, preferred_element_type=F32)
    return o / denom


def _stack_heads(q):
    return jnp.concatenate([q[:, g * LANES:(g + 1) * LANES] for g in range(A_GROUP)], axis=0)


def _unstack_heads(o, rows):
    return jnp.concatenate([o[g * rows:(g + 1) * rows, :] for g in range(A_GROUP)], axis=1)


def _sink_col(sink_ref, h, rows):
    r = lax.broadcasted_iota(jnp.int32, (A_GROUP * rows, 1), 0)
    col = jnp.zeros((A_GROUP * rows, 1), F32)
    for g in range(A_GROUP):
        col = jnp.where((r >= g * rows) & (r < (g + 1) * rows), sink_ref[h * A_GROUP + g], col)
    return col


def _ctx_attn_kernel(sink_ref, q_ref, k_ref, v_ref, o_ref):
    h = pl.program_id(1)
    rows = q_ref.shape[0]
    q = _stack_heads(q_ref[...])
    s = lax.dot_general(q, k_ref[...], (((1,), (1,)), ((), ())), preferred_element_type=F32)
    s = s * (A_HEAD_DIM ** -0.5)
    o = _softmax_pv(s, _sink_col(sink_ref, h, rows), v_ref[...])
    o_ref[...] = _unstack_heads(o, rows).astype(o_ref.dtype)


def _ctx_attention(qkv, sink, n_seq):
    gw = A_GROUP * A_HEAD_DIM
    return pl.pallas_call(
        _ctx_attn_kernel,
        grid=(n_seq, A_KV_HEADS),
        in_specs=[pl.BlockSpec(memory_space=pltpu.MemorySpace.SMEM),
                  pl.BlockSpec((SEQ, gw), lambda s, h: (s, h)),
                  pl.BlockSpec((SEQ, A_HEAD_DIM), lambda s, h: (s, OFF_K // A_HEAD_DIM + h)),
                  pl.BlockSpec((SEQ, A_HEAD_DIM), lambda s, h: (s, OFF_V // A_HEAD_DIM + h))],
        out_specs=pl.BlockSpec((SEQ, gw), lambda s, h: (s, h)),
        out_shape=jax.ShapeDtypeStruct((n_seq * SEQ, Q_W), BF16),
        compiler_params=_cparams(("parallel", "parallel")), name="ctx_attn",
    )(sink, qkv, qkv, qkv)


def _lat_attn_kernel(sink_ref, bias_ref, q_ref, kc_ref, vc_ref, kp_ref, k0_ref, kn_ref,
                     vp_ref, v0_ref, vn_ref, o_ref):
    gw = A_GROUP * A_HEAD_DIM
    scale = A_HEAD_DIM ** -0.5
    nt = (((1,), (1,)), ((), ()))
    for h in range(A_KV_HEADS):
        hd = slice(h * A_HEAD_DIM, (h + 1) * A_HEAD_DIM)
        q = _stack_heads(q_ref[:, h * gw:(h + 1) * gw])
        keys = (kc_ref[0, :, hd], kp_ref[:, hd], k0_ref[:, hd], kn_ref[:, hd])
        vals = (vc_ref[0, :, hd], vp_ref[:, hd], v0_ref[:, hd], vn_ref[:, hd])
        s = [lax.dot_general(q, k, nt, preferred_element_type=F32) * scale for k in keys]
        s[1] = s[1] + bias_ref[0, :, :A_BLOCK]
        s[3] = s[3] + bias_ref[0, :, A_BLOCK:]
        sink = _sink_col(sink_ref, h, A_BLOCK)
        m = sink
        for part in s:
            m = jnp.maximum(m, jnp.max(part, axis=1, keepdims=True))
        denom = jnp.exp(sink - m)
        o = None
        for part, v in zip(s, vals):
            p = jnp.exp(part - m)
            denom = denom + jnp.sum(p, axis=1, keepdims=True)
            pv = jnp.dot(p.astype(BF16), v, preferred_element_type=F32)
            o = pv if o is None else o + pv
        o_ref[:, h * gw:(h + 1) * gw] = _unstack_heads(o / denom, A_BLOCK).astype(o_ref.dtype)


def _band_bias():
    ii = np.arange(A_BLOCK)[:, None]
    jj = np.arange(A_BLOCK)[None, :]
    prev = jj >= ii
    nxt = jj <= ii
    none = np.zeros((A_BLOCK, A_BLOCK), bool)
    variants = []
    for has_prev, has_next in ((False, True), (True, True), (True, False)):
        valid = np.concatenate([prev if has_prev else none, nxt if has_next else none], axis=1)
        variants.append(np.tile(np.where(valid, 0.0, NEG_BIG).astype(np.float32), (A_GROUP, 1)))
    return jnp.asarray(np.stack(variants))


def _lat_attention(qkv, k_ctx, v_ctx, sink, n_ctx_rows, n_lat):
    nb = DEC_SEQ // A_BLOCK
    base = n_ctx_rows // A_BLOCK
    n_ctx_keys = k_ctx.shape[1]
    kcol = OFF_K // KV_W
    vcol = OFF_V // KV_W
    row = lambda b, i: base + b * nb + i
    prev = lambda i: jnp.maximum(i - 1, 0)
    nxt = lambda i: jnp.minimum(i + 1, nb - 1)
    blk = (A_BLOCK, KV_W)
    ctx_blk = (1, n_ctx_keys, KV_W)
    bias_sel = lambda b, i: (jnp.where(i == 0, 0, jnp.where(i == nb - 1, 2, 1)), 0, 0)
    return pl.pallas_call(
        _lat_attn_kernel,
        grid=(n_lat, nb),
        in_specs=[pl.BlockSpec(memory_space=pltpu.MemorySpace.SMEM),
                  pl.BlockSpec((1, A_GROUP * A_BLOCK, 2 * A_BLOCK), bias_sel),
                  pl.BlockSpec((A_BLOCK, Q_W), lambda b, i: (row(b, i), 0)),
                  pl.BlockSpec(ctx_blk, lambda b, i: (b, 0, 0)),
                  pl.BlockSpec(ctx_blk, lambda b, i: (b, 0, 0)),
                  pl.BlockSpec(blk, lambda b, i: (row(b, prev(i)), kcol)),
                  pl.BlockSpec(blk, lambda b, i: (row(b, i), kcol)),
                  pl.BlockSpec(blk, lambda b, i: (row(b, nxt(i)), kcol)),
                  pl.BlockSpec(blk, lambda b, i: (row(b, prev(i)), vcol)),
                  pl.BlockSpec(blk, lambda b, i: (row(b, i), vcol)),
                  pl.BlockSpec(blk, lambda b, i: (row(b, nxt(i)), vcol))],
        out_specs=pl.BlockSpec((A_BLOCK, Q_W), lambda b, i: (b * nb + i, 0)),
        out_shape=jax.ShapeDtypeStruct((n_lat * DEC_SEQ, Q_W), BF16),
        compiler_params=_cparams(("parallel", "parallel")), name="lat_attn",
    )(sink, _band_bias(), qkv, k_ctx, v_ctx, qkv, qkv, qkv, qkv, qkv, qkv)


def _gmlp_kernel(uv_u_ref, uv_v_ref, w_ref, b_ref, o_ref):
    v = uv_v_ref[...].astype(F32)
    mu = jnp.mean(v, axis=-1, keepdims=True)
    vc = v - mu
    var = jnp.mean(vc * vc, axis=-1, keepdims=True)
    vn = (vc * lax.rsqrt(var + EPS)).astype(BF16)
    tm = v.shape[0]
    for c in range(tm // B_CHUNK):
        rows = slice(c * B_CHUNK, (c + 1) * B_CHUNK)
        for g in range(B_GROUPS):
            cols = slice(g * B_GDIM, (g + 1) * B_GDIM)
            mixed = jnp.dot(w_ref[g], vn[rows, cols], preferred_element_type=F32) + b_ref[g]
            o_ref[rows, cols] = (uv_u_ref[rows, cols].astype(F32) * mixed).astype(o_ref.dtype)


def _gmlp(uv, gm_w, gm_b, n_ctx_rows):
    t_rows = uv.shape[0]
    tm = _row_tile(n_ctx_rows, 512)
    return pl.pallas_call(
        _gmlp_kernel, grid=(t_rows // tm,),
        in_specs=[pl.BlockSpec((tm, B_WIDTH), lambda i: (i, 0)),
                  pl.BlockSpec((tm, B_WIDTH), lambda i: (i, 1)),
                  pl.BlockSpec((B_GROUPS, B_CHUNK, B_CHUNK), lambda i: (0, 0, 0)),
                  pl.BlockSpec((B_GROUPS, B_CHUNK, 1), lambda i: (0, 0, 0))],
        out_specs=pl.BlockSpec((tm, B_WIDTH), lambda i: (i, 0)),
        out_shape=jax.ShapeDtypeStruct((t_rows, B_WIDTH), BF16),
        compiler_params=_cparams(("parallel",)), name="gmlp",
    )(uv, uv, gm_w.astype(BF16), gm_b.reshape(B_GROUPS, B_CHUNK, 1))


def _split3(x):
    hi = x.astype(BF16)
    r1 = x - hi.astype(F32)
    mid = r1.astype(BF16)
    lo = (r1 - mid.astype(F32)).astype(BF16)
    return hi, mid, lo


def _dot3_rhs(sel, x):
    return sum(jnp.dot(sel, p, preferred_element_type=F32) for p in _split3(x))


def _dot3_lhs(x, sel):
    return sum(jnp.dot(p, sel, preferred_element_type=F32) for p in _split3(x))


def _dot2_lhs(x, sel):
    return sum(jnp.dot(p, sel, preferred_element_type=F32) for p in _split3(x)[:2])


def _tri(lower):
    i = lax.broadcasted_iota(jnp.int32, (C_CHUNK, C_CHUNK), 0)
    j = lax.broadcasted_iota(jnp.int32, (C_CHUNK, C_CHUNK), 1)
    return (j <= i) if lower else (j >= i)


def _head_expand(d):
    r = lax.broadcasted_iota(jnp.int32, (LANES, C_INNER), 0)
    c = lax.broadcasted_iota(jnp.int32, (LANES, C_INNER), 1)
    return (r == (c >> 6) + C_HEADS * d).astype(BF16)


def _chunk_cumsums(dt, a_row):
    a = dt * a_row
    lane = lax.broadcasted_iota(jnp.int32, a.shape, 1)
    pre = _dot3_rhs(_tri(True).astype(BF16), a)
    suf = _dot3_rhs(_tri(False).astype(BF16), a)
    cs = jnp.where(lane < C_HEADS, pre, suf)
    tot = jnp.where(lane[0:1, :] < C_HEADS, pre[C_CHUNK - 1:C_CHUNK, :], suf[0:1, :])
    return cs, tot


SSD_STEP_ROWS = 2 * C_CHUNK
assert SEQ % SSD_STEP_ROWS == 0 and DEC_SEQ % SSD_STEP_ROWS == 0


def _ssd_local_kernel(x_ref, hp_ref, hn_ref, dtr_ref, cw_ref, cb_ref, dtb_ref, alog_ref,
                      act_ref, dt_ref, st_ref, dec_ref, *, n_ctx_blocks):
    b = pl.program_id(0)
    rows = x_ref.shape[0]
    ctx_per = SEQ // rows
    lat_per = DEC_SEQ // rows
    in_seq = jnp.where(b < n_ctx_blocks, b % ctx_per, (b - n_ctx_blocks) % lat_per)
    per = jnp.where(b < n_ctx_blocks, ctx_per, lat_per)
    first = in_seq == 0
    last = in_seq == per - 1
    x = x_ref[...].astype(F32)
    halo = hp_ref.shape[0]
    prev_row = jnp.where(first, 0.0, hp_ref[...].astype(F32)[halo - 1:halo, :])
    next_row = jnp.where(last, 0.0, hn_ref[...].astype(F32)[0:1, :])
    r = lax.broadcasted_iota(jnp.int32, (SUBLANES, x.shape[1]), 0)
    down = pltpu.roll(x, 1, 0)
    up = pltpu.roll(x, rows - 1, 0)
    x_m1 = jnp.concatenate([jnp.where(r == 0, prev_row, down[:SUBLANES]), down[SUBLANES:]], axis=0)
    x_p1 = jnp.concatenate([up[:rows - SUBLANES], jnp.where(r == SUBLANES - 1, next_row, up[rows - SUBLANES:])],
                           axis=0)
    act = _silu(cw_ref[0:1, :] * x_m1 + cw_ref[1:2, :] * x + cw_ref[2:3, :] * x_p1 + cb_ref[...])
    act_bf = act.astype(BF16)
    act_ref[...] = act_bf
    z = dtr_ref[...] + dtb_ref[...]
    dt_all = jnp.maximum(z, 0.0) + jnp.log1p(jnp.exp(-jnp.abs(z)))
    dt_ref[...] = dt_all
    a_row = -jnp.exp(alog_ref[...])
    for s in range(rows // C_CHUNK):
        tok = slice(s * C_CHUNK, (s + 1) * C_CHUNK)
        dt = dt_all[tok]
        cs, tot = _chunk_cumsums(dt, a_row)
        w = jnp.exp(tot - cs) * dt
        xs = act_bf[tok, :C_INNER].astype(F32)
        bmat_t = jnp.transpose(act_bf[tok, C_INNER:C_INNER + LANES].astype(F32))
        for d in range(2):
            e = _head_expand(d)
            xw = (xs * _dot2_lhs(w, e)).astype(BF16)
            dec_ref[s, d] = jnp.exp(_dot3_lhs(tot, e))
            for g in range(C_GROUPS):
                bt = bmat_t[g * C_STATE:(g + 1) * C_STATE, :].astype(BF16)
                cols = slice(g * C_REP * C_HEAD_DIM, (g + 1) * C_REP * C_HEAD_DIM)
                st_ref[s, d, :, cols] = jnp.dot(bt, xw[:, cols], preferred_element_type=F32)


def _ssd_local(xbc, dt_raw, conv_w, conv_b, dt_bias, a_log, n_ctx_rows):
    t_rows = xbc.shape[0]
    rows = SSD_STEP_ROWS
    n_sub = rows // C_CHUNK
    nb = t_rows // rows
    nc = t_rows // C_CHUNK
    halo = 16
    per = rows // halo
    pad = lambda v: jnp.pad(v.reshape(1, -1), ((0, 0), (0, LANES - v.size)))
    kern = functools.partial(_ssd_local_kernel, n_ctx_blocks=n_ctx_rows // rows)
    row = lambda w: pl.BlockSpec((rows, w), lambda c: (c, 0))
    const = lambda shape: pl.BlockSpec(shape, lambda c: (0,) * len(shape))
    return pl.pallas_call(
        kern, grid=(nb,),
        in_specs=[row(C_XBC),
                  pl.BlockSpec((halo, C_XBC), lambda c: (jnp.maximum(c * per - 1, 0), 0)),
                  pl.BlockSpec((halo, C_XBC), lambda c: (jnp.minimum((c + 1) * per, nb * per - 1), 0)),
                  row(LANES), const((3, C_XBC)), const((1, C_XBC)), const((1, LANES)), const((1, LANES))],
        out_specs=[row(C_XBC), row(LANES),
                   pl.BlockSpec((n_sub, 2, C_STATE, C_INNER), lambda c: (c, 0, 0, 0)),
                   pl.BlockSpec((n_sub, 2, 1, C_INNER), lambda c: (c, 0, 0, 0))],
        out_shape=[jax.ShapeDtypeStruct((t_rows, C_XBC), BF16),
                   jax.ShapeDtypeStruct((t_rows, LANES), F32),
                   jax.ShapeDtypeStruct((nc, 2, C_STATE, C_INNER), F32),
                   jax.ShapeDtypeStruct((nc, 2, 1, C_INNER), F32)],
        compiler_params=_cparams(("parallel",)), name="ssd_local",
    )(xbc, xbc, xbc, dt_raw, conv_w, conv_b.reshape(1, C_XBC), pad(dt_bias), pad(a_log))


def _ssd_scan_kernel(h0_ref, st_ref, dec_ref, *refs, n_chunks):
    hin_ref, hfin_ref = refs[-2:]
    d = pl.program_id(1)

    def run(order):
        for j in range(C_INNER // LANES):
            cols = slice(j * LANES, (j + 1) * LANES)
            h = h0_ref[0, 0, :, cols]
            for c in order:
                hin_ref[c, 0, :, cols] = h
                h = h * dec_ref[c, 0, :, cols] + st_ref[c, 0, :, cols]
            hfin_ref[0, 0, :, cols] = h

    @pl.when(d == 0)
    def _():
        run(range(n_chunks))

    @pl.when(d == 1)
    def _():
        run(range(n_chunks - 1, -1, -1))


def _ssd_scan(h0, st, dec, first_chunk, n_seq, n_chunks):
    assert first_chunk % n_chunks == 0
    base = first_chunk // n_chunks
    blk = (n_chunks, 1, C_STATE, C_INNER)
    one = (1, 1, C_STATE, C_INNER)
    return pl.pallas_call(
        functools.partial(_ssd_scan_kernel, n_chunks=n_chunks),
        grid=(n_seq, 2),
        in_specs=[pl.BlockSpec(one, lambda s, d: (s, d, 0, 0)),
                  pl.BlockSpec(blk, lambda s, d: (base + s, d, 0, 0)),
                  pl.BlockSpec((n_chunks, 1, 1, C_INNER), lambda s, d: (base + s, d, 0, 0))],
        out_specs=[pl.BlockSpec(blk, lambda s, d: (s, d, 0, 0)),
                   pl.BlockSpec(one, lambda s, d: (s, d, 0, 0))],
        out_shape=[jax.ShapeDtypeStruct((n_seq * n_chunks, 2, C_STATE, C_INNER), F32),
                   jax.ShapeDtypeStruct((n_seq, 2, C_STATE, C_INNER), F32)],
        compiler_params=_cparams(("parallel", "parallel")), name="ssd_scan",
    )(h0, st, dec)


def _ssd_out_chunk(act, dt, hin, z, a_row, dskip, norm_g, side_work=()):
    side_work = iter(side_work)
    cs, _ = _chunk_cumsums(dt, a_row)
    cs2 = cs * LOG2_E
    cl_t = jnp.transpose(cs2 - jnp.log2(dt))
    x_bf = act[:, :C_INNER]
    bm = act[:, C_INNER:C_INNER + C_GROUPS * C_STATE]
    cm = act[:, C_INNER + C_GROUPS * C_STATE:]
    tril = _tri(True)
    triu = _tri(False)
    lane = lax.broadcasted_iota(jnp.int32, (C_CHUNK, LANES), 1)
    y_parts = []
    for g in range(C_GROUPS):
        ns = slice(g * C_STATE, (g + 1) * C_STATE)
        cb = lax.dot_general(cm[:, ns], bm[:, ns], (((1,), (1,)), ((), ())), preferred_element_type=F32)
        for hp in range(C_REP // 2):
            pair = []
            for h in (g * C_REP + 2 * hp, g * C_REP + 2 * hp + 1):
                hb = h + C_HEADS
                lf = jnp.exp2(jnp.where(tril, cs2[:, h:h + 1] - cl_t[h:h + 1, :], NEG_BIG))
                lb = jnp.exp2(jnp.where(triu, cs2[:, hb:hb + 1] - cl_t[hb:hb + 1, :], NEG_BIG))
                pair.append((cb * (lf + lb)).astype(BF16))
            cols = slice((g * C_REP + 2 * hp) * C_HEAD_DIM, (g * C_REP + 2 * hp + 2) * C_HEAD_DIM)
            xp = x_bf[:, cols]
            y0 = jnp.dot(pair[0], xp, preferred_element_type=F32)
            y1 = jnp.dot(pair[1], xp, preferred_element_type=F32)
            y_parts.append(jnp.where(lane < C_HEAD_DIM, y0, y1))
            for thunk in itertools.islice(side_work, 1):
                thunk()
    y = jnp.concatenate(y_parts, axis=1)
    grow_small = jnp.exp2(cs2)
    for d in range(2):
        grow = _dot2_lhs(grow_small, _head_expand(d))
        yo = []
        for g in range(C_GROUPS):
            ns = slice(g * C_STATE, (g + 1) * C_STATE)
            cols = slice(g * C_REP * C_HEAD_DIM, (g + 1) * C_REP * C_HEAD_DIM)
            yo.append(jnp.dot(cm[:, ns], hin(d, cols).astype(BF16), preferred_element_type=F32))
        y = y + jnp.concatenate(yo, axis=1) * grow
    y = y + dskip * x_bf.astype(F32)
    o = y * _silu(z.astype(F32))
    return o * lax.rsqrt(jnp.mean(o * o, axis=-1, keepdims=True) + EPS) * norm_g


def _state_to_rows(h):
    b = h.shape[0]
    return jnp.transpose(h, (0, 1, 4, 2, 3)).reshape(b, 2, C_STATE, C_INNER)


def _rows_to_state(h):
    b = h.shape[0]
    return jnp.transpose(h.reshape(b, 2, C_STATE, C_HEADS, C_HEAD_DIM), (0, 1, 3, 4, 2))


def _ssd_merge_kernel(act_ref, dt_ref, hin_ctx_ref, hin_lat_ref, z_ref, alog_ref, dskip_ref, ng_ref,
                      a0c_ref, a0l_ref, a1_ref, w_ref, g0_ref, g1_ref, g2_ref, o_ref, *, n_ctx_blocks):
    is_ctx = pl.program_id(0) < n_ctx_blocks
    a0 = jnp.where(is_ctx, a0c_ref[...], a0l_ref[...])
    n_sub = act_ref.shape[0] // C_CHUNK
    width = 2 * LANES
    n_pieces = D_MODEL // width
    assert 2 * n_pieces == n_sub * C_HEADS // 2
    acc = [None] * n_pieces

    def branch_piece(p):
        cols = slice(p * width, (p + 1) * width)
        acc[p] = (g0_ref[:, cols].astype(F32) * jnp.dot(a0, w_ref[0, :, cols], preferred_element_type=F32)
                  + g1_ref[:, cols].astype(F32) * jnp.dot(a1_ref[...], w_ref[1, :, cols],
                                                          preferred_element_type=F32))

    side = iter([f for p in range(n_pieces) for f in (functools.partial(branch_piece, p), lambda: None)])
    a_row = -jnp.exp(alog_ref[...])
    o_c = []
    for s in range(n_sub):
        tok = slice(s * C_CHUNK, (s + 1) * C_CHUNK)
        hin = lambda d, cols, s=s: jnp.where(is_ctx, hin_ctx_ref[s, d, :, cols], hin_lat_ref[s, d, :, cols])
        o_c.append(_ssd_out_chunk(act_ref[tok, :], dt_ref[tok, :], hin, z_ref[tok, :], a_row,
                                  dskip_ref[...], ng_ref[...], side_work=side).astype(BF16))
    o_c = jnp.concatenate(o_c, axis=0)
    for p in range(n_pieces):
        cols = slice(p * width, (p + 1) * width)
        o_ref[:, cols] = (acc[p] + g2_ref[:, cols].astype(F32)
                          * jnp.dot(o_c, w_ref[2, :, cols], preferred_element_type=F32)).astype(o_ref.dtype)


def _ssd_merge(act, dt, hin_ctx, hin_lat, z, a_log, d_skip, norm_g, o_a_ctx, o_a_lat, o_b, w_br, gates,
               n_ctx_rows):
    t_rows = act.shape[0]
    rows = SSD_STEP_ROWS
    ncb = n_ctx_rows // rows
    pad = lambda v: jnp.pad(v.reshape(1, -1), ((0, 0), (0, LANES - v.size)))
    row = lambda w, col=0: pl.BlockSpec((rows, w), lambda c: (c, col))
    const = lambda shape, **kw: pl.BlockSpec(shape, lambda c: (0,) * len(shape), **kw)
    hin_blk = (rows // C_CHUNK, 2, C_STATE, C_INNER)
    d_exp = jnp.repeat(d_skip, C_HEAD_DIM).reshape(1, C_INNER)
    return pl.pallas_call(
        functools.partial(_ssd_merge_kernel, n_ctx_blocks=ncb), grid=(t_rows // rows,),
        in_specs=[row(C_XBC), row(LANES),
                  pl.BlockSpec(hin_blk, lambda c: (jnp.minimum(c, ncb - 1), 0, 0, 0)),
                  pl.BlockSpec(hin_blk, lambda c: (jnp.maximum(c - ncb, 0), 0, 0, 0)),
                  row(C_INNER), const((1, LANES)), const((1, C_INNER)), const((1, C_INNER))]
                 + _pair_specs((rows, Q_W), ncb)
                 + [row(B_WIDTH), const((N_BRANCH, C_INNER, D_MODEL), pipeline_mode=pl.Buffered(1)),
                    row(D_MODEL, 0), row(D_MODEL, 1), row(D_MODEL, 2)],
        out_specs=row(D_MODEL),
        out_shape=jax.ShapeDtypeStruct((t_rows, D_MODEL), BF16),
        compiler_params=_cparams(("parallel",)), name="ssd_merge",
    )(act, dt, hin_ctx, hin_lat, z, pad(a_log), d_exp, norm_g.reshape(1, C_INNER),
      o_a_ctx, o_a_lat, o_b, w_br, gates, gates, gates)


def _out_proj_kernel(a_ref, w_ref, *refs, split_x, n_ctx_tiles):
    refs = list(refs)
    if split_x:
        x = _pick_rows(pl.program_id(0), n_ctx_tiles, refs.pop(0), refs.pop(0))
    else:
        x = refs.pop(0)[...]
    gate_ref, g_ref, sc_ref, sh_ref, wr_ref, br_ref, xo_ref, h_ref, meta_ref = refs
    tm = x.shape[0]
    half = tm // 2
    n_p = D_MODEL // PROJ_CHUNK
    half_w = D_MODEL // 2

    def matmul_pieces(rows, acc):
        a = a_ref[rows, :]

        def piece(p):
            acc[p] = jnp.dot(a, w_ref[:, p * PROJ_CHUNK:(p + 1) * PROJ_CHUNK], preferred_element_type=F32)
        return [functools.partial(piece, p) for p in range(n_p)]

    def tail_pieces(r, acc):
        rows = slice(r * half, (r + 1) * half)
        xs, hs = [None] * n_p, [None] * (2 * SUBLANES)
        inv = []

        def resid(p):
            cols = slice(p * PROJ_CHUNK, (p + 1) * PROJ_CHUNK)
            xs[p] = x[rows, cols] + gate_ref[0][:, cols] * acc[p]
            xo_ref[rows, cols] = xs[p]

        def norm(k):
            if not inv:
                ss = sum(jnp.sum(xp * xp, axis=-1, keepdims=True) for xp in xs)
                inv.append(lax.rsqrt(ss * (1.0 / D_MODEL) + EPS))
            pair = []
            for base in (0, half_w):
                c0 = base + k * LANES
                cols = slice(c0, c0 + LANES)
                xk = xs[c0 // PROJ_CHUNK][:, c0 % PROJ_CHUNK:c0 % PROJ_CHUNK + LANES]
                hk = (xk * inv[0] * g_ref[:, cols] * sc_ref[0][:, cols] + sh_ref[0][:, cols]).astype(BF16)
                hs[(0 if base == 0 else SUBLANES) + k] = hk
                pair.append(hk)
            packed = _pack_bf16_pairs(jnp.concatenate(pair, axis=1))
            h_ref[pl.ds(r * half * SUBLANES + k, half, stride=SUBLANES), :] = packed

        def route():
            logits = jnp.dot(jnp.concatenate(hs, axis=1), wr_ref[...], preferred_element_type=F32) + br_ref[...]
            meta_ref[rows, :] = _route(logits)

        return ([functools.partial(resid, p) for p in range(n_p)]
                + [functools.partial(norm, k) for k in range(SUBLANES)] + [route])

    acc0, acc1 = [None] * n_p, [None] * n_p
    for piece in matmul_pieces(slice(0, half), acc0):
        piece()
    tail0 = iter(tail_pieces(0, acc0))
    for piece in matmul_pieces(slice(half, tm), acc1):
        piece()
        for t in itertools.islice(tail0, 2):
            t()
    for t in tail0:
        t()
    for t in tail_pieces(1, acc1):
        t()


def _out_proj(merged, w_o, x, gate, norm_g, mod, router, n_ctx_rows):
    assert D_MODEL // 2 == SUBLANES * LANES
    t_rows = merged.shape[0]
    tm = _row_tile(n_ctx_rows, 512)
    nct = n_ctx_rows // tm
    split_x = isinstance(x, tuple)
    if split_x:
        x_args, x_specs = list(x), _pair_specs((tm, D_MODEL), nct)
    else:
        x_args, x_specs = [x], [pl.BlockSpec((tm, D_MODEL), lambda i: (i, 0))]
    vec_spec = pl.BlockSpec((1, 1, D_MODEL), lambda i: (_group_of_tile(i, tm, n_ctx_rows), 0, 0))
    const = lambda shape, **kw: pl.BlockSpec(shape, lambda i: (0,) * len(shape), **kw)
    return pl.pallas_call(
        functools.partial(_out_proj_kernel, split_x=split_x, n_ctx_tiles=nct),
        grid=(t_rows // tm,),
        in_specs=[pl.BlockSpec((tm, D_MODEL), lambda i: (i, 0)),
                  const((D_MODEL, D_MODEL), pipeline_mode=pl.Buffered(1))] + x_specs + [
                  vec_spec, const((1, D_MODEL)), vec_spec, vec_spec, const((D_MODEL, LANES)), const((1, LANES))],
        out_specs=[pl.BlockSpec((tm, D_MODEL), lambda i: (i, 0)),
                   pl.BlockSpec((tm * SUBLANES, LANES), lambda i: (i, 0)),
                   pl.BlockSpec((tm, LANES), lambda i: (i, 0))],
        out_shape=[jax.ShapeDtypeStruct((t_rows, D_MODEL), F32),
                   jax.ShapeDtypeStruct((t_rows * SUBLANES, LANES), jnp.uint32),
                   jax.ShapeDtypeStruct((t_rows, LANES), F32)],
        compiler_params=_cparams(("parallel",)), name="out_proj",
    )(merged, w_o, *x_args, gate, norm_g.reshape(1, D_MODEL), mod[0], mod[1], router[0], router[1])


def _moe_kernel(src_ref, ea_ref, eb_ref, nv_ref, h_hbm, meta_ref, ga_ref, ua_ref, da_ref, gb_ref, ub_ref, db_ref,
                o_ref, xbuf, sem):
    t = pl.program_id(0)
    tm = xbuf.shape[1] // SUBLANES
    slot = t % 2

    def gather(tile, dst_slot):
        for r in range(tm):
            row = pl.multiple_of(src_ref[tile * tm + r], SUBLANES)
            pltpu.make_async_copy(h_hbm.at[pl.ds(row, SUBLANES), :],
                                  xbuf.at[dst_slot, pl.ds(r * SUBLANES, SUBLANES), :],
                                  sem.at[dst_slot]).start()

    @pl.when(t == 0)
    def _():
        gather(0, 0)

    @pl.when(t + 1 < nv_ref[0])
    def _():
        gather(t + 1, 1 - slot)

    @pl.when(t < nv_ref[0])
    def _():
        pltpu.make_async_copy(h_hbm.at[pl.ds(0, tm * SUBLANES), :], xbuf.at[slot], sem.at[slot]).wait()
        x = _unpack_bf16_pairs(jnp.concatenate(
            [xbuf[slot, pl.ds(k, tm, stride=SUBLANES), :] for k in range(SUBLANES)], axis=1))

        def expert(g_ref, u_ref, d_ref, wcol):
            g = jnp.dot(x, g_ref[0, 0], preferred_element_type=F32)
            u = jnp.dot(x, u_ref[0, 0], preferred_element_type=F32)
            he = _silu(g) * u
            return wcol * jnp.dot(he.astype(BF16), d_ref[0, 0], preferred_element_type=F32)

        o = (expert(ga_ref, ua_ref, da_ref, meta_ref[:, 1:2])
             + expert(gb_ref, ub_ref, db_ref, meta_ref[:, 2:3]))
        o_ref[...] = o.astype(o_ref.dtype)

    @pl.when(t >= nv_ref[0])
    def _():
        o_ref[...] = jnp.zeros(o_ref.shape, o_ref.dtype)


def _moe_tile(n_rows):
    return 256 if n_rows >= 8192 else 128


def _moe_plan(bucket, tm):
    t_rows = bucket.shape[0]
    n_tiles = t_rows // tm + MOE_BUCKETS
    iota = jnp.arange(t_rows, dtype=jnp.int32)
    sorted_bucket, order = lax.sort((bucket, iota), num_keys=1)
    counts = jnp.sum((bucket[:, None] == jnp.arange(MOE_BUCKETS, dtype=jnp.int32)[None, :]).astype(jnp.int32), axis=0)
    padded = ((counts + tm - 1) // tm) * tm
    pad_end = jnp.cumsum(padded)
    pad_off = pad_end - padded
    off = jnp.cumsum(counts) - counts
    dest = pad_off[sorted_bucket] + iota - off[sorted_bucket]
    _, pos = lax.sort((order, dest), num_keys=1)
    tile_start = jnp.arange(n_tiles, dtype=jnp.int32) * tm
    tile_bucket = jnp.sum((tile_start[:, None] >= pad_end[None, :]).astype(jnp.int32), axis=1)
    last_bucket = jnp.max(jnp.where(counts > 0, jnp.arange(MOE_BUCKETS, dtype=jnp.int32), 0))
    tile_bucket = jnp.where(tile_start < pad_end[-1], tile_bucket, last_bucket)
    n_valid = (pad_end[-1] // tm).astype(jnp.int32).reshape(1)
    row_bucket = jnp.repeat(tile_bucket, tm)
    q = jnp.arange(n_tiles * tm, dtype=jnp.int32) - pad_off[row_bucket]
    src = order[jnp.clip(off[row_bucket] + q, 0, t_rows - 1)]
    pair_lo = jnp.asarray(np.array([0, 0, 0, 1, 1, 2], np.int32))
    pair_hi = jnp.asarray(np.array([1, 2, 3, 2, 3, 3], np.int32))
    grp = tile_bucket // MOE_PAIRS
    ea = grp * MOE_PER_GROUP + pair_lo[tile_bucket % MOE_PAIRS]
    eb = grp * MOE_PER_GROUP + pair_hi[tile_bucket % MOE_PAIRS]
    return src, pos, ea, eb, n_valid


def _moe(h2, meta, w_gate, w_up, w_down, layer):
    t_rows = meta.shape[0]
    tm = _moe_tile(t_rows)
    src, pos, ea, eb, n_valid = _moe_plan(meta[:, 0].astype(jnp.int32), tm)
    n_tiles = ea.shape[0]
    meta_sorted = meta.at[src].get(mode="promise_in_bounds")
    up_spec = lambda sel: pl.BlockSpec((1, 1, D_MODEL, MOE_HIDDEN), sel)
    dn_spec = lambda sel: pl.BlockSpec((1, 1, MOE_HIDDEN, D_MODEL), sel)
    sel_a = lambda t, src, ea, eb, nv: (layer, ea[t], 0, 0)
    sel_b = lambda t, src, ea, eb, nv: (layer, eb[t], 0, 0)
    y_sorted = pl.pallas_call(
        _moe_kernel,
        grid_spec=pltpu.PrefetchScalarGridSpec(
            num_scalar_prefetch=4, grid=(n_tiles,),
            in_specs=[pl.BlockSpec(memory_space=pl.ANY),
                      pl.BlockSpec((tm, LANES), lambda t, src, ea, eb, nv: (t, 0)),
                      up_spec(sel_a), up_spec(sel_a), dn_spec(sel_a),
                      up_spec(sel_b), up_spec(sel_b), dn_spec(sel_b)],
            out_specs=pl.BlockSpec((tm, D_MODEL), lambda t, src, ea, eb, nv: (t, 0)),
            scratch_shapes=[pltpu.VMEM((2, tm * SUBLANES, LANES), jnp.uint32), pltpu.SemaphoreType.DMA((2,))]),
        out_shape=jax.ShapeDtypeStruct((n_tiles * tm, D_MODEL), BF16),
        compiler_params=_cparams(("arbitrary",)), name="moe",
    )(src * SUBLANES, ea, eb, n_valid, h2, meta_sorted, w_gate, w_up, w_down, w_gate, w_up, w_down)
    return y_sorted.at[pos].get(mode="promise_in_bounds")


def _vec(mod_l, k):
    return mod_l[:, k * D_MODEL:(k + 1) * D_MODEL].reshape(-1, 1, D_MODEL)


def kernel(x_prompt, x_sample, cache_k, cache_v, state_ssm, c, c_ctx, w_ada, b_ada, norm1_g, w_in, attn_sink,
           gm_w, gm_b, conv_w, conv_b, dt_bias, a_log, d_skip, ssm_norm_g, w_br, w_o, norm2_g, w_rg, b_rg,
           w_re, b_re, w_gate, w_up, w_down, final_g):
    depth = w_in.shape[0]
    n_ctx, n_lat = x_prompt.shape[0], x_sample.shape[0]
    assert x_prompt.shape[1:] == (SEQ, D_MODEL) and x_sample.shape[1:] == (DEC_SEQ, D_MODEL)
    n_ctx_rows = n_ctx * SEQ
    n_lat_rows = n_lat * DEC_SEQ
    x_parts = (x_prompt.reshape(n_ctx_rows, D_MODEL), x_sample.reshape(n_lat_rows, D_MODEL))
    x = None
    bf = lambda a: a.astype(BF16)
    w_gate_bf, w_up_bf, w_down_bf = bf(w_gate), bf(w_up), bf(w_down)

    n_cond = 1 + n_lat
    cond = jnp.zeros((((n_cond + 7) // 8) * 8, D_MODEL), F32).at[0].set(c_ctx).at[1:n_cond].set(c)
    mod = _adaln(cond, w_ada, b_ada)
    rope = _rope_tables()
    head_cols = (np.arange(A_HEADS + A_KV_HEADS)[:, None] * A_HEAD_DIM + _rope_head_perm()[None, :]).reshape(-1)
    qkv_cols = np.concatenate([head_cols, np.arange(Q_W + KV_W, Q_W + 2 * KV_W)])

    ks, vs, hs = [], [], []
    pending = None
    for l in range(depth):
        shift1, scale1, gate1, shift2, scale2, gate2 = (_vec(mod[l], k) for k in range(6))
        w = w_in[l]
        if x is None:
            (h,) = _norm_stage(x_parts, norm1_g[l], n_ctx_rows, mod=(1.0 + scale1, shift1))
        else:
            x, h = _norm_stage(x, norm1_g[l], n_ctx_rows, res=pending, mod=(1.0 + scale1, shift1), write_x=True)
        qkv = _proj_qkv(h, bf(w[:, OFF_Q:OFF_GU][:, qkv_cols]), rope, n_ctx_rows)
        kv32 = _proj(h, bf(w[:, OFF_K:OFF_GU]), n_ctx_rows, out_dtype=F32, tn=2 * KV_W, row_range=(0, n_ctx_rows))
        uv = _proj(h, bf(w[:, OFF_GU:OFF_CZ]), n_ctx_rows)
        cz = _proj(h, bf(w[:, OFF_CZ:OFF_XBC]), n_ctx_rows)
        w_xbc_dt = jnp.pad(w[:, OFF_XBC:OFF_GATES], ((0, 0), (0, LANES - 2 * C_HEADS)))
        xbc, dt_raw = _proj_split(h, bf(w_xbc_dt), C_XBC, (BF16, F32), n_ctx_rows)
        gates = _proj(h, bf(w[:, OFF_GATES:]), n_ctx_rows, kind="sigmoid")

        k_ctx = bf(cache_k[:, l].reshape(n_lat, -1, KV_W)[:, :, qkv_cols[:KV_W]])
        v_ctx = bf(cache_v[:, l].reshape(n_lat, -1, KV_W))
        o_a_lat = _lat_attention(qkv, k_ctx, v_ctx, attn_sink[l], n_ctx_rows, n_lat)
        o_a_ctx = _ctx_attention(qkv, attn_sink[l], n_ctx)
        ks.append(kv32[:, :KV_W].reshape(n_ctx, SEQ, A_KV_HEADS, A_HEAD_DIM))
        vs.append(kv32[:, KV_W:].reshape(n_ctx, SEQ, A_KV_HEADS, A_HEAD_DIM))
        o_b = _gmlp(uv, gm_w[l], gm_b[l], n_ctx_rows)
        act, dt, st, dec = _ssd_local(xbc, dt_raw, conv_w[l], conv_b[l], dt_bias[l], a_log[l], n_ctx_rows)
        ncc = n_ctx_rows // C_CHUNK
        hin_l, _ = _ssd_scan(_state_to_rows(state_ssm[:, l]), st, dec, ncc, n_lat, DEC_SEQ // C_CHUNK)
        hin_c, hfin_c = _ssd_scan(jnp.zeros((n_ctx, 2, C_STATE, C_INNER), F32), st, dec, 0, n_ctx, SEQ // C_CHUNK)
        hs.append(_rows_to_state(hfin_c))
        merged = _ssd_merge(act, dt, hin_c, hin_l, cz, a_log[l], d_skip[l], ssm_norm_g[l],
                            o_a_ctx, o_a_lat, o_b, bf(w_br[l]), gates, n_ctx_rows)
        w_r = bf(jnp.pad(jnp.concatenate([w_rg[l], w_re[l]], axis=1),
                         ((0, 0), (0, LANES - MOE_GROUPS - MOE_EXPERTS))))
        b_r = jnp.pad(jnp.concatenate([b_rg[l], b_re[l]]), (0, LANES - MOE_GROUPS - MOE_EXPERTS)).reshape(1, LANES)
        x, h2, meta = _out_proj(merged, bf(w_o[l]), x_parts if x is None else x, gate1, norm2_g[l],
                                (1.0 + scale2, shift2), (w_r, b_r), n_ctx_rows)
        y_moe = _moe(h2, meta, w_gate_bf, w_up_bf, w_down_bf, l)
        pending = (y_moe, gate2)

    (y_prompt,) = _norm_stage(x, final_g, n_ctx_rows, n_rows=n_ctx_rows, res=pending, out_dtype=F32)
    (y_sample,) = _norm_stage(x, final_g, n_ctx_rows, first=n_ctx_rows, n_rows=n_lat_rows, res=pending,
                              out_dtype=F32)
    return (y_prompt.reshape(n_ctx, SEQ, D_MODEL), y_sample.reshape(n_lat, DEC_SEQ, D_MODEL),
            jnp.stack(ks, axis=1), jnp.stack(vs, axis=1), jnp.stack(hs, axis=1))
```

```python
import functools
import itertools

import numpy as np
import jax
import jax.numpy as jnp
from jax import lax
from jax.experimental import pallas as pl
from jax.experimental.pallas import tpu as pltpu

F32 = jnp.float32
BF16 = jnp.bfloat16

D_MODEL = 2048
SEQ = 256
DEC_SEQ = 4096
GRID_W = 64
EPS = 1e-6
A_HEADS = 8
A_KV_HEADS = 2
A_HEAD_DIM = 128
A_GROUP = A_HEADS // A_KV_HEADS
A_BLOCK = 128
ROPE_BASE = 10000.0
ROPE_FREQS = A_HEAD_DIM // 4
B_WIDTH = 1024
B_GROUPS = 4
B_CHUNK = 128
B_GDIM = B_WIDTH // B_GROUPS
C_INNER = 1024
C_HEAD_DIM = 64
C_HEADS = C_INNER // C_HEAD_DIM
C_GROUPS = 2
C_REP = C_HEADS // C_GROUPS
C_STATE = 64
C_CHUNK = 128
C_XBC = C_INNER + 2 * C_GROUPS * C_STATE
N_BRANCH = 3
MOE_GROUPS = 4
MOE_PER_GROUP = 4
MOE_EXPERTS = 16
MOE_HIDDEN = 512
MOE_PAIRS = 6
MOE_BUCKETS = MOE_GROUPS * MOE_PAIRS
Q_W = A_HEADS * A_HEAD_DIM
KV_W = A_KV_HEADS * A_HEAD_DIM
OFF_Q = 0
OFF_K = OFF_Q + Q_W
OFF_V = OFF_K + KV_W
OFF_GU = OFF_V + KV_W
OFF_GV = OFF_GU + B_WIDTH
OFF_CZ = OFF_GV + B_WIDTH
OFF_XBC = OFF_CZ + C_INNER
OFF_DT = OFF_XBC + C_XBC
OFF_GATES = OFF_DT + 2 * C_HEADS
LANES = 128
SUBLANES = 8
LOG2_E = 1.4426950408889634
NEG_BIG = -1e30
V7X_VMEM_LIMIT_MB = 56


def _cparams(dims, vmem_mb=48):
    assert vmem_mb <= V7X_VMEM_LIMIT_MB
    return pltpu.CompilerParams(dimension_semantics=dims, vmem_limit_bytes=vmem_mb * 1024 * 1024)


def _silu(x):
    return x * jax.nn.sigmoid(x)


def _row_tile(n_ctx_rows, cap):
    for t in (1024, 512, 256, 128):
        if t <= cap and n_ctx_rows % t == 0 and DEC_SEQ % t == 0:
            return t
    raise ValueError("context rows must be a multiple of 128")


def _group_of_tile(i, tm, n_ctx_rows):
    nct = n_ctx_rows // tm
    per = DEC_SEQ // tm
    return jnp.where(i < nct, 0, 1 + (i - nct) // per)


def _adaln_kernel(c_ref, w_ref, b_ref, o_ref):
    s = _silu(c_ref[...])
    o_ref[0] = jnp.dot(s.astype(BF16), w_ref[0].astype(BF16), preferred_element_type=F32) + b_ref[0]


def _adaln(cond, w_ada, b_ada):
    depth = w_ada.shape[0]
    r = cond.shape[0]
    tn = 1024
    return pl.pallas_call(
        _adaln_kernel,
        grid=(depth, 6 * D_MODEL // tn),
        in_specs=[pl.BlockSpec((r, D_MODEL), lambda l, j: (0, 0)),
                  pl.BlockSpec((1, D_MODEL, tn), lambda l, j: (l, 0, j)),
                  pl.BlockSpec((1, 1, tn), lambda l, j: (l, 0, j))],
        out_specs=pl.BlockSpec((1, r, tn), lambda l, j: (l, 0, j)),
        out_shape=jax.ShapeDtypeStruct((depth, r, 6 * D_MODEL), F32),
        compiler_params=_cparams(("parallel", "parallel")),
        name="adaln",
    )(cond, w_ada, b_ada.reshape(depth, 1, 6 * D_MODEL))


def _route(logits):
    lane = lax.broadcasted_iota(jnp.int32, logits.shape, 1)
    is_g = lane < MOE_GROUPS
    lg = jnp.where(is_g, logits, NEG_BIG)
    gmax = jnp.max(lg, axis=1, keepdims=True)
    gidx = jnp.min(jnp.where(lg == gmax, lane, LANES), axis=1, keepdims=True)
    p_grp = 1.0 / jnp.sum(jnp.where(is_g, jnp.exp(lg - gmax), 0.0), axis=1, keepdims=True)
    e_id = lane - MOE_GROUPS
    in_grp = (e_id >= 0) & (e_id < MOE_EXPERTS) & ((e_id >> 2) == gidx)
    le = jnp.where(in_grp, logits, NEG_BIG)
    m1 = jnp.max(le, axis=1, keepdims=True)
    i1 = jnp.min(jnp.where(le == m1, lane, LANES), axis=1, keepdims=True)
    le2 = jnp.where(lane == i1, NEG_BIG, le)
    m2 = jnp.max(le2, axis=1, keepdims=True)
    i2 = jnp.min(jnp.where(le2 == m2, lane, LANES), axis=1, keepdims=True)
    e = jnp.exp(m2 - m1)
    w1 = p_grp / (1.0 + e)
    w2 = p_grp * e / (1.0 + e)
    first_lo = i1 < i2
    lo = jnp.where(first_lo, i1, i2) - MOE_GROUPS
    hi = jnp.where(first_lo, i2, i1) - MOE_GROUPS
    w_lo = jnp.where(first_lo, w1, w2)
    w_hi = jnp.where(first_lo, w2, w1)
    lo_l = lo & 3
    hi_l = hi & 3
    base = jnp.where(lo_l == 0, 0, jnp.where(lo_l == 1, 3, 5))
    bucket = gidx * MOE_PAIRS + base + hi_l - lo_l - 1
    return jnp.where(lane == 0, bucket.astype(F32),
                     jnp.where(lane == 1, w_lo, jnp.where(lane == 2, w_hi, 0.0)))


def _pack_bf16_pairs(x):
    w = x.shape[1] // 2
    lo = lax.bitcast_convert_type(x[:, :w].astype(F32), jnp.uint32)
    hi = lax.bitcast_convert_type(x[:, w:].astype(F32), jnp.uint32)
    return (lo >> 16) | hi


def _unpack_bf16_pairs(u):
    lo = lax.bitcast_convert_type(u << 16, F32)
    hi = lax.bitcast_convert_type(u & jnp.uint32(0xFFFF0000), F32)
    return jnp.concatenate([lo.astype(BF16), hi.astype(BF16)], axis=1)


def _pick_rows(i, n_ctx_tiles, ctx_ref, lat_ref):
    return jnp.where(i < n_ctx_tiles, ctx_ref[...], lat_ref[...])


def _pair_specs(block, n_ctx_tiles, tile=lambda *g: g[0], col=lambda *g: 0):
    return [pl.BlockSpec(block, lambda *g: (jnp.minimum(tile(*g), n_ctx_tiles - 1), col(*g))),
            pl.BlockSpec(block, lambda *g: (jnp.maximum(tile(*g) - n_ctx_tiles, 0), col(*g)))]


def _norm_kernel(*refs, has_res, has_mod, write_x, split_x, n_ctx_tiles):
    refs = list(refs)
    if split_x:
        x = _pick_rows(pl.program_id(0), n_ctx_tiles, refs.pop(0), refs.pop(0))
    else:
        x = refs.pop(0)[...]
    if has_res:
        y_ref = refs.pop(0)
        gate_ref = refs.pop(0)
        x = x + gate_ref[0] * y_ref[...].astype(F32)
    g_ref = refs.pop(0)
    if has_mod:
        sc_ref = refs.pop(0)
        sh_ref = refs.pop(0)
    if write_x:
        xo_ref = refs.pop(0)
        xo_ref[...] = x
    h_ref = refs.pop(0)
    h = x * lax.rsqrt(jnp.mean(x * x, axis=-1, keepdims=True) + EPS) * g_ref[...]
    if has_mod:
        h = h * sc_ref[0] + sh_ref[0]
    h_ref[...] = h.astype(h_ref.dtype)


def _route_and_pack(h, wr_ref, br_ref, h_ref, meta_ref):
    h_bf = h.astype(BF16)
    packed = _pack_bf16_pairs(h_bf)
    n_tok = packed.shape[0]
    for k in range(SUBLANES):
        h_ref[pl.ds(k, n_tok, stride=SUBLANES), :] = packed[:, k * LANES:(k + 1) * LANES]
    logits = jnp.dot(h_bf, wr_ref[...], preferred_element_type=F32) + br_ref[...]
    meta_ref[...] = _route(logits)


def _norm_stage(x, norm_g, n_ctx_rows, *, first=0, n_rows=None, res=None, mod=None, write_x=False,
                out_dtype=BF16):
    tm = _row_tile(n_ctx_rows, 512)
    split_x = isinstance(x, tuple)
    if n_rows is None:
        n_rows = sum(a.shape[0] for a in x) if split_x else x.shape[0]
    off = first // tm
    grp = lambda i: _group_of_tile(i + off, tm, n_ctx_rows)
    vec_spec = pl.BlockSpec((1, 1, D_MODEL), lambda i: (grp(i), 0, 0))
    if split_x:
        assert first == 0
        args, specs = list(x), _pair_specs((tm, D_MODEL), n_ctx_rows // tm)
    else:
        args, specs = [x], [pl.BlockSpec((tm, D_MODEL), lambda i: (i + off, 0))]
    if res is not None:
        args += [res[0], res[1]]
        specs += [pl.BlockSpec((tm, D_MODEL), lambda i: (i + off, 0)), vec_spec]
    args.append(norm_g.reshape(1, D_MODEL))
    specs.append(pl.BlockSpec((1, D_MODEL), lambda i: (0, 0)))
    if mod is not None:
        args += [mod[0], mod[1]]
        specs += [vec_spec, vec_spec]
    out_row = pl.BlockSpec((tm, D_MODEL), lambda i: (i, 0))
    outs, out_specs = [], []
    if write_x:
        outs.append(jax.ShapeDtypeStruct((n_rows, D_MODEL), F32))
        out_specs.append(out_row)
    outs.append(jax.ShapeDtypeStruct((n_rows, D_MODEL), out_dtype))
    out_specs.append(out_row)
    kern = functools.partial(_norm_kernel, has_res=res is not None, has_mod=mod is not None,
                             write_x=write_x, split_x=split_x, n_ctx_tiles=n_ctx_rows // tm)
    return pl.pallas_call(
        kern, grid=(n_rows // tm,), in_specs=specs, out_specs=out_specs, out_shape=outs,
        compiler_params=_cparams(("parallel",)), name="norm_stage",
    )(*args)


def _rope_head_perm():
    f = ROPE_FREQS
    return np.concatenate([np.arange(0, f), np.arange(2 * f, 3 * f), np.arange(f, 2 * f), np.arange(3 * f, 4 * f)])


def _rope_chunk(x, cos, sin):
    return x * cos + pltpu.roll(x, A_HEAD_DIM // 2, 1) * sin


PROJ_CHUNK = 256


def _dot_then_epilogue(a, w_ref, n_cols, epilogue):
    prev = None
    for c in range(n_cols // PROJ_CHUNK):
        acc = jnp.dot(a, w_ref[:, c * PROJ_CHUNK:(c + 1) * PROJ_CHUNK], preferred_element_type=F32)
        if prev is not None:
            epilogue(*prev)
        prev = (c, acc)
    epilogue(*prev)


def _proj_kernel(a_ref, w_ref, o_ref, *, kind):
    def epilogue(c, acc):
        if kind == "sigmoid":
            acc = jax.nn.sigmoid(acc)
        o_ref[:, c * PROJ_CHUNK:(c + 1) * PROJ_CHUNK] = acc.astype(o_ref.dtype)

    _dot_then_epilogue(a_ref[...], w_ref, o_ref.shape[1], epilogue)


def _proj(a, w, n_ctx_rows, *, kind="plain", out_dtype=BF16, tm_cap=1024, row_range=None):
    t_rows, k = a.shape
    n = w.shape[1]
    assert n % PROJ_CHUNK == 0
    tm = _row_tile(n_ctx_rows, tm_cap)
    first, n_rows = (0, t_rows) if row_range is None else row_range
    off = first // tm
    return pl.pallas_call(
        functools.partial(_proj_kernel, kind=kind), grid=(n_rows // tm,),
        in_specs=[pl.BlockSpec((tm, k), lambda i: (i + off, 0)),
                  pl.BlockSpec((k, n), lambda i: (0, 0), pipeline_mode=pl.Buffered(1))],
        out_specs=pl.BlockSpec((tm, n), lambda i: (i, 0)),
        out_shape=jax.ShapeDtypeStruct((n_rows, n), out_dtype),
        compiler_params=_cparams(("parallel",)), name="proj_" + kind,
    )(a, w)


def _proj_qkv_kernel(a_ref, w_ref, cos_ref, sin_ref, o_ref, *, n_ctx_tiles):
    is_lat = pl.program_id(0) >= n_ctx_tiles
    cos = jnp.where(is_lat, cos_ref[...], 1.0)
    sin = jnp.where(is_lat, sin_ref[...], 0.0)

    def epilogue(c, acc):
        for hh in range(PROJ_CHUNK // A_HEAD_DIM):
            head = c * (PROJ_CHUNK // A_HEAD_DIM) + hh
            chunk = acc[:, hh * A_HEAD_DIM:(hh + 1) * A_HEAD_DIM]
            if head < A_HEADS + A_KV_HEADS:
                chunk = _rope_chunk(chunk, cos, sin)
            o_ref[:, head * A_HEAD_DIM:(head + 1) * A_HEAD_DIM] = chunk.astype(o_ref.dtype)

    _dot_then_epilogue(a_ref[...], w_ref, o_ref.shape[1], epilogue)


def _proj_qkv(a, w, rope, n_ctx_rows):
    t_rows, k = a.shape
    n = w.shape[1]
    assert n == Q_W + 2 * KV_W
    tm = _row_tile(n_ctx_rows, 1024)
    nct = n_ctx_rows // tm
    per = DEC_SEQ // tm
    pos_blk = lambda i: (jnp.where(i < nct, 0, (i - nct) % per), 0)
    return pl.pallas_call(
        functools.partial(_proj_qkv_kernel, n_ctx_tiles=nct), grid=(t_rows // tm,),
        in_specs=[pl.BlockSpec((tm, k), lambda i: (i, 0)),
                  pl.BlockSpec((k, n), lambda i: (0, 0), pipeline_mode=pl.Buffered(1)),
                  pl.BlockSpec((tm, LANES), pos_blk), pl.BlockSpec((tm, LANES), pos_blk)],
        out_specs=pl.BlockSpec((tm, n), lambda i: (i, 0)),
        out_shape=jax.ShapeDtypeStruct((t_rows, n), BF16),
        compiler_params=_cparams(("parallel",)), name="proj_qkv",
    )(a, w, rope[0], rope[1])


def _proj_split_kernel(a_ref, w_ref, o0_ref, o1_ref):
    acc = jnp.dot(a_ref[...], w_ref[...], preferred_element_type=F32)
    n0 = o0_ref.shape[1]
    o0_ref[...] = acc[:, :n0].astype(o0_ref.dtype)
    o1_ref[...] = acc[:, n0:].astype(o1_ref.dtype)


def _proj_split(a, w, n0, dtypes, n_ctx_rows):
    t_rows, k = a.shape
    n = w.shape[1]
    tm = _row_tile(n_ctx_rows, 1024)
    return pl.pallas_call(
        _proj_split_kernel, grid=(t_rows // tm,),
        in_specs=[pl.BlockSpec((tm, k), lambda i: (i, 0)), pl.BlockSpec((k, n), lambda i: (0, 0))],
        out_specs=[pl.BlockSpec((tm, n0), lambda i: (i, 0)), pl.BlockSpec((tm, n - n0), lambda i: (i, 0))],
        out_shape=[jax.ShapeDtypeStruct((t_rows, n0), dtypes[0]), jax.ShapeDtypeStruct((t_rows, n - n0), dtypes[1])],
        compiler_params=_cparams(("parallel",)), name="proj_split",
    )(a, w)


def _rope_tables():
    pos = np.arange(DEC_SEQ)
    row = (pos // GRID_W).astype(np.float32)
    col = (pos % GRID_W).astype(np.float32)
    inv = jnp.asarray(ROPE_BASE, F32) ** (-jnp.arange(ROPE_FREQS, dtype=F32) / ROPE_FREQS)
    ar = jnp.asarray(row)[:, None] * inv
    ac = jnp.asarray(col)[:, None] * inv
    cos = jnp.concatenate([jnp.cos(ar), jnp.cos(ac), jnp.cos(ar), jnp.cos(ac)], axis=1)
    sin = jnp.concatenate([-jnp.sin(ar), -jnp.sin(ac), jnp.sin(ar), jnp.sin(ac)], axis=1)
    return cos, sin


def _softmax_pv(s, sink_col, v):
    m = jnp.maximum(jnp.max(s, axis=1, keepdims=True), sink_col)
    p = jnp.exp(s - m)
    denom = jnp.sum(p, axis=1, keepdims=True) + jnp.exp(sink_col - m)
    o = jnp.dot(p.astype(BF16), v, preferred_element_type=F32)
    return o / denom


def _stack_heads(q):
    return jnp.concatenate([q[:, g * LANES:(g + 1) * LANES] for g in range(A_GROUP)], axis=0)


def _unstack_heads(o, rows):
    return jnp.concatenate([o[g * rows:(g + 1) * rows, :] for g in range(A_GROUP)], axis=1)


def _sink_col(sink_ref, h, rows):
    r = lax.broadcasted_iota(jnp.int32, (A_GROUP * rows, 1), 0)
    col = jnp.zeros((A_GROUP * rows, 1), F32)
    for g in range(A_GROUP):
        col = jnp.where((r >= g * rows) & (r < (g + 1) * rows), sink_ref[h * A_GROUP + g], col)
    return col


def _ctx_attn_kernel(sink_ref, q_ref, k_ref, v_ref, o_ref):
    h = pl.program_id(1)
    rows = q_ref.shape[0]
    q = _stack_heads(q_ref[...])
    s = lax.dot_general(q, k_ref[...], (((1,), (1,)), ((), ())), preferred_element_type=F32)
    s = s * (A_HEAD_DIM ** -0.5)
    o = _softmax_pv(s, _sink_col(sink_ref, h, rows), v_ref[...])
    o_ref[...] = _unstack_heads(o, rows).astype(o_ref.dtype)


def _ctx_attention(qkv, sink, n_seq):
    gw = A_GROUP * A_HEAD_DIM
    return pl.pallas_call(
        _ctx_attn_kernel,
        grid=(n_seq, A_KV_HEADS),
        in_specs=[pl.BlockSpec(memory_space=pltpu.MemorySpace.SMEM),
                  pl.BlockSpec((SEQ, gw), lambda s, h: (s, h)),
                  pl.BlockSpec((SEQ, A_HEAD_DIM), lambda s, h: (s, OFF_K // A_HEAD_DIM + h)),
                  pl.BlockSpec((SEQ, A_HEAD_DIM), lambda s, h: (s, OFF_V // A_HEAD_DIM + h))],
        out_specs=pl.BlockSpec((SEQ, gw), lambda s, h: (s, h)),
        out_shape=jax.ShapeDtypeStruct((n_seq * SEQ, Q_W), BF16),
        compiler_params=_cparams(("parallel", "parallel")), name="ctx_attn",
    )(sink, qkv, qkv, qkv)


def _lat_attn_kernel(sink_ref, bias_ref, q_ref, kc_ref, vc_ref, kp_ref, k0_ref, kn_ref,
                     vp_ref, v0_ref, vn_ref, o_ref):
    gw = A_GROUP * A_HEAD_DIM
    scale = A_HEAD_DIM ** -0.5
    nt = (((1,), (1,)), ((), ()))
    for h in range(A_KV_HEADS):
        hd = slice(h * A_HEAD_DIM, (h + 1) * A_HEAD_DIM)
        q = _stack_heads(q_ref[:, h * gw:(h + 1) * gw])
        keys = (kc_ref[0, :, hd], kp_ref[:, hd], k0_ref[:, hd], kn_ref[:, hd])
        vals = (vc_ref[0, :, hd], vp_ref[:, hd], v0_ref[:, hd], vn_ref[:, hd])
        s = [lax.dot_general(q, k, nt, preferred_element_type=F32) * scale for k in keys]
        s[1] = s[1] + bias_ref[0, :, :A_BLOCK]
        s[3] = s[3] + bias_ref[0, :, A_BLOCK:]
        sink = _sink_col(sink_ref, h, A_BLOCK)
        m = sink
        for part in s:
            m = jnp.maximum(m, jnp.max(part, axis=1, keepdims=True))
        denom = jnp.exp(sink - m)
        o = None
        for part, v in zip(s, vals):
            p = jnp.exp(part - m)
            denom = denom + jnp.sum(p, axis=1, keepdims=True)
            pv = jnp.dot(p.astype(BF16), v, preferred_element_type=F32)
            o = pv if o is None else o + pv
        o_ref[:, h * gw:(h + 1) * gw] = _unstack_heads(o / denom, A_BLOCK).astype(o_ref.dtype)


def _band_bias():
    ii = np.arange(A_BLOCK)[:, None]
    jj = np.arange(A_BLOCK)[None, :]
    prev = jj >= ii
    nxt = jj <= ii
    none = np.zeros((A_BLOCK, A_BLOCK), bool)
    variants = []
    for has_prev, has_next in ((False, True), (True, True), (True, False)):
        valid = np.concatenate([prev if has_prev else none, nxt if has_next else none], axis=1)
        variants.append(np.tile(np.where(valid, 0.0, NEG_BIG).astype(np.float32), (A_GROUP, 1)))
    return jnp.asarray(np.stack(variants))


def _lat_attention(qkv, k_ctx, v_ctx, sink, n_ctx_rows, n_lat):
    nb = DEC_SEQ // A_BLOCK
    base = n_ctx_rows // A_BLOCK
    n_ctx_keys = k_ctx.shape[1]
    kcol = OFF_K // KV_W
    vcol = OFF_V // KV_W
    row = lambda b, i: base + b * nb + i
    prev = lambda i: jnp.maximum(i - 1, 0)
    nxt = lambda i: jnp.minimum(i + 1, nb - 1)
    blk = (A_BLOCK, KV_W)
    ctx_blk = (1, n_ctx_keys, KV_W)
    bias_sel = lambda b, i: (jnp.where(i == 0, 0, jnp.where(i == nb - 1, 2, 1)), 0, 0)
    return pl.pallas_call(
        _lat_attn_kernel,
        grid=(n_lat, nb),
        in_specs=[pl.BlockSpec(memory_space=pltpu.MemorySpace.SMEM),
                  pl.BlockSpec((1, A_GROUP * A_BLOCK, 2 * A_BLOCK), bias_sel),
                  pl.BlockSpec((A_BLOCK, Q_W), lambda b, i: (row(b, i), 0)),
                  pl.BlockSpec(ctx_blk, lambda b, i: (b, 0, 0)),
                  pl.BlockSpec(ctx_blk, lambda b, i: (b, 0, 0)),
                  pl.BlockSpec(blk, lambda b, i: (row(b, prev(i)), kcol)),
                  pl.BlockSpec(blk, lambda b, i: (row(b, i), kcol)),
                  pl.BlockSpec(blk, lambda b, i: (row(b, nxt(i)), kcol)),
                  pl.BlockSpec(blk, lambda b, i: (row(b, prev(i)), vcol)),
                  pl.BlockSpec(blk, lambda b, i: (row(b, i), vcol)),
                  pl.BlockSpec(blk, lambda b, i: (row(b, nxt(i)), vcol))],
        out_specs=pl.BlockSpec((A_BLOCK, Q_W), lambda b, i: (b * nb + i, 0)),
        out_shape=jax.ShapeDtypeStruct((n_lat * DEC_SEQ, Q_W), BF16),
        compiler_params=_cparams(("parallel", "parallel")), name="lat_attn",
    )(sink, _band_bias(), qkv, k_ctx, v_ctx, qkv, qkv, qkv, qkv, qkv, qkv)


def _gmlp_kernel(uv_u_ref, uv_v_ref, w_ref, b_ref, o_ref):
    v = uv_v_ref[...].astype(F32)
    mu = jnp.mean(v, axis=-1, keepdims=True)
    vc = v - mu
    var = jnp.mean(vc * vc, axis=-1, keepdims=True)
    vn = (vc * lax.rsqrt(var + EPS)).astype(BF16)
    tm = v.shape[0]
    for c in range(tm // B_CHUNK):
        rows = slice(c * B_CHUNK, (c + 1) * B_CHUNK)
        for g in range(B_GROUPS):
            cols = slice(g * B_GDIM, (g + 1) * B_GDIM)
            mixed = jnp.dot(w_ref[g], vn[rows, cols], preferred_element_type=F32) + b_ref[g]
            o_ref[rows, cols] = (uv_u_ref[rows, cols].astype(F32) * mixed).astype(o_ref.dtype)


def _gmlp(uv, gm_w, gm_b, n_ctx_rows):
    t_rows = uv.shape[0]
    tm = _row_tile(n_ctx_rows, 512)
    return pl.pallas_call(
        _gmlp_kernel, grid=(t_rows // tm,),
        in_specs=[pl.BlockSpec((tm, B_WIDTH), lambda i: (i, 0)),
                  pl.BlockSpec((tm, B_WIDTH), lambda i: (i, 1)),
                  pl.BlockSpec((B_GROUPS, B_CHUNK, B_CHUNK), lambda i: (0, 0, 0)),
                  pl.BlockSpec((B_GROUPS, B_CHUNK, 1), lambda i: (0, 0, 0))],
        out_specs=pl.BlockSpec((tm, B_WIDTH), lambda i: (i, 0)),
        out_shape=jax.ShapeDtypeStruct((t_rows, B_WIDTH), BF16),
        compiler_params=_cparams(("parallel",)), name="gmlp",
    )(uv, uv, gm_w.astype(BF16), gm_b.reshape(B_GROUPS, B_CHUNK, 1))


def _split3(x):
    hi = x.astype(BF16)
    r1 = x - hi.astype(F32)
    mid = r1.astype(BF16)
    lo = (r1 - mid.astype(F32)).astype(BF16)
    return hi, mid, lo


def _dot3_rhs(sel, x):
    return sum(jnp.dot(sel, p, preferred_element_type=F32) for p in _split3(x))


def _dot3_lhs(x, sel):
    return sum(jnp.dot(p, sel, preferred_element_type=F32) for p in _split3(x))


def _dot2_lhs(x, sel):
    return sum(jnp.dot(p, sel, preferred_element_type=F32) for p in _split3(x)[:2])


def _tri(lower):
    i = lax.broadcasted_iota(jnp.int32, (C_CHUNK, C_CHUNK), 0)
    j = lax.broadcasted_iota(jnp.int32, (C_CHUNK, C_CHUNK), 1)
    return (j <= i) if lower else (j >= i)


def _head_expand(d):
    r = lax.broadcasted_iota(jnp.int32, (LANES, C_INNER), 0)
    c = lax.broadcasted_iota(jnp.int32, (LANES, C_INNER), 1)
    return (r == (c >> 6) + C_HEADS * d).astype(BF16)


def _chunk_cumsums(dt, a_row):
    a = dt * a_row
    lane = lax.broadcasted_iota(jnp.int32, a.shape, 1)
    pre = _dot3_rhs(_tri(True).astype(BF16), a)
    suf = _dot3_rhs(_tri(False).astype(BF16), a)
    cs = jnp.where(lane < C_HEADS, pre, suf)
    tot = jnp.where(lane[0:1, :] < C_HEADS, pre[C_CHUNK - 1:C_CHUNK, :], suf[0:1, :])
    return cs, tot


SSD_STEP_ROWS = 2 * C_CHUNK
assert SEQ % SSD_STEP_ROWS == 0 and DEC_SEQ % SSD_STEP_ROWS == 0


def _ssd_local_kernel(x_ref, hp_ref, hn_ref, dtr_ref, cw_ref, cb_ref, dtb_ref, alog_ref,
                      act_ref, dt_ref, st_ref, dec_ref, *, n_ctx_blocks):
    b = pl.program_id(0)
    rows = x_ref.shape[0]
    ctx_per = SEQ // rows
    lat_per = DEC_SEQ // rows
    in_seq = jnp.where(b < n_ctx_blocks, b % ctx_per, (b - n_ctx_blocks) % lat_per)
    per = jnp.where(b < n_ctx_blocks, ctx_per, lat_per)
    first = in_seq == 0
    last = in_seq == per - 1
    x = x_ref[...].astype(F32)
    halo = hp_ref.shape[0]
    prev_row = jnp.where(first, 0.0, hp_ref[...].astype(F32)[halo - 1:halo, :])
    next_row = jnp.where(last, 0.0, hn_ref[...].astype(F32)[0:1, :])
    r = lax.broadcasted_iota(jnp.int32, (SUBLANES, x.shape[1]), 0)
    down = pltpu.roll(x, 1, 0)
    up = pltpu.roll(x, rows - 1, 0)
    x_m1 = jnp.concatenate([jnp.where(r == 0, prev_row, down[:SUBLANES]), down[SUBLANES:]], axis=0)
    x_p1 = jnp.concatenate([up[:rows - SUBLANES], jnp.where(r == SUBLANES - 1, next_row, up[rows - SUBLANES:])],
                           axis=0)
    act = _silu(cw_ref[0:1, :] * x_m1 + cw_ref[1:2, :] * x + cw_ref[2:3, :] * x_p1 + cb_ref[...])
    act_bf = act.astype(BF16)
    act_ref[...] = act_bf
    z = dtr_ref[...] + dtb_ref[...]
    dt_all = jnp.maximum(z, 0.0) + jnp.log1p(jnp.exp(-jnp.abs(z)))
    dt_ref[...] = dt_all
    a_row = -jnp.exp(alog_ref[...])
    for s in range(rows // C_CHUNK):
        tok = slice(s * C_CHUNK, (s + 1) * C_CHUNK)
        dt = dt_all[tok]
        cs, tot = _chunk_cumsums(dt, a_row)
        w = jnp.exp(tot - cs) * dt
        xs = act_bf[tok, :C_INNER].astype(F32)
        bmat_t = jnp.transpose(act_bf[tok, C_INNER:C_INNER + LANES].astype(F32))
        for d in range(2):
            e = _head_expand(d)
            xw = (xs * _dot2_lhs(w, e)).astype(BF16)
            dec_ref[s, d] = jnp.exp(_dot3_lhs(tot, e))
            for g in range(C_GROUPS):
                bt = bmat_t[g * C_STATE:(g + 1) * C_STATE, :].astype(BF16)
                cols = slice(g * C_REP * C_HEAD_DIM, (g + 1) * C_REP * C_HEAD_DIM)
                st_ref[s, d, :, cols] = jnp.dot(bt, xw[:, cols], preferred_element_type=F32)


def _ssd_local(xbc, dt_raw, conv_w, conv_b, dt_bias, a_log, n_ctx_rows):
    t_rows = xbc.shape[0]
    rows = SSD_STEP_ROWS
    n_sub = rows // C_CHUNK
    nb = t_rows // rows
    nc = t_rows // C_CHUNK
    halo = 16
    per = rows // halo
    pad = lambda v: jnp.pad(v.reshape(1, -1), ((0, 0), (0, LANES - v.size)))
    kern = functools.partial(_ssd_local_kernel, n_ctx_blocks=n_ctx_rows // rows)
    row = lambda w: pl.BlockSpec((rows, w), lambda c: (c, 0))
    const = lambda shape: pl.BlockSpec(shape, lambda c: (0,) * len(shape))
    return pl.pallas_call(
        kern, grid=(nb,),
        in_specs=[row(C_XBC),
                  pl.BlockSpec((halo, C_XBC), lambda c: (jnp.maximum(c * per - 1, 0), 0)),
                  pl.BlockSpec((halo, C_XBC), lambda c: (jnp.minimum((c + 1) * per, nb * per - 1), 0)),
                  row(LANES), const((3, C_XBC)), const((1, C_XBC)), const((1, LANES)), const((1, LANES))],
        out_specs=[row(C_XBC), row(LANES),
                   pl.BlockSpec((n_sub, 2, C_STATE, C_INNER), lambda c: (c, 0, 0, 0)),
                   pl.BlockSpec((n_sub, 2, 1, C_INNER), lambda c: (c, 0, 0, 0))],
        out_shape=[jax.ShapeDtypeStruct((t_rows, C_XBC), BF16),
                   jax.ShapeDtypeStruct((t_rows, LANES), F32),
                   jax.ShapeDtypeStruct((nc, 2, C_STATE, C_INNER), F32),
                   jax.ShapeDtypeStruct((nc, 2, 1, C_INNER), F32)],
        compiler_params=_cparams(("parallel",)), name="ssd_local",
    )(xbc, xbc, xbc, dt_raw, conv_w, conv_b.reshape(1, C_XBC), pad(dt_bias), pad(a_log))


def _ssd_scan_kernel(h0_ref, st_ref, dec_ref, *refs, n_chunks):
    hin_ref, hfin_ref = refs[-2:]
    d = pl.program_id(1)

    def run(order):
        for j in range(C_INNER // LANES):
            cols = slice(j * LANES, (j + 1) * LANES)
            h = h0_ref[0, 0, :, cols]
            for c in order:
                hin_ref[c, 0, :, cols] = h
                h = h * dec_ref[c, 0, :, cols] + st_ref[c, 0, :, cols]
            hfin_ref[0, 0, :, cols] = h

    @pl.when(d == 0)
    def _():
        run(range(n_chunks))

    @pl.when(d == 1)
    def _():
        run(range(n_chunks - 1, -1, -1))


def _ssd_scan(h0, st, dec, first_chunk, n_seq, n_chunks):
    assert first_chunk % n_chunks == 0
    base = first_chunk // n_chunks
    blk = (n_chunks, 1, C_STATE, C_INNER)
    one = (1, 1, C_STATE, C_INNER)
    return pl.pallas_call(
        functools.partial(_ssd_scan_kernel, n_chunks=n_chunks),
        grid=(n_seq, 2),
        in_specs=[pl.BlockSpec(one, lambda s, d: (s, d, 0, 0)),
                  pl.BlockSpec(blk, lambda s, d: (base + s, d, 0, 0)),
                  pl.BlockSpec((n_chunks, 1, 1, C_INNER), lambda s, d: (base + s, d, 0, 0))],
        out_specs=[pl.BlockSpec(blk, lambda s, d: (s, d, 0, 0)),
                   pl.BlockSpec(one, lambda s, d: (s, d, 0, 0))],
        out_shape=[jax.ShapeDtypeStruct((n_seq * n_chunks, 2, C_STATE, C_INNER), F32),
                   jax.ShapeDtypeStruct((n_seq, 2, C_STATE, C_INNER), F32)],
        compiler_params=_cparams(("parallel", "parallel")), name="ssd_scan",
    )(h0, st, dec)


def _ssd_out_chunk(act, dt, hin, z, a_row, dskip, norm_g, side_work=()):
    side_work = iter(side_work)
    cs, _ = _chunk_cumsums(dt, a_row)
    cs2 = cs * LOG2_E
    cl_t = jnp.transpose(cs2 - jnp.log2(dt))
    x_bf = act[:, :C_INNER]
    bm = act[:, C_INNER:C_INNER + C_GROUPS * C_STATE]
    cm = act[:, C_INNER + C_GROUPS * C_STATE:]
    tril = _tri(True)
    triu = _tri(False)
    lane = lax.broadcasted_iota(jnp.int32, (C_CHUNK, LANES), 1)
    y_parts = []
    for g in range(C_GROUPS):
        ns = slice(g * C_STATE, (g + 1) * C_STATE)
        cb = lax.dot_general(cm[:, ns], bm[:, ns], (((1,), (1,)), ((), ())), preferred_element_type=F32)
        for hp in range(C_REP // 2):
            pair = []
            for h in (g * C_REP + 2 * hp, g * C_REP + 2 * hp + 1):
                hb = h + C_HEADS
                lf = jnp.exp2(jnp.where(tril, cs2[:, h:h + 1] - cl_t[h:h + 1, :], NEG_BIG))
                lb = jnp.exp2(jnp.where(triu, cs2[:, hb:hb + 1] - cl_t[hb:hb + 1, :], NEG_BIG))
                pair.append((cb * (lf + lb)).astype(BF16))
            cols = slice((g * C_REP + 2 * hp) * C_HEAD_DIM, (g * C_REP + 2 * hp + 2) * C_HEAD_DIM)
            xp = x_bf[:, cols]
            y0 = jnp.dot(pair[0], xp, preferred_element_type=F32)
            y1 = jnp.dot(pair[1], xp, preferred_element_type=F32)
            y_parts.append(jnp.where(lane < C_HEAD_DIM, y0, y1))
            for thunk in itertools.islice(side_work, 1):
                thunk()
    y = jnp.concatenate(y_parts, axis=1)
    grow_small = jnp.exp2(cs2)
    for d in range(2):
        grow = _dot2_lhs(grow_small, _head_expand(d))
        yo = []
        for g in range(C_GROUPS):
            ns = slice(g * C_STATE, (g + 1) * C_STATE)
            cols = slice(g * C_REP * C_HEAD_DIM, (g + 1) * C_REP * C_HEAD_DIM)
            yo.append(jnp.dot(cm[:, ns], hin(d, cols).astype(BF16), preferred_element_type=F32))
        y = y + jnp.concatenate(yo, axis=1) * grow
    y = y + dskip * x_bf.astype(F32)
    o = y * _silu(z.astype(F32))
    return o * lax.rsqrt(jnp.mean(o * o, axis=-1, keepdims=True) + EPS) * norm_g


def _state_to_rows(h):
    b = h.shape[0]
    return jnp.transpose(h, (0, 1, 4, 2, 3)).reshape(b, 2, C_STATE, C_INNER)


def _rows_to_state(h):
    b = h.shape[0]
    return jnp.transpose(h.reshape(b, 2, C_STATE, C_HEADS, C_HEAD_DIM), (0, 1, 3, 4, 2))


def _ssd_merge_kernel(act_ref, dt_ref, hin_ctx_ref, hin_lat_ref, z_ref, alog_ref, dskip_ref, ng_ref,
                      a0c_ref, a0l_ref, a1_ref, w_ref, g0_ref, g1_ref, g2_ref, o_ref, *, n_ctx_blocks):
    is_ctx = pl.program_id(0) < n_ctx_blocks
    a0 = jnp.where(is_ctx, a0c_ref[...], a0l_ref[...])
    n_sub = act_ref.shape[0] // C_CHUNK
    width = 2 * LANES
    n_pieces = D_MODEL // width
    assert 2 * n_pieces == n_sub * C_HEADS // 2
    acc = [None] * n_pieces

    def branch_piece(p):
        cols = slice(p * width, (p + 1) * width)
        acc[p] = (g0_ref[:, cols].astype(F32) * jnp.dot(a0, w_ref[0, :, cols], preferred_element_type=F32)
                  + g1_ref[:, cols].astype(F32) * jnp.dot(a1_ref[...], w_ref[1, :, cols],
                                                          preferred_element_type=F32))

    side = iter([f for p in range(n_pieces) for f in (functools.partial(branch_piece, p), lambda: None)])
    a_row = -jnp.exp(alog_ref[...])
    o_c = []
    for s in range(n_sub):
        tok = slice(s * C_CHUNK, (s + 1) * C_CHUNK)
        hin = lambda d, cols, s=s: jnp.where(is_ctx, hin_ctx_ref[s, d, :, cols], hin_lat_ref[s, d, :, cols])
        o_c.append(_ssd_out_chunk(act_ref[tok, :], dt_ref[tok, :], hin, z_ref[tok, :], a_row,
                                  dskip_ref[...], ng_ref[...], side_work=side).astype(BF16))
    o_c = jnp.concatenate(o_c, axis=0)
    for p in range(n_pieces):
        cols = slice(p * width, (p + 1) * width)
        o_ref[:, cols] = (acc[p] + g2_ref[:, cols].astype(F32)
                          * jnp.dot(o_c, w_ref[2, :, cols], preferred_element_type=F32)).astype(o_ref.dtype)


def _ssd_merge(act, dt, hin_ctx, hin_lat, z, a_log, d_skip, norm_g, o_a_ctx, o_a_lat, o_b, w_br, gates,
               n_ctx_rows):
    t_rows = act.shape[0]
    rows = SSD_STEP_ROWS
    ncb = n_ctx_rows // rows
    pad = lambda v: jnp.pad(v.reshape(1, -1), ((0, 0), (0, LANES - v.size)))
    row = lambda w, col=0: pl.BlockSpec((rows, w), lambda c: (c, col))
    const = lambda shape, **kw: pl.BlockSpec(shape, lambda c: (0,) * len(shape), **kw)
    hin_blk = (rows // C_CHUNK, 2, C_STATE, C_INNER)
    d_exp = jnp.repeat(d_skip, C_HEAD_DIM).reshape(1, C_INNER)
    return pl.pallas_call(
        functools.partial(_ssd_merge_kernel, n_ctx_blocks=ncb), grid=(t_rows // rows,),
        in_specs=[row(C_XBC), row(LANES),
                  pl.BlockSpec(hin_blk, lambda c: (jnp.minimum(c, ncb - 1), 0, 0, 0)),
                  pl.BlockSpec(hin_blk, lambda c: (jnp.maximum(c - ncb, 0), 0, 0, 0)),
                  row(C_INNER), const((1, LANES)), const((1, C_INNER)), const((1, C_INNER))]
                 + _pair_specs((rows, Q_W), ncb)
                 + [row(B_WIDTH), const((N_BRANCH, C_INNER, D_MODEL), pipeline_mode=pl.Buffered(1)),
                    row(D_MODEL, 0), row(D_MODEL, 1), row(D_MODEL, 2)],
        out_specs=row(D_MODEL),
        out_shape=jax.ShapeDtypeStruct((t_rows, D_MODEL), BF16),
        compiler_params=_cparams(("parallel",)), name="ssd_merge",
    )(act, dt, hin_ctx, hin_lat, z, pad(a_log), d_exp, norm_g.reshape(1, C_INNER),
      o_a_ctx, o_a_lat, o_b, w_br, gates, gates, gates)


def _out_proj_kernel(a_ref, w_ref, *refs, split_x, n_ctx_tiles):
    refs = list(refs)
    if split_x:
        x = _pick_rows(pl.program_id(0), n_ctx_tiles, refs.pop(0), refs.pop(0))
    else:
        x = refs.pop(0)[...]
    gate_ref, g_ref, sc_ref, sh_ref, wr_ref, br_ref, xo_ref, h_ref, meta_ref = refs
    tm = x.shape[0]
    half = tm // 2
    n_p = D_MODEL // PROJ_CHUNK
    half_w = D_MODEL // 2

    def matmul_pieces(rows, acc):
        a = a_ref[rows, :]

        def piece(p):
            acc[p] = jnp.dot(a, w_ref[:, p * PROJ_CHUNK:(p + 1) * PROJ_CHUNK], preferred_element_type=F32)
        return [functools.partial(piece, p) for p in range(n_p)]

    def tail_pieces(r, acc):
        rows = slice(r * half, (r + 1) * half)
        xs, hs = [None] * n_p, [None] * (2 * SUBLANES)
        inv = []

        def resid(p):
            cols = slice(p * PROJ_CHUNK, (p + 1) * PROJ_CHUNK)
            xs[p] = x[rows, cols] + gate_ref[0][:, cols] * acc[p]
            xo_ref[rows, cols] = xs[p]

        def norm(k):
            if not inv:
                ss = sum(jnp.sum(xp * xp, axis=-1, keepdims=True) for xp in xs)
                inv.append(lax.rsqrt(ss * (1.0 / D_MODEL) + EPS))
            pair = []
            for base in (0, half_w):
                c0 = base + k * LANES
                cols = slice(c0, c0 + LANES)
                xk = xs[c0 // PROJ_CHUNK][:, c0 % PROJ_CHUNK:c0 % PROJ_CHUNK + LANES]
                hk = (xk * inv[0] * g_ref[:, cols] * sc_ref[0][:, cols] + sh_ref[0][:, cols]).astype(BF16)
                hs[(0 if base == 0 else SUBLANES) + k] = hk
                pair.append(hk)
            packed = _pack_bf16_pairs(jnp.concatenate(pair, axis=1))
            h_ref[pl.ds(r * half * SUBLANES + k, half, stride=SUBLANES), :] = packed

        def route():
            logits = jnp.dot(jnp.concatenate(hs, axis=1), wr_ref[...], preferred_element_type=F32) + br_ref[...]
            meta_ref[rows, :] = _route(logits)

        return ([functools.partial(resid, p) for p in range(n_p)]
                + [functools.partial(norm, k) for k in range(SUBLANES)] + [route])

    acc0, acc1 = [None] * n_p, [None] * n_p
    for piece in matmul_pieces(slice(0, half), acc0):
        piece()
    tail0 = iter(tail_pieces(0, acc0))
    for piece in matmul_pieces(slice(half, tm), acc1):
        piece()
        for t in itertools.islice(tail0, 2):
            t()
    for t in tail0:
        t()
    for t in tail_pieces(1, acc1):
        t()


def _out_proj(merged, w_o, x, gate, norm_g, mod, router, n_ctx_rows):
    assert D_MODEL // 2 == SUBLANES * LANES
    t_rows = merged.shape[0]
    tm = _row_tile(n_ctx_rows, 512)
    nct = n_ctx_rows // tm
    split_x = isinstance(x, tuple)
    if split_x:
        x_args, x_specs = list(x), _pair_specs((tm, D_MODEL), nct)
    else:
        x_args, x_specs = [x], [pl.BlockSpec((tm, D_MODEL), lambda i: (i, 0))]
    vec_spec = pl.BlockSpec((1, 1, D_MODEL), lambda i: (_group_of_tile(i, tm, n_ctx_rows), 0, 0))
    const = lambda shape, **kw: pl.BlockSpec(shape, lambda i: (0,) * len(shape), **kw)
    return pl.pallas_call(
        functools.partial(_out_proj_kernel, split_x=split_x, n_ctx_tiles=nct),
        grid=(t_rows // tm,),
        in_specs=[pl.BlockSpec((tm, D_MODEL), lambda i: (i, 0)),
                  const((D_MODEL, D_MODEL), pipeline_mode=pl.Buffered(1))] + x_specs + [
                  vec_spec, const((1, D_MODEL)), vec_spec, vec_spec, const((D_MODEL, LANES)), const((1, LANES))],
        out_specs=[pl.BlockSpec((tm, D_MODEL), lambda i: (i, 0)),
                   pl.BlockSpec((tm * SUBLANES, LANES), lambda i: (i, 0)),
                   pl.BlockSpec((tm, LANES), lambda i: (i, 0))],
        out_shape=[jax.ShapeDtypeStruct((t_rows, D_MODEL), F32),
                   jax.ShapeDtypeStruct((t_rows * SUBLANES, LANES), jnp.uint32),
                   jax.ShapeDtypeStruct((t_rows, LANES), F32)],
        compiler_params=_cparams(("parallel",)), name="out_proj",
    )(merged, w_o, *x_args, gate, norm_g.reshape(1, D_MODEL), mod[0], mod[1], router[0], router[1])


def _moe_kernel(src_ref, ea_ref, eb_ref, nv_ref, h_hbm, meta_ref, ga_ref, ua_ref, da_ref, gb_ref, ub_ref, db_ref,
                o_ref, xbuf, sem):
    t = pl.program_id(0)
    tm = xbuf.shape[1] // SUBLANES
    slot = t % 2

    def fetch_token(tile, dst_slot, r):
        row = pl.multiple_of(src_ref[tile * tm + r], SUBLANES)
        pltpu.make_async_copy(h_hbm.at[pl.ds(row, SUBLANES), :],
                              xbuf.at[dst_slot, pl.ds(r * SUBLANES, SUBLANES), :], sem.at[dst_slot]).start()

    def wait_tile(w_slot):
        pltpu.make_async_copy(h_hbm.at[pl.ds(0, tm * SUBLANES), :], xbuf.at[w_slot], sem.at[w_slot]).wait()

    @pl.when(t == 0)
    def _():
        for r in range(tm):
            fetch_token(0, 0, r)

    @pl.when(t < nv_ref[0])
    def _():
        nxt = jnp.minimum(t + 1, nv_ref[0] - 1)
        fetches = iter([functools.partial(fetch_token, nxt, 1 - slot, r) for r in range(tm)])
        wait_tile(slot)
        x = _unpack_bf16_pairs(jnp.concatenate(
            [xbuf[slot, pl.ds(k, tm, stride=SUBLANES), :] for k in range(SUBLANES)], axis=1))
        n_hid = MOE_HIDDEN // PROJ_CHUNK
        n_out = D_MODEL // PROJ_CHUNK
        per_piece = -(-tm // (2 * (n_hid + n_out)))
        out = [None] * n_out

        def after_piece():
            for f in itertools.islice(fetches, per_piece):
                f()

        for e, (g_ref, u_ref, d_ref, lane) in enumerate(((ga_ref, ua_ref, da_ref, 1), (gb_ref, ub_ref, db_ref, 2))):
            he = []
            for c in range(n_hid):
                cols = slice(c * PROJ_CHUNK, (c + 1) * PROJ_CHUNK)
                g = jnp.dot(x, g_ref[0, 0, :, cols], preferred_element_type=F32)
                u = jnp.dot(x, u_ref[0, 0, :, cols], preferred_element_type=F32)
                he.append((_silu(g) * u).astype(BF16))
                after_piece()
            he = jnp.concatenate(he, axis=1)
            wcol = meta_ref[:, lane:lane + 1]
            for c in range(n_out):
                cols = slice(c * PROJ_CHUNK, (c + 1) * PROJ_CHUNK)
                term = wcol * jnp.dot(he, d_ref[0, 0, :, cols], preferred_element_type=F32)
                if e == 0:
                    out[c] = term
                else:
                    o_ref[:, cols] = (out[c] + term).astype(o_ref.dtype)
                after_piece()
        for f in fetches:
            f()

    @pl.when(t == nv_ref[0] - 1)
    def _():
        wait_tile(1 - slot)

    @pl.when(t >= nv_ref[0])
    def _():
        o_ref[...] = jnp.zeros(o_ref.shape, o_ref.dtype)


def _moe_tile(n_rows):
    return 256 if n_rows >= 8192 else 128


def _moe_plan(bucket, tm):
    t_rows = bucket.shape[0]
    n_tiles = t_rows // tm + MOE_BUCKETS
    iota = jnp.arange(t_rows, dtype=jnp.int32)
    sorted_bucket, order = lax.sort((bucket, iota), num_keys=1)
    counts = jnp.sum((bucket[:, None] == jnp.arange(MOE_BUCKETS, dtype=jnp.int32)[None, :]).astype(jnp.int32), axis=0)
    padded = ((counts + tm - 1) // tm) * tm
    pad_end = jnp.cumsum(padded)
    pad_off = pad_end - padded
    off = jnp.cumsum(counts) - counts
    dest = pad_off[sorted_bucket] + iota - off[sorted_bucket]
    _, pos = lax.sort((order, dest), num_keys=1)
    tile_start = jnp.arange(n_tiles, dtype=jnp.int32) * tm
    tile_bucket = jnp.sum((tile_start[:, None] >= pad_end[None, :]).astype(jnp.int32), axis=1)
    last_bucket = jnp.max(jnp.where(counts > 0, jnp.arange(MOE_BUCKETS, dtype=jnp.int32), 0))
    tile_bucket = jnp.where(tile_start < pad_end[-1], tile_bucket, last_bucket)
    n_valid = (pad_end[-1] // tm).astype(jnp.int32).reshape(1)
    row_bucket = jnp.repeat(tile_bucket, tm)
    q = jnp.arange(n_tiles * tm, dtype=jnp.int32) - pad_off[row_bucket]
    src = order[jnp.clip(off[row_bucket] + q, 0, t_rows - 1)]
    pair_lo = jnp.asarray(np.array([0, 0, 0, 1, 1, 2], np.int32))
    pair_hi = jnp.asarray(np.array([1, 2, 3, 2, 3, 3], np.int32))
    grp = tile_bucket // MOE_PAIRS
    ea = grp * MOE_PER_GROUP + pair_lo[tile_bucket % MOE_PAIRS]
    eb = grp * MOE_PER_GROUP + pair_hi[tile_bucket % MOE_PAIRS]
    return src, pos, ea, eb, n_valid


def _moe(h2, meta, w_gate, w_up, w_down, layer):
    t_rows = meta.shape[0]
    tm = _moe_tile(t_rows)
    src, pos, ea, eb, n_valid = _moe_plan(meta[:, 0].astype(jnp.int32), tm)
    n_tiles = ea.shape[0]
    meta_sorted = meta.at[src].get(mode="promise_in_bounds")
    up_spec = lambda sel: pl.BlockSpec((1, 1, D_MODEL, MOE_HIDDEN), sel)
    dn_spec = lambda sel: pl.BlockSpec((1, 1, MOE_HIDDEN, D_MODEL), sel)
    sel_a = lambda t, src, ea, eb, nv: (layer, ea[t], 0, 0)
    sel_b = lambda t, src, ea, eb, nv: (layer, eb[t], 0, 0)
    y_sorted = pl.pallas_call(
        _moe_kernel,
        grid_spec=pltpu.PrefetchScalarGridSpec(
            num_scalar_prefetch=4, grid=(n_tiles,),
            in_specs=[pl.BlockSpec(memory_space=pl.ANY),
                      pl.BlockSpec((tm, LANES), lambda t, src, ea, eb, nv: (t, 0)),
                      up_spec(sel_a), up_spec(sel_a), dn_spec(sel_a),
                      up_spec(sel_b), up_spec(sel_b), dn_spec(sel_b)],
            out_specs=pl.BlockSpec((tm, D_MODEL), lambda t, src, ea, eb, nv: (t, 0)),
            scratch_shapes=[pltpu.VMEM((2, tm * SUBLANES, LANES), jnp.uint32), pltpu.SemaphoreType.DMA((2,))]),
        out_shape=jax.ShapeDtypeStruct((n_tiles * tm, D_MODEL), BF16),
        compiler_params=_cparams(("arbitrary",)), name="moe",
    )(src * SUBLANES, ea, eb, n_valid, h2, meta_sorted, w_gate, w_up, w_down, w_gate, w_up, w_down)
    return y_sorted.at[pos].get(mode="promise_in_bounds")


def _vec(mod_l, k):
    return mod_l[:, k * D_MODEL:(k + 1) * D_MODEL].reshape(-1, 1, D_MODEL)


def kernel(x_prompt, x_sample, cache_k, cache_v, state_ssm, c, c_ctx, w_ada, b_ada, norm1_g, w_in, attn_sink,
           gm_w, gm_b, conv_w, conv_b, dt_bias, a_log, d_skip, ssm_norm_g, w_br, w_o, norm2_g, w_rg, b_rg,
           w_re, b_re, w_gate, w_up, w_down, final_g):
    depth = w_in.shape[0]
    n_ctx, n_lat = x_prompt.shape[0], x_sample.shape[0]
    assert x_prompt.shape[1:] == (SEQ, D_MODEL) and x_sample.shape[1:] == (DEC_SEQ, D_MODEL)
    n_ctx_rows = n_ctx * SEQ
    n_lat_rows = n_lat * DEC_SEQ
    x_parts = (x_prompt.reshape(n_ctx_rows, D_MODEL), x_sample.reshape(n_lat_rows, D_MODEL))
    x = None
    bf = lambda a: a.astype(BF16)
    w_gate_bf, w_up_bf, w_down_bf = bf(w_gate), bf(w_up), bf(w_down)

    n_cond = 1 + n_lat
    cond = jnp.zeros((((n_cond + 7) // 8) * 8, D_MODEL), F32).at[0].set(c_ctx).at[1:n_cond].set(c)
    mod = _adaln(cond, w_ada, b_ada)
    rope = _rope_tables()
    head_cols = (np.arange(A_HEADS + A_KV_HEADS)[:, None] * A_HEAD_DIM + _rope_head_perm()[None, :]).reshape(-1)
    qkv_cols = np.concatenate([head_cols, np.arange(Q_W + KV_W, Q_W + 2 * KV_W)])

    ks, vs, hs = [], [], []
    pending = None
    for l in range(depth):
        shift1, scale1, gate1, shift2, scale2, gate2 = (_vec(mod[l], k) for k in range(6))
        w = w_in[l]
        if x is None:
            (h,) = _norm_stage(x_parts, norm1_g[l], n_ctx_rows, mod=(1.0 + scale1, shift1))
        else:
            x, h = _norm_stage(x, norm1_g[l], n_ctx_rows, res=pending, mod=(1.0 + scale1, shift1), write_x=True)
        qkv = _proj_qkv(h, bf(w[:, OFF_Q:OFF_GU][:, qkv_cols]), rope, n_ctx_rows)
        kv32 = _proj(h, bf(w[:, OFF_K:OFF_GU]), n_ctx_rows, out_dtype=F32, row_range=(0, n_ctx_rows))
        uv = _proj(h, bf(w[:, OFF_GU:OFF_CZ]), n_ctx_rows)
        cz = _proj(h, bf(w[:, OFF_CZ:OFF_XBC]), n_ctx_rows)
        w_xbc_dt = jnp.pad(w[:, OFF_XBC:OFF_GATES], ((0, 0), (0, LANES - 2 * C_HEADS)))
        xbc, dt_raw = _proj_split(h, bf(w_xbc_dt), C_XBC, (BF16, F32), n_ctx_rows)
        gates = _proj(h, bf(w[:, OFF_GATES:]), n_ctx_rows, kind="sigmoid", tm_cap=512)

        k_ctx = bf(cache_k[:, l].reshape(n_lat, -1, KV_W)[:, :, qkv_cols[:KV_W]])
        v_ctx = bf(cache_v[:, l].reshape(n_lat, -1, KV_W))
        o_a_lat = _lat_attention(qkv, k_ctx, v_ctx, attn_sink[l], n_ctx_rows, n_lat)
        o_a_ctx = _ctx_attention(qkv, attn_sink[l], n_ctx)
        ks.append(kv32[:, :KV_W].reshape(n_ctx, SEQ, A_KV_HEADS, A_HEAD_DIM))
        vs.append(kv32[:, KV_W:].reshape(n_ctx, SEQ, A_KV_HEADS, A_HEAD_DIM))
        o_b = _gmlp(uv, gm_w[l], gm_b[l], n_ctx_rows)
        act, dt, st, dec = _ssd_local(xbc, dt_raw, conv_w[l], conv_b[l], dt_bias[l], a_log[l], n_ctx_rows)
        ncc = n_ctx_rows // C_CHUNK
        hin_l, _ = _ssd_scan(_state_to_rows(state_ssm[:, l]), st, dec, ncc, n_lat, DEC_SEQ // C_CHUNK)
        hin_c, hfin_c = _ssd_scan(jnp.zeros((n_ctx, 2, C_STATE, C_INNER), F32), st, dec, 0, n_ctx, SEQ // C_CHUNK)
        hs.append(_rows_to_state(hfin_c))
        merged = _ssd_merge(act, dt, hin_c, hin_l, cz, a_log[l], d_skip[l], ssm_norm_g[l],
                            o_a_ctx, o_a_lat, o_b, bf(w_br[l]), gates, n_ctx_rows)
        w_r = bf(jnp.pad(jnp.concatenate([w_rg[l], w_re[l]], axis=1),
                         ((0, 0), (0, LANES - MOE_GROUPS - MOE_EXPERTS))))
        b_r = jnp.pad(jnp.concatenate([b_rg[l], b_re[l]]), (0, LANES - MOE_GROUPS - MOE_EXPERTS)).reshape(1, LANES)
        x, h2, meta = _out_proj(merged, bf(w_o[l]), x_parts if x is None else x, gate1, norm2_g[l],
                                (1.0 + scale2, shift2), (w_r, b_r), n_ctx_rows)
        y_moe = _moe(h2, meta, w_gate_bf, w_up_bf, w_down_bf, l)
        pending = (y_moe, gate2)

    (y_prompt,) = _norm_stage(x, final_g, n_ctx_rows, n_rows=n_ctx_rows, res=pending, out_dtype=F32)
    (y_sample,) = _norm_stage(x, final_g, n_ctx_rows, first=n_ctx_rows, n_rows=n_lat_rows, res=pending,
                              out_dtype=F32)
    return (y_prompt.reshape(n_ctx, SEQ, D_MODEL), y_sample.reshape(n_lat, DEC_SEQ, D_MODEL),
            jnp.stack(ks, axis=1), jnp.stack(vs, axis=1), jnp.stack(hs, axis=1))
```

```python
import functools
import itertools

import numpy as np
import jax
import jax.numpy as jnp
from jax import lax
from jax.experimental import pallas as pl
from jax.experimental.pallas import tpu as pltpu

F32 = jnp.float32
BF16 = jnp.bfloat16

D_MODEL = 2048
SEQ = 256
DEC_SEQ = 4096
GRID_W = 64
EPS = 1e-6
A_HEADS = 8
A_KV_HEADS = 2
A_HEAD_DIM = 128
A_GROUP = A_HEADS // A_KV_HEADS
A_BLOCK = 128
ROPE_BASE = 10000.0
ROPE_FREQS = A_HEAD_DIM // 4
B_WIDTH = 1024
B_GROUPS = 4
B_CHUNK = 128
B_GDIM = B_WIDTH // B_GROUPS
C_INNER = 1024
C_HEAD_DIM = 64
C_HEADS = C_INNER // C_HEAD_DIM
C_GROUPS = 2
C_REP = C_HEADS // C_GROUPS
C_STATE = 64
C_CHUNK = 128
C_XBC = C_INNER + 2 * C_GROUPS * C_STATE
N_BRANCH = 3
MOE_GROUPS = 4
MOE_PER_GROUP = 4
MOE_EXPERTS = 16
MOE_HIDDEN = 512
MOE_PAIRS = 6
MOE_BUCKETS = MOE_GROUPS * MOE_PAIRS
Q_W = A_HEADS * A_HEAD_DIM
KV_W = A_KV_HEADS * A_HEAD_DIM
OFF_Q = 0
OFF_K = OFF_Q + Q_W
OFF_V = OFF_K + KV_W
OFF_GU = OFF_V + KV_W
OFF_GV = OFF_GU + B_WIDTH
OFF_CZ = OFF_GV + B_WIDTH
OFF_XBC = OFF_CZ + C_INNER
OFF_DT = OFF_XBC + C_XBC
OFF_GATES = OFF_DT + 2 * C_HEADS
LANES = 128
SUBLANES = 8
LOG2_E = 1.4426950408889634
NEG_BIG = -1e30
V7X_VMEM_LIMIT_MB = 56


def _cparams(dims, vmem_mb=48):
    assert vmem_mb <= V7X_VMEM_LIMIT_MB
    return pltpu.CompilerParams(dimension_semantics=dims, vmem_limit_bytes=vmem_mb * 1024 * 1024)


def _silu(x):
    return x * jax.nn.sigmoid(x)


def _row_tile(n_ctx_rows, cap):
    for t in (1024, 512, 256, 128):
        if t <= cap and n_ctx_rows % t == 0 and DEC_SEQ % t == 0:
            return t
    raise ValueError("context rows must be a multiple of 128")


def _group_of_tile(i, tm, n_ctx_rows):
    nct = n_ctx_rows // tm
    per = DEC_SEQ // tm
    return jnp.where(i < nct, 0, 1 + (i - nct) // per)


def _adaln_kernel(c_ref, w_ref, b_ref, o_ref):
    s = _silu(c_ref[...])
    o_ref[0] = jnp.dot(s.astype(BF16), w_ref[0].astype(BF16), preferred_element_type=F32) + b_ref[0]


def _adaln(cond, w_ada, b_ada):
    depth = w_ada.shape[0]
    r = cond.shape[0]
    tn = 1024
    return pl.pallas_call(
        _adaln_kernel,
        grid=(depth, 6 * D_MODEL // tn),
        in_specs=[pl.BlockSpec((r, D_MODEL), lambda l, j: (0, 0)),
                  pl.BlockSpec((1, D_MODEL, tn), lambda l, j: (l, 0, j)),
                  pl.BlockSpec((1, 1, tn), lambda l, j: (l, 0, j))],
        out_specs=pl.BlockSpec((1, r, tn), lambda l, j: (l, 0, j)),
        out_shape=jax.ShapeDtypeStruct((depth, r, 6 * D_MODEL), F32),
        compiler_params=_cparams(("parallel", "parallel")),
        name="adaln",
    )(cond, w_ada, b_ada.reshape(depth, 1, 6 * D_MODEL))


def _route(logits):
    lane = lax.broadcasted_iota(jnp.int32, logits.shape, 1)
    is_g = lane < MOE_GROUPS
    lg = jnp.where(is_g, logits, NEG_BIG)
    gmax = jnp.max(lg, axis=1, keepdims=True)
    gidx = jnp.min(jnp.where(lg == gmax, lane, LANES), axis=1, keepdims=True)
    p_grp = 1.0 / jnp.sum(jnp.where(is_g, jnp.exp(lg - gmax), 0.0), axis=1, keepdims=True)
    e_id = lane - MOE_GROUPS
    in_grp = (e_id >= 0) & (e_id < MOE_EXPERTS) & ((e_id >> 2) == gidx)
    le = jnp.where(in_grp, logits, NEG_BIG)
    m1 = jnp.max(le, axis=1, keepdims=True)
    i1 = jnp.min(jnp.where(le == m1, lane, LANES), axis=1, keepdims=True)
    le2 = jnp.where(lane == i1, NEG_BIG, le)
    m2 = jnp.max(le2, axis=1, keepdims=True)
    i2 = jnp.min(jnp.where(le2 == m2, lane, LANES), axis=1, keepdims=True)
    e = jnp.exp(m2 - m1)
    w1 = p_grp / (1.0 + e)
    w2 = p_grp * e / (1.0 + e)
    first_lo = i1 < i2
    lo = jnp.where(first_lo, i1, i2) - MOE_GROUPS
    hi = jnp.where(first_lo, i2, i1) - MOE_GROUPS
    w_lo = jnp.where(first_lo, w1, w2)
    w_hi = jnp.where(first_lo, w2, w1)
    lo_l = lo & 3
    hi_l = hi & 3
    base = jnp.where(lo_l == 0, 0, jnp.where(lo_l == 1, 3, 5))
    bucket = gidx * MOE_PAIRS + base + hi_l - lo_l - 1
    return jnp.where(lane == 0, bucket.astype(F32),
                     jnp.where(lane == 1, w_lo, jnp.where(lane == 2, w_hi, 0.0)))


def _pack_bf16_pairs(x):
    w = x.shape[1] // 2
    lo = lax.bitcast_convert_type(x[:, :w].astype(F32), jnp.uint32)
    hi = lax.bitcast_convert_type(x[:, w:].astype(F32), jnp.uint32)
    return (lo >> 16) | hi


def _unpack_bf16_pairs(u):
    lo = lax.bitcast_convert_type(u << 16, F32)
    hi = lax.bitcast_convert_type(u & jnp.uint32(0xFFFF0000), F32)
    return jnp.concatenate([lo.astype(BF16), hi.astype(BF16)], axis=1)


def _pick_rows(i, n_ctx_tiles, ctx_ref, lat_ref):
    return jnp.where(i < n_ctx_tiles, ctx_ref[...], lat_ref[...])


def _pair_specs(block, n_ctx_tiles, tile=lambda *g: g[0], col=lambda *g: 0):
    return [pl.BlockSpec(block, lambda *g: (jnp.minimum(tile(*g), n_ctx_tiles - 1), col(*g))),
            pl.BlockSpec(block, lambda *g: (jnp.maximum(tile(*g) - n_ctx_tiles, 0), col(*g)))]


def _norm_kernel(*refs, has_res, has_mod, write_x, split_x, n_ctx_tiles):
    refs = list(refs)
    if split_x:
        x = _pick_rows(pl.program_id(0), n_ctx_tiles, refs.pop(0), refs.pop(0))
    else:
        x = refs.pop(0)[...]
    if has_res:
        y_ref = refs.pop(0)
        gate_ref = refs.pop(0)
        x = x + gate_ref[0] * y_ref[...].astype(F32)
    g_ref = refs.pop(0)
    if has_mod:
        sc_ref = refs.pop(0)
        sh_ref = refs.pop(0)
    if write_x:
        xo_ref = refs.pop(0)
        xo_ref[...] = x
    h_ref = refs.pop(0)
    h = x * lax.rsqrt(jnp.mean(x * x, axis=-1, keepdims=True) + EPS) * g_ref[...]
    if has_mod:
        h = h * sc_ref[0] + sh_ref[0]
    h_ref[...] = h.astype(h_ref.dtype)


def _route_and_pack(h, wr_ref, br_ref, h_ref, meta_ref):
    h_bf = h.astype(BF16)
    packed = _pack_bf16_pairs(h_bf)
    n_tok = packed.shape[0]
    for k in range(SUBLANES):
        h_ref[pl.ds(k, n_tok, stride=SUBLANES), :] = packed[:, k * LANES:(k + 1) * LANES]
    logits = jnp.dot(h_bf, wr_ref[...], preferred_element_type=F32) + br_ref[...]
    meta_ref[...] = _route(logits)


def _norm_stage(x, norm_g, n_ctx_rows, *, first=0, n_rows=None, res=None, mod=None, write_x=False,
                out_dtype=BF16):
    tm = _row_tile(n_ctx_rows, 512)
    split_x = isinstance(x, tuple)
    if n_rows is None:
        n_rows = sum(a.shape[0] for a in x) if split_x else x.shape[0]
    off = first // tm
    grp = lambda i: _group_of_tile(i + off, tm, n_ctx_rows)
    vec_spec = pl.BlockSpec((1, 1, D_MODEL), lambda i: (grp(i), 0, 0))
    if split_x:
        assert first == 0
        args, specs = list(x), _pair_specs((tm, D_MODEL), n_ctx_rows // tm)
    else:
        args, specs = [x], [pl.BlockSpec((tm, D_MODEL), lambda i: (i + off, 0))]
    if res is not None:
        args += [res[0], res[1]]
        specs += [pl.BlockSpec((tm, D_MODEL), lambda i: (i + off, 0)), vec_spec]
    args.append(norm_g.reshape(1, D_MODEL))
    specs.append(pl.BlockSpec((1, D_MODEL), lambda i: (0, 0)))
    if mod is not None:
        args += [mod[0], mod[1]]
        specs += [vec_spec, vec_spec]
    out_row = pl.BlockSpec((tm, D_MODEL), lambda i: (i, 0))
    outs, out_specs = [], []
    if write_x:
        outs.append(jax.ShapeDtypeStruct((n_rows, D_MODEL), F32))
        out_specs.append(out_row)
    outs.append(jax.ShapeDtypeStruct((n_rows, D_MODEL), out_dtype))
    out_specs.append(out_row)
    kern = functools.partial(_norm_kernel, has_res=res is not None, has_mod=mod is not None,
                             write_x=write_x, split_x=split_x, n_ctx_tiles=n_ctx_rows // tm)
    return pl.pallas_call(
        kern, grid=(n_rows // tm,), in_specs=specs, out_specs=out_specs, out_shape=outs,
        compiler_params=_cparams(("parallel",)), name="norm_stage",
    )(*args)


def _rope_head_perm():
    f = ROPE_FREQS
    return np.concatenate([np.arange(0, f), np.arange(2 * f, 3 * f), np.arange(f, 2 * f), np.arange(3 * f, 4 * f)])


def _rope_chunk(x, cos, sin):
    return x * cos + pltpu.roll(x, A_HEAD_DIM // 2, 1) * sin


PROJ_CHUNK = 256


def _dot_then_epilogue(a, w_ref, n_cols, epilogue):
    prev = None
    for c in range(n_cols // PROJ_CHUNK):
        acc = jnp.dot(a, w_ref[:, c * PROJ_CHUNK:(c + 1) * PROJ_CHUNK], preferred_element_type=F32)
        if prev is not None:
            epilogue(*prev)
        prev = (c, acc)
    epilogue(*prev)


def _proj_kernel(a_ref, w_ref, o_ref, *, kind):
    def epilogue(c, acc):
        if kind == "sigmoid":
            acc = jax.nn.sigmoid(acc)
        o_ref[:, c * PROJ_CHUNK:(c + 1) * PROJ_CHUNK] = acc.astype(o_ref.dtype)

    _dot_then_epilogue(a_ref[...], w_ref, o_ref.shape[1], epilogue)


def _proj(a, w, n_ctx_rows, *, kind="plain", out_dtype=BF16, tm_cap=1024, row_range=None):
    t_rows, k = a.shape
    n = w.shape[1]
    assert n % PROJ_CHUNK == 0
    tm = _row_tile(n_ctx_rows, tm_cap)
    first, n_rows = (0, t_rows) if row_range is None else row_range
    off = first // tm
    return pl.pallas_call(
        functools.partial(_proj_kernel, kind=kind), grid=(n_rows // tm,),
        in_specs=[pl.BlockSpec((tm, k), lambda i: (i + off, 0)),
                  pl.BlockSpec((k, n), lambda i: (0, 0), pipeline_mode=pl.Buffered(1))],
        out_specs=pl.BlockSpec((tm, n), lambda i: (i, 0)),
        out_shape=jax.ShapeDtypeStruct((n_rows, n), out_dtype),
        compiler_params=_cparams(("parallel",)), name="proj_" + kind,
    )(a, w)


def _proj_qkv_kernel(a_ref, w_ref, cos_ref, sin_ref, o_ref, *, n_ctx_tiles):
    is_lat = pl.program_id(0) >= n_ctx_tiles
    cos = jnp.where(is_lat, cos_ref[...], 1.0)
    sin = jnp.where(is_lat, sin_ref[...], 0.0)

    def epilogue(c, acc):
        for hh in range(PROJ_CHUNK // A_HEAD_DIM):
            head = c * (PROJ_CHUNK // A_HEAD_DIM) + hh
            chunk = acc[:, hh * A_HEAD_DIM:(hh + 1) * A_HEAD_DIM]
            if head < A_HEADS + A_KV_HEADS:
                chunk = _rope_chunk(chunk, cos, sin)
            o_ref[:, head * A_HEAD_DIM:(head + 1) * A_HEAD_DIM] = chunk.astype(o_ref.dtype)

    _dot_then_epilogue(a_ref[...], w_ref, o_ref.shape[1], epilogue)


def _proj_qkv(a, w, rope, n_ctx_rows):
    t_rows, k = a.shape
    n = w.shape[1]
    assert n == Q_W + 2 * KV_W
    tm = _row_tile(n_ctx_rows, 1024)
    nct = n_ctx_rows // tm
    per = DEC_SEQ // tm
    pos_blk = lambda i: (jnp.where(i < nct, 0, (i - nct) % per), 0)
    return pl.pallas_call(
        functools.partial(_proj_qkv_kernel, n_ctx_tiles=nct), grid=(t_rows // tm,),
        in_specs=[pl.BlockSpec((tm, k), lambda i: (i, 0)),
                  pl.BlockSpec((k, n), lambda i: (0, 0), pipeline_mode=pl.Buffered(1)),
                  pl.BlockSpec((tm, LANES), pos_blk), pl.BlockSpec((tm, LANES), pos_blk)],
        out_specs=pl.BlockSpec((tm, n), lambda i: (i, 0)),
        out_shape=jax.ShapeDtypeStruct((t_rows, n), BF16),
        compiler_params=_cparams(("parallel",)), name="proj_qkv",
    )(a, w, rope[0], rope[1])


def _proj_split_kernel(a_ref, w_ref, o0_ref, o1_ref):
    acc = jnp.dot(a_ref[...], w_ref[...], preferred_element_type=F32)
    n0 = o0_ref.shape[1]
    o0_ref[...] = acc[:, :n0].astype(o0_ref.dtype)
    o1_ref[...] = acc[:, n0:].astype(o1_ref.dtype)


def _proj_split(a, w, n0, dtypes, n_ctx_rows):
    t_rows, k = a.shape
    n = w.shape[1]
    tm = _row_tile(n_ctx_rows, 1024)
    return pl.pallas_call(
        _proj_split_kernel, grid=(t_rows // tm,),
        in_specs=[pl.BlockSpec((tm, k), lambda i: (i, 0)), pl.BlockSpec((k, n), lambda i: (0, 0))],
        out_specs=[pl.BlockSpec((tm, n0), lambda i: (i, 0)), pl.BlockSpec((tm, n - n0), lambda i: (i, 0))],
        out_shape=[jax.ShapeDtypeStruct((t_rows, n0), dtypes[0]), jax.ShapeDtypeStruct((t_rows, n - n0), dtypes[1])],
        compiler_params=_cparams(("parallel",)), name="proj_split",
    )(a, w)


def _rope_tables():
    pos = np.arange(DEC_SEQ)
    row = (pos // GRID_W).astype(np.float32)
    col = (pos % GRID_W).astype(np.float32)
    inv = jnp.asarray(ROPE_BASE, F32) ** (-jnp.arange(ROPE_FREQS, dtype=F32) / ROPE_FREQS)
    ar = jnp.asarray(row)[:, None] * inv
    ac = jnp.asarray(col)[:, None] * inv
    cos = jnp.concatenate([jnp.cos(ar), jnp.cos(ac), jnp.cos(ar), jnp.cos(ac)], axis=1)
    sin = jnp.concatenate([-jnp.sin(ar), -jnp.sin(ac), jnp.sin(ar), jnp.sin(ac)], axis=1)
    return cos, sin


def _softmax_pv(s, sink_col, v):
    m = jnp.maximum(jnp.max(s, axis=1, keepdims=True), sink_col)
    p = jnp.exp(s - m)
    denom = jnp.sum(p, axis=1, keepdims=True) + jnp.exp(sink_col - m)
    o = jnp.dot(p.astype(BF16), v, preferred_element_type=F32)
    return o / denom


def _stack_heads(q):
    return jnp.concatenate([q[:, g * LANES:(g + 1) * LANES] for g in range(A_GROUP)], axis=0)


def _unstack_heads(o, rows):
    return jnp.concatenate([o[g * rows:(g + 1) * rows, :] for g in range(A_GROUP)], axis=1)


def _sink_col(sink_ref, h, rows):
    r = lax.broadcasted_iota(jnp.int32, (A_GROUP * rows, 1), 0)
    col = jnp.zeros((A_GROUP * rows, 1), F32)
    for g in range(A_GROUP):
        col = jnp.where((r >= g * rows) & (r < (g + 1) * rows), sink_ref[h * A_GROUP + g], col)
    return col


def _ctx_attn_kernel(sink_ref, q_ref, k_ref, v_ref, o_ref):
    h = pl.program_id(1)
    rows = q_ref.shape[0]
    q = _stack_heads(q_ref[...])
    s = lax.dot_general(q, k_ref[...], (((1,), (1,)), ((), ())), preferred_element_type=F32)
    s = s * (A_HEAD_DIM ** -0.5)
    o = _softmax_pv(s, _sink_col(sink_ref, h, rows), v_ref[...])
    o_ref[...] = _unstack_heads(o, rows).astype(o_ref.dtype)


def _ctx_attention(qkv, sink, n_seq):
    gw = A_GROUP * A_HEAD_DIM
    return pl.pallas_call(
        _ctx_attn_kernel,
        grid=(n_seq, A_KV_HEADS),
        in_specs=[pl.BlockSpec(memory_space=pltpu.MemorySpace.SMEM),
                  pl.BlockSpec((SEQ, gw), lambda s, h: (s, h)),
                  pl.BlockSpec((SEQ, A_HEAD_DIM), lambda s, h: (s, OFF_K // A_HEAD_DIM + h)),
                  pl.BlockSpec((SEQ, A_HEAD_DIM), lambda s, h: (s, OFF_V // A_HEAD_DIM + h))],
        out_specs=pl.BlockSpec((SEQ, gw), lambda s, h: (s, h)),
        out_shape=jax.ShapeDtypeStruct((n_seq * SEQ, Q_W), BF16),
        compiler_params=_cparams(("parallel", "parallel")), name="ctx_attn",
    )(sink, qkv, qkv, qkv)


def _lat_attn_kernel(sink_ref, bias_ref, q_ref, kc_ref, vc_ref, kp_ref, k0_ref, kn_ref,
                     vp_ref, v0_ref, vn_ref, o_ref):
    gw = A_GROUP * A_HEAD_DIM
    scale = A_HEAD_DIM ** -0.5
    nt = (((1,), (1,)), ((), ()))
    for h in range(A_KV_HEADS):
        hd = slice(h * A_HEAD_DIM, (h + 1) * A_HEAD_DIM)
        q = _stack_heads(q_ref[:, h * gw:(h + 1) * gw])
        keys = (kc_ref[0, :, hd], kp_ref[:, hd], k0_ref[:, hd], kn_ref[:, hd])
        vals = (vc_ref[0, :, hd], vp_ref[:, hd], v0_ref[:, hd], vn_ref[:, hd])
        s = [lax.dot_general(q, k, nt, preferred_element_type=F32) * scale for k in keys]
        s[1] = s[1] + bias_ref[0, :, :A_BLOCK]
        s[3] = s[3] + bias_ref[0, :, A_BLOCK:]
        sink = _sink_col(sink_ref, h, A_BLOCK)
        m = sink
        for part in s:
            m = jnp.maximum(m, jnp.max(part, axis=1, keepdims=True))
        denom = jnp.exp(sink - m)
        o = None
        for part, v in zip(s, vals):
            p = jnp.exp(part - m)
            denom = denom + jnp.sum(p, axis=1, keepdims=True)
            pv = jnp.dot(p.astype(BF16), v, preferred_element_type=F32)
            o = pv if o is None else o + pv
        o_ref[:, h * gw:(h + 1) * gw] = _unstack_heads(o / denom, A_BLOCK).astype(o_ref.dtype)


def _band_bias():
    ii = np.arange(A_BLOCK)[:, None]
    jj = np.arange(A_BLOCK)[None, :]
    prev = jj >= ii
    nxt = jj <= ii
    none = np.zeros((A_BLOCK, A_BLOCK), bool)
    variants = []
    for has_prev, has_next in ((False, True), (True, True), (True, False)):
        valid = np.concatenate([prev if has_prev else none, nxt if has_next else none], axis=1)
        variants.append(np.tile(np.where(valid, 0.0, NEG_BIG).astype(np.float32), (A_GROUP, 1)))
    return jnp.asarray(np.stack(variants))


def _lat_attention(qkv, k_ctx, v_ctx, sink, n_ctx_rows, n_lat):
    nb = DEC_SEQ // A_BLOCK
    base = n_ctx_rows // A_BLOCK
    n_ctx_keys = k_ctx.shape[1]
    kcol = OFF_K // KV_W
    vcol = OFF_V // KV_W
    row = lambda b, i: base + b * nb + i
    prev = lambda i: jnp.maximum(i - 1, 0)
    nxt = lambda i: jnp.minimum(i + 1, nb - 1)
    blk = (A_BLOCK, KV_W)
    ctx_blk = (1, n_ctx_keys, KV_W)
    bias_sel = lambda b, i: (jnp.where(i == 0, 0, jnp.where(i == nb - 1, 2, 1)), 0, 0)
    return pl.pallas_call(
        _lat_attn_kernel,
        grid=(n_lat, nb),
        in_specs=[pl.BlockSpec(memory_space=pltpu.MemorySpace.SMEM),
                  pl.BlockSpec((1, A_GROUP * A_BLOCK, 2 * A_BLOCK), bias_sel),
                  pl.BlockSpec((A_BLOCK, Q_W), lambda b, i: (row(b, i), 0)),
                  pl.BlockSpec(ctx_blk, lambda b, i: (b, 0, 0)),
                  pl.BlockSpec(ctx_blk, lambda b, i: (b, 0, 0)),
                  pl.BlockSpec(blk, lambda b, i: (row(b, prev(i)), kcol)),
                  pl.BlockSpec(blk, lambda b, i: (row(b, i), kcol)),
                  pl.BlockSpec(blk, lambda b, i: (row(b, nxt(i)), kcol)),
                  pl.BlockSpec(blk, lambda b, i: (row(b, prev(i)), vcol)),
                  pl.BlockSpec(blk, lambda b, i: (row(b, i), vcol)),
                  pl.BlockSpec(blk, lambda b, i: (row(b, nxt(i)), vcol))],
        out_specs=pl.BlockSpec((A_BLOCK, Q_W), lambda b, i: (b * nb + i, 0)),
        out_shape=jax.ShapeDtypeStruct((n_lat * DEC_SEQ, Q_W), BF16),
        compiler_params=_cparams(("parallel", "parallel")), name="lat_attn",
    )(sink, _band_bias(), qkv, k_ctx, v_ctx, qkv, qkv, qkv, qkv, qkv, qkv)


def _gmlp_kernel(uv_u_ref, uv_v_ref, w_ref, b_ref, o_ref):
    v = uv_v_ref[...].astype(F32)
    mu = jnp.mean(v, axis=-1, keepdims=True)
    vc = v - mu
    var = jnp.mean(vc * vc, axis=-1, keepdims=True)
    vn = (vc * lax.rsqrt(var + EPS)).astype(BF16)
    tm = v.shape[0]
    for c in range(tm // B_CHUNK):
        rows = slice(c * B_CHUNK, (c + 1) * B_CHUNK)
        for g in range(B_GROUPS):
            cols = slice(g * B_GDIM, (g + 1) * B_GDIM)
            mixed = jnp.dot(w_ref[g], vn[rows, cols], preferred_element_type=F32) + b_ref[g]
            o_ref[rows, cols] = (uv_u_ref[rows, cols].astype(F32) * mixed).astype(o_ref.dtype)


def _gmlp(uv, gm_w, gm_b, n_ctx_rows):
    t_rows = uv.shape[0]
    tm = _row_tile(n_ctx_rows, 512)
    return pl.pallas_call(
        _gmlp_kernel, grid=(t_rows // tm,),
        in_specs=[pl.BlockSpec((tm, B_WIDTH), lambda i: (i, 0)),
                  pl.BlockSpec((tm, B_WIDTH), lambda i: (i, 1)),
                  pl.BlockSpec((B_GROUPS, B_CHUNK, B_CHUNK), lambda i: (0, 0, 0)),
                  pl.BlockSpec((B_GROUPS, B_CHUNK, 1), lambda i: (0, 0, 0))],
        out_specs=pl.BlockSpec((tm, B_WIDTH), lambda i: (i, 0)),
        out_shape=jax.ShapeDtypeStruct((t_rows, B_WIDTH), BF16),
        compiler_params=_cparams(("parallel",)), name="gmlp",
    )(uv, uv, gm_w.astype(BF16), gm_b.reshape(B_GROUPS, B_CHUNK, 1))


def _split3(x):
    hi = x.astype(BF16)
    r1 = x - hi.astype(F32)
    mid = r1.astype(BF16)
    lo = (r1 - mid.astype(F32)).astype(BF16)
    return hi, mid, lo


def _dot3_rhs(sel, x):
    return sum(jnp.dot(sel, p, preferred_element_type=F32) for p in _split3(x))


def _dot3_lhs(x, sel):
    return sum(jnp.dot(p, sel, preferred_element_type=F32) for p in _split3(x))


def _dot2_lhs(x, sel):
    return sum(jnp.dot(p, sel, preferred_element_type=F32) for p in _split3(x)[:2])


def _tri(lower):
    i = lax.broadcasted_iota(jnp.int32, (C_CHUNK, C_CHUNK), 0)
    j = lax.broadcasted_iota(jnp.int32, (C_CHUNK, C_CHUNK), 1)
    return (j <= i) if lower else (j >= i)


def _head_expand(d):
    r = lax.broadcasted_iota(jnp.int32, (LANES, C_INNER), 0)
    c = lax.broadcasted_iota(jnp.int32, (LANES, C_INNER), 1)
    return (r == (c >> 6) + C_HEADS * d).astype(BF16)


def _chunk_cumsums(dt, a_row):
    a = dt * a_row
    lane = lax.broadcasted_iota(jnp.int32, a.shape, 1)
    pre = _dot3_rhs(_tri(True).astype(BF16), a)
    suf = _dot3_rhs(_tri(False).astype(BF16), a)
    cs = jnp.where(lane < C_HEADS, pre, suf)
    tot = jnp.where(lane[0:1, :] < C_HEADS, pre[C_CHUNK - 1:C_CHUNK, :], suf[0:1, :])
    return cs, tot


SSD_STEP_ROWS = 2 * C_CHUNK
assert SEQ % SSD_STEP_ROWS == 0 and DEC_SEQ % SSD_STEP_ROWS == 0


def _ssd_local_kernel(x_ref, hp_ref, hn_ref, dtr_ref, cw_ref, cb_ref, dtb_ref, alog_ref,
                      act_ref, dt_ref, st_ref, dec_ref, *, n_ctx_blocks):
    b = pl.program_id(0)
    rows = x_ref.shape[0]
    ctx_per = SEQ // rows
    lat_per = DEC_SEQ // rows
    in_seq = jnp.where(b < n_ctx_blocks, b % ctx_per, (b - n_ctx_blocks) % lat_per)
    per = jnp.where(b < n_ctx_blocks, ctx_per, lat_per)
    first = in_seq == 0
    last = in_seq == per - 1
    x = x_ref[...].astype(F32)
    halo = hp_ref.shape[0]
    prev_row = jnp.where(first, 0.0, hp_ref[...].astype(F32)[halo - 1:halo, :])
    next_row = jnp.where(last, 0.0, hn_ref[...].astype(F32)[0:1, :])
    r = lax.broadcasted_iota(jnp.int32, (SUBLANES, x.shape[1]), 0)
    down = pltpu.roll(x, 1, 0)
    up = pltpu.roll(x, rows - 1, 0)
    x_m1 = jnp.concatenate([jnp.where(r == 0, prev_row, down[:SUBLANES]), down[SUBLANES:]], axis=0)
    x_p1 = jnp.concatenate([up[:rows - SUBLANES], jnp.where(r == SUBLANES - 1, next_row, up[rows - SUBLANES:])],
                           axis=0)
    act = _silu(cw_ref[0:1, :] * x_m1 + cw_ref[1:2, :] * x + cw_ref[2:3, :] * x_p1 + cb_ref[...])
    act_bf = act.astype(BF16)
    act_ref[...] = act_bf
    z = dtr_ref[...] + dtb_ref[...]
    dt_all = jnp.maximum(z, 0.0) + jnp.log1p(jnp.exp(-jnp.abs(z)))
    dt_ref[...] = dt_all
    a_row = -jnp.exp(alog_ref[...])
    for s in range(rows // C_CHUNK):
        tok = slice(s * C_CHUNK, (s + 1) * C_CHUNK)
        dt = dt_all[tok]
        cs, tot = _chunk_cumsums(dt, a_row)
        w = jnp.exp(tot - cs) * dt
        xs = act_bf[tok, :C_INNER].astype(F32)
        bmat_t = jnp.transpose(act_bf[tok, C_INNER:C_INNER + LANES].astype(F32))
        for d in range(2):
            e = _head_expand(d)
            xw = (xs * _dot2_lhs(w, e)).astype(BF16)
            dec_ref[s, d] = jnp.exp(_dot3_lhs(tot, e))
            for g in range(C_GROUPS):
                bt = bmat_t[g * C_STATE:(g + 1) * C_STATE, :].astype(BF16)
                cols = slice(g * C_REP * C_HEAD_DIM, (g + 1) * C_REP * C_HEAD_DIM)
                st_ref[s, d, :, cols] = jnp.dot(bt, xw[:, cols], preferred_element_type=F32)


def _ssd_local(xbc, dt_raw, conv_w, conv_b, dt_bias, a_log, n_ctx_rows):
    t_rows = xbc.shape[0]
    rows = SSD_STEP_ROWS
    n_sub = rows // C_CHUNK
    nb = t_rows // rows
    nc = t_rows // C_CHUNK
    halo = 16
    per = rows // halo
    pad = lambda v: jnp.pad(v.reshape(1, -1), ((0, 0), (0, LANES - v.size)))
    kern = functools.partial(_ssd_local_kernel, n_ctx_blocks=n_ctx_rows // rows)
    row = lambda w: pl.BlockSpec((rows, w), lambda c: (c, 0))
    const = lambda shape: pl.BlockSpec(shape, lambda c: (0,) * len(shape))
    return pl.pallas_call(
        kern, grid=(nb,),
        in_specs=[row(C_XBC),
                  pl.BlockSpec((halo, C_XBC), lambda c: (jnp.maximum(c * per - 1, 0), 0)),
                  pl.BlockSpec((halo, C_XBC), lambda c: (jnp.minimum((c + 1) * per, nb * per - 1), 0)),
                  row(LANES), const((3, C_XBC)), const((1, C_XBC)), const((1, LANES)), const((1, LANES))],
        out_specs=[row(C_XBC), row(LANES),
                   pl.BlockSpec((n_sub, 2, C_STATE, C_INNER), lambda c: (c, 0, 0, 0)),
                   pl.BlockSpec((n_sub, 2, 1, C_INNER), lambda c: (c, 0, 0, 0))],
        out_shape=[jax.ShapeDtypeStruct((t_rows, C_XBC), BF16),
                   jax.ShapeDtypeStruct((t_rows, LANES), F32),
                   jax.ShapeDtypeStruct((nc, 2, C_STATE, C_INNER), F32),
                   jax.ShapeDtypeStruct((nc, 2, 1, C_INNER), F32)],
        compiler_params=_cparams(("parallel",)), name="ssd_local",
    )(xbc, xbc, xbc, dt_raw, conv_w, conv_b.reshape(1, C_XBC), pad(dt_bias), pad(a_log))


def _ssd_scan_kernel(h0_ref, st_ref, dec_ref, *refs, n_chunks):
    hin_ref, hfin_ref = refs[-2:]
    d = pl.program_id(1)

    def run(order):
        for j in range(C_INNER // LANES):
            cols = slice(j * LANES, (j + 1) * LANES)
            h = h0_ref[0, 0, :, cols]
            for c in order:
                hin_ref[c, 0, :, cols] = h
                h = h * dec_ref[c, 0, :, cols] + st_ref[c, 0, :, cols]
            hfin_ref[0, 0, :, cols] = h

    @pl.when(d == 0)
    def _():
        run(range(n_chunks))

    @pl.when(d == 1)
    def _():
        run(range(n_chunks - 1, -1, -1))


def _ssd_scan(h0, st, dec, first_chunk, n_seq, n_chunks):
    assert first_chunk % n_chunks == 0
    base = first_chunk // n_chunks
    blk = (n_chunks, 1, C_STATE, C_INNER)
    one = (1, 1, C_STATE, C_INNER)
    return pl.pallas_call(
        functools.partial(_ssd_scan_kernel, n_chunks=n_chunks),
        grid=(n_seq, 2),
        in_specs=[pl.BlockSpec(one, lambda s, d: (s, d, 0, 0)),
                  pl.BlockSpec(blk, lambda s, d: (base + s, d, 0, 0)),
                  pl.BlockSpec((n_chunks, 1, 1, C_INNER), lambda s, d: (base + s, d, 0, 0))],
        out_specs=[pl.BlockSpec(blk, lambda s, d: (s, d, 0, 0)),
                   pl.BlockSpec(one, lambda s, d: (s, d, 0, 0))],
        out_shape=[jax.ShapeDtypeStruct((n_seq * n_chunks, 2, C_STATE, C_INNER), F32),
                   jax.ShapeDtypeStruct((n_seq, 2, C_STATE, C_INNER), F32)],
        compiler_params=_cparams(("parallel", "parallel")), name="ssd_scan",
    )(h0, st, dec)


def _ssd_out_chunk(act, dt, hin, z, a_row, dskip, norm_g, side_work=()):
    side_work = iter(side_work)
    cs, _ = _chunk_cumsums(dt, a_row)
    cs2 = cs * LOG2_E
    cl_t = jnp.transpose(cs2 - jnp.log2(dt))
    x_bf = act[:, :C_INNER]
    bm = act[:, C_INNER:C_INNER + C_GROUPS * C_STATE]
    cm = act[:, C_INNER + C_GROUPS * C_STATE:]
    tril = _tri(True)
    triu = _tri(False)
    lane = lax.broadcasted_iota(jnp.int32, (C_CHUNK, LANES), 1)
    y_parts = []
    for g in range(C_GROUPS):
        ns = slice(g * C_STATE, (g + 1) * C_STATE)
        cb = lax.dot_general(cm[:, ns], bm[:, ns], (((1,), (1,)), ((), ())), preferred_element_type=F32)
        for hp in range(C_REP // 2):
            pair = []
            for h in (g * C_REP + 2 * hp, g * C_REP + 2 * hp + 1):
                hb = h + C_HEADS
                lf = jnp.exp2(jnp.where(tril, cs2[:, h:h + 1] - cl_t[h:h + 1, :], NEG_BIG))
                lb = jnp.exp2(jnp.where(triu, cs2[:, hb:hb + 1] - cl_t[hb:hb + 1, :], NEG_BIG))
                pair.append((cb * (lf + lb)).astype(BF16))
            cols = slice((g * C_REP + 2 * hp) * C_HEAD_DIM, (g * C_REP + 2 * hp + 2) * C_HEAD_DIM)
            xp = x_bf[:, cols]
            y0 = jnp.dot(pair[0], xp, preferred_element_type=F32)
            y1 = jnp.dot(pair[1], xp, preferred_element_type=F32)
            y_parts.append(jnp.where(lane < C_HEAD_DIM, y0, y1))
            for thunk in itertools.islice(side_work, 1):
                thunk()
    y = jnp.concatenate(y_parts, axis=1)
    grow_small = jnp.exp2(cs2)
    for d in range(2):
        grow = _dot2_lhs(grow_small, _head_expand(d))
        yo = []
        for g in range(C_GROUPS):
            ns = slice(g * C_STATE, (g + 1) * C_STATE)
            cols = slice(g * C_REP * C_HEAD_DIM, (g + 1) * C_REP * C_HEAD_DIM)
            yo.append(jnp.dot(cm[:, ns], hin(d, cols).astype(BF16), preferred_element_type=F32))
        y = y + jnp.concatenate(yo, axis=1) * grow
    y = y + dskip * x_bf.astype(F32)
    o = y * _silu(z.astype(F32))
    return o * lax.rsqrt(jnp.mean(o * o, axis=-1, keepdims=True) + EPS) * norm_g


def _state_to_rows(h):
    b = h.shape[0]
    return jnp.transpose(h, (0, 1, 4, 2, 3)).reshape(b, 2, C_STATE, C_INNER)


def _rows_to_state(h):
    b = h.shape[0]
    return jnp.transpose(h.reshape(b, 2, C_STATE, C_HEADS, C_HEAD_DIM), (0, 1, 3, 4, 2))


def _ssd_merge_kernel(act_ref, dt_ref, hin_ctx_ref, hin_lat_ref, z_ref, alog_ref, dskip_ref, ng_ref,
                      a0c_ref, a0l_ref, a1_ref, w_ref, g0_ref, g1_ref, g2_ref, o_ref, *, n_ctx_blocks):
    is_ctx = pl.program_id(0) < n_ctx_blocks
    a0 = jnp.where(is_ctx, a0c_ref[...], a0l_ref[...])
    n_sub = act_ref.shape[0] // C_CHUNK
    width = 2 * LANES
    n_pieces = D_MODEL // width
    assert 2 * n_pieces == n_sub * C_HEADS // 2
    acc = [None] * n_pieces

    def branch_piece(p):
        cols = slice(p * width, (p + 1) * width)
        acc[p] = (g0_ref[:, cols].astype(F32) * jnp.dot(a0, w_ref[0, :, cols], preferred_element_type=F32)
                  + g1_ref[:, cols].astype(F32) * jnp.dot(a1_ref[...], w_ref[1, :, cols],
                                                          preferred_element_type=F32))

    side = iter([f for p in range(n_pieces) for f in (functools.partial(branch_piece, p), lambda: None)])
    a_row = -jnp.exp(alog_ref[...])
    o_c = []
    for s in range(n_sub):
        tok = slice(s * C_CHUNK, (s + 1) * C_CHUNK)
        hin = lambda d, cols, s=s: jnp.where(is_ctx, hin_ctx_ref[s, d, :, cols], hin_lat_ref[s, d, :, cols])
        o_c.append(_ssd_out_chunk(act_ref[tok, :], dt_ref[tok, :], hin, z_ref[tok, :], a_row,
                                  dskip_ref[...], ng_ref[...], side_work=side).astype(BF16))
    o_c = jnp.concatenate(o_c, axis=0)
    for p in range(n_pieces):
        cols = slice(p * width, (p + 1) * width)
        o_ref[:, cols] = (acc[p] + g2_ref[:, cols].astype(F32)
                          * jnp.dot(o_c, w_ref[2, :, cols], preferred_element_type=F32)).astype(o_ref.dtype)


def _ssd_merge(act, dt, hin_ctx, hin_lat, z, a_log, d_skip, norm_g, o_a_ctx, o_a_lat, o_b, w_br, gates,
               n_ctx_rows):
    t_rows = act.shape[0]
    rows = SSD_STEP_ROWS
    ncb = n_ctx_rows // rows
    pad = lambda v: jnp.pad(v.reshape(1, -1), ((0, 0), (0, LANES - v.size)))
    row = lambda w, col=0: pl.BlockSpec((rows, w), lambda c: (c, col))
    const = lambda shape, **kw: pl.BlockSpec(shape, lambda c: (0,) * len(shape), **kw)
    hin_blk = (rows // C_CHUNK, 2, C_STATE, C_INNER)
    d_exp = jnp.repeat(d_skip, C_HEAD_DIM).reshape(1, C_INNER)
    return pl.pallas_call(
        functools.partial(_ssd_merge_kernel, n_ctx_blocks=ncb), grid=(t_rows // rows,),
        in_specs=[row(C_XBC), row(LANES),
                  pl.BlockSpec(hin_blk, lambda c: (jnp.minimum(c, ncb - 1), 0, 0, 0)),
                  pl.BlockSpec(hin_blk, lambda c: (jnp.maximum(c - ncb, 0), 0, 0, 0)),
                  row(C_INNER), const((1, LANES)), const((1, C_INNER)), const((1, C_INNER))]
                 + _pair_specs((rows, Q_W), ncb)
                 + [row(B_WIDTH), const((N_BRANCH, C_INNER, D_MODEL), pipeline_mode=pl.Buffered(1)),
                    row(D_MODEL, 0), row(D_MODEL, 1), row(D_MODEL, 2)],
        out_specs=row(D_MODEL),
        out_shape=jax.ShapeDtypeStruct((t_rows, D_MODEL), BF16),
        compiler_params=_cparams(("parallel",)), name="ssd_merge",
    )(act, dt, hin_ctx, hin_lat, z, pad(a_log), d_exp, norm_g.reshape(1, C_INNER),
      o_a_ctx, o_a_lat, o_b, w_br, gates, gates, gates)


def _out_proj_kernel(a_ref, w_ref, *refs, split_x, n_ctx_tiles):
    refs = list(refs)
    if split_x:
        x = _pick_rows(pl.program_id(0), n_ctx_tiles, refs.pop(0), refs.pop(0))
    else:
        x = refs.pop(0)[...]
    gate_ref, g_ref, sc_ref, sh_ref, wr_ref, br_ref, xo_ref, h_ref, meta_ref = refs
    tm = x.shape[0]
    half = tm // 2
    n_p = D_MODEL // PROJ_CHUNK
    half_w = D_MODEL // 2

    def matmul_pieces(rows, acc):
        a = a_ref[rows, :]

        def piece(p):
            acc[p] = jnp.dot(a, w_ref[:, p * PROJ_CHUNK:(p + 1) * PROJ_CHUNK], preferred_element_type=F32)
        return [functools.partial(piece, p) for p in range(n_p)]

    def tail_pieces(r, acc):
        rows = slice(r * half, (r + 1) * half)
        xs, hs = [None] * n_p, [None] * (2 * SUBLANES)
        inv = []

        def resid(p):
            cols = slice(p * PROJ_CHUNK, (p + 1) * PROJ_CHUNK)
            xs[p] = x[rows, cols] + gate_ref[0][:, cols] * acc[p]
            xo_ref[rows, cols] = xs[p]

        def norm(k):
            if not inv:
                ss = sum(jnp.sum(xp * xp, axis=-1, keepdims=True) for xp in xs)
                inv.append(lax.rsqrt(ss * (1.0 / D_MODEL) + EPS))
            pair = []
            for base in (0, half_w):
                c0 = base + k * LANES
                cols = slice(c0, c0 + LANES)
                xk = xs[c0 // PROJ_CHUNK][:, c0 % PROJ_CHUNK:c0 % PROJ_CHUNK + LANES]
                hk = (xk * inv[0] * g_ref[:, cols] * sc_ref[0][:, cols] + sh_ref[0][:, cols]).astype(BF16)
                hs[(0 if base == 0 else SUBLANES) + k] = hk
                pair.append(hk)
            packed = _pack_bf16_pairs(jnp.concatenate(pair, axis=1))
            h_ref[pl.ds(r * half * SUBLANES + k, half, stride=SUBLANES), :] = packed

        def route():
            logits = jnp.dot(jnp.concatenate(hs, axis=1), wr_ref[...], preferred_element_type=F32) + br_ref[...]
            meta_ref[rows, :] = _route(logits)

        return ([functools.partial(resid, p) for p in range(n_p)]
                + [functools.partial(norm, k) for k in range(SUBLANES)] + [route])

    acc0, acc1 = [None] * n_p, [None] * n_p
    for piece in matmul_pieces(slice(0, half), acc0):
        piece()
    tail0 = iter(tail_pieces(0, acc0))
    for piece in matmul_pieces(slice(half, tm), acc1):
        piece()
        for t in itertools.islice(tail0, 2):
            t()
    for t in tail0:
        t()
    for t in tail_pieces(1, acc1):
        t()


def _out_proj(merged, w_o, x, gate, norm_g, mod, router, n_ctx_rows):
    assert D_MODEL // 2 == SUBLANES * LANES
    t_rows = merged.shape[0]
    tm = _row_tile(n_ctx_rows, 512)
    nct = n_ctx_rows // tm
    split_x = isinstance(x, tuple)
    if split_x:
        x_args, x_specs = list(x), _pair_specs((tm, D_MODEL), nct)
    else:
        x_args, x_specs = [x], [pl.BlockSpec((tm, D_MODEL), lambda i: (i, 0))]
    vec_spec = pl.BlockSpec((1, 1, D_MODEL), lambda i: (_group_of_tile(i, tm, n_ctx_rows), 0, 0))
    const = lambda shape, **kw: pl.BlockSpec(shape, lambda i: (0,) * len(shape), **kw)
    return pl.pallas_call(
        functools.partial(_out_proj_kernel, split_x=split_x, n_ctx_tiles=nct),
        grid=(t_rows // tm,),
        in_specs=[pl.BlockSpec((tm, D_MODEL), lambda i: (i, 0)),
                  const((D_MODEL, D_MODEL), pipeline_mode=pl.Buffered(1))] + x_specs + [
                  vec_spec, const((1, D_MODEL)), vec_spec, vec_spec, const((D_MODEL, LANES)), const((1, LANES))],
        out_specs=[pl.BlockSpec((tm, D_MODEL), lambda i: (i, 0)),
                   pl.BlockSpec((tm * SUBLANES, LANES), lambda i: (i, 0)),
                   pl.BlockSpec((tm, LANES), lambda i: (i, 0))],
        out_shape=[jax.ShapeDtypeStruct((t_rows, D_MODEL), F32),
                   jax.ShapeDtypeStruct((t_rows * SUBLANES, LANES), jnp.uint32),
                   jax.ShapeDtypeStruct((t_rows, LANES), F32)],
        compiler_params=_cparams(("parallel",)), name="out_proj",
    )(merged, w_o, *x_args, gate, norm_g.reshape(1, D_MODEL), mod[0], mod[1], router[0], router[1])


def _moe_kernel(src_ref, ea_ref, eb_ref, nv_ref, h_hbm, meta_ref, ga_ref, ua_ref, da_ref, gb_ref, ub_ref, db_ref,
                o_ref, xbuf, sem):
    t = pl.program_id(0)
    tm = xbuf.shape[1] // SUBLANES
    slot = t % 2

    def gather(tile, dst_slot):
        for r in range(tm):
            row = pl.multiple_of(src_ref[tile * tm + r], SUBLANES)
            pltpu.make_async_copy(h_hbm.at[pl.ds(row, SUBLANES), :],
                                  xbuf.at[dst_slot, pl.ds(r * SUBLANES, SUBLANES), :],
                                  sem.at[dst_slot]).start()

    @pl.when(t == 0)
    def _():
        gather(0, 0)

    @pl.when(t + 1 < nv_ref[0])
    def _():
        gather(t + 1, 1 - slot)

    @pl.when(t < nv_ref[0])
    def _():
        pltpu.make_async_copy(h_hbm.at[pl.ds(0, tm * SUBLANES), :], xbuf.at[slot], sem.at[slot]).wait()
        x = _unpack_bf16_pairs(jnp.concatenate(
            [xbuf[slot, pl.ds(k, tm, stride=SUBLANES), :] for k in range(SUBLANES)], axis=1))

        def expert(g_ref, u_ref, d_ref, wcol):
            g = jnp.dot(x, g_ref[0, 0], preferred_element_type=F32)
            u = jnp.dot(x, u_ref[0, 0], preferred_element_type=F32)
            he = _silu(g) * u
            return wcol * jnp.dot(he.astype(BF16), d_ref[0, 0], preferred_element_type=F32)

        o = (expert(ga_ref, ua_ref, da_ref, meta_ref[:, 1:2])
             + expert(gb_ref, ub_ref, db_ref, meta_ref[:, 2:3]))
        o_ref[...] = o.astype(o_ref.dtype)

    @pl.when(t >= nv_ref[0])
    def _():
        o_ref[...] = jnp.zeros(o_ref.shape, o_ref.dtype)


def _moe_tile(n_rows):
    return 256 if n_rows >= 8192 else 128


def _moe_plan(bucket, tm):
    t_rows = bucket.shape[0]
    n_tiles = t_rows // tm + MOE_BUCKETS
    iota = jnp.arange(t_rows, dtype=jnp.int32)
    sorted_bucket, order = lax.sort((bucket, iota), num_keys=1)
    counts = jnp.sum((bucket[:, None] == jnp.arange(MOE_BUCKETS, dtype=jnp.int32)[None, :]).astype(jnp.int32), axis=0)
    padded = ((counts + tm - 1) // tm) * tm
    pad_end = jnp.cumsum(padded)
    pad_off = pad_end - padded
    off = jnp.cumsum(counts) - counts
    dest = pad_off[sorted_bucket] + iota - off[sorted_bucket]
    _, pos = lax.sort((order, dest), num_keys=1)
    tile_start = jnp.arange(n_tiles, dtype=jnp.int32) * tm
    tile_bucket = jnp.sum((tile_start[:, None] >= pad_end[None, :]).astype(jnp.int32), axis=1)
    last_bucket = jnp.max(jnp.where(counts > 0, jnp.arange(MOE_BUCKETS, dtype=jnp.int32), 0))
    tile_bucket = jnp.where(tile_start < pad_end[-1], tile_bucket, last_bucket)
    n_valid = (pad_end[-1] // tm).astype(jnp.int32).reshape(1)
    row_bucket = jnp.repeat(tile_bucket, tm)
    q = jnp.arange(n_tiles * tm, dtype=jnp.int32) - pad_off[row_bucket]
    src = order[jnp.clip(off[row_bucket] + q, 0, t_rows - 1)]
    pair_lo = jnp.asarray(np.array([0, 0, 0, 1, 1, 2], np.int32))
    pair_hi = jnp.asarray(np.array([1, 2, 3, 2, 3, 3], np.int32))
    grp = tile_bucket // MOE_PAIRS
    ea = grp * MOE_PER_GROUP + pair_lo[tile_bucket % MOE_PAIRS]
    eb = grp * MOE_PER_GROUP + pair_hi[tile_bucket % MOE_PAIRS]
    return src, pos, ea, eb, n_valid


def _moe(h2, meta, w_gate, w_up, w_down, layer):
    t_rows = meta.shape[0]
    tm = _moe_tile(t_rows)
    src, pos, ea, eb, n_valid = _moe_plan(meta[:, 0].astype(jnp.int32), tm)
    n_tiles = ea.shape[0]
    meta_sorted = meta.at[src].get(mode="promise_in_bounds")
    up_spec = lambda sel: pl.BlockSpec((1, 1, D_MODEL, MOE_HIDDEN), sel)
    dn_spec = lambda sel: pl.BlockSpec((1, 1, MOE_HIDDEN, D_MODEL), sel)
    sel_a = lambda t, src, ea, eb, nv: (layer, ea[t], 0, 0)
    sel_b = lambda t, src, ea, eb, nv: (layer, eb[t], 0, 0)
    y_sorted = pl.pallas_call(
        _moe_kernel,
        grid_spec=pltpu.PrefetchScalarGridSpec(
            num_scalar_prefetch=4, grid=(n_tiles,),
            in_specs=[pl.BlockSpec(memory_space=pl.ANY),
                      pl.BlockSpec((tm, LANES), lambda t, src, ea, eb, nv: (t, 0)),
                      up_spec(sel_a), up_spec(sel_a), dn_spec(sel_a),
                      up_spec(sel_b), up_spec(sel_b), dn_spec(sel_b)],
            out_specs=pl.BlockSpec((tm, D_MODEL), lambda t, src, ea, eb, nv: (t, 0)),
            scratch_shapes=[pltpu.VMEM((2, tm * SUBLANES, LANES), jnp.uint32), pltpu.SemaphoreType.DMA((2,))]),
        out_shape=jax.ShapeDtypeStruct((n_tiles * tm, D_MODEL), BF16),
        compiler_params=_cparams(("arbitrary",)), name="moe",
    )(src * SUBLANES, ea, eb, n_valid, h2, meta_sorted, w_gate, w_up, w_down, w_gate, w_up, w_down)
    return y_sorted.at[pos].get(mode="promise_in_bounds")


def _vec(mod_l, k):
    return mod_l[:, k * D_MODEL:(k + 1) * D_MODEL].reshape(-1, 1, D_MODEL)


def kernel(x_prompt, x_sample, cache_k, cache_v, state_ssm, c, c_ctx, w_ada, b_ada, norm1_g, w_in, attn_sink,
           gm_w, gm_b, conv_w, conv_b, dt_bias, a_log, d_skip, ssm_norm_g, w_br, w_o, norm2_g, w_rg, b_rg,
           w_re, b_re, w_gate, w_up, w_down, final_g):
    depth = w_in.shape[0]
    n_ctx, n_lat = x_prompt.shape[0], x_sample.shape[0]
    assert x_prompt.shape[1:] == (SEQ, D_MODEL) and x_sample.shape[1:] == (DEC_SEQ, D_MODEL)
    n_ctx_rows = n_ctx * SEQ
    n_lat_rows = n_lat * DEC_SEQ
    x_parts = (x_prompt.reshape(n_ctx_rows, D_MODEL), x_sample.reshape(n_lat_rows, D_MODEL))
    x = None
    bf = lambda a: a.astype(BF16)
    w_gate_bf, w_up_bf, w_down_bf = bf(w_gate), bf(w_up), bf(w_down)

    n_cond = 1 + n_lat
    cond = jnp.zeros((((n_cond + 7) // 8) * 8, D_MODEL), F32).at[0].set(c_ctx).at[1:n_cond].set(c)
    mod = _adaln(cond, w_ada, b_ada)
    rope = _rope_tables()
    head_cols = (np.arange(A_HEADS + A_KV_HEADS)[:, None] * A_HEAD_DIM + _rope_head_perm()[None, :]).reshape(-1)
    qkv_cols = np.concatenate([head_cols, np.arange(Q_W + KV_W, Q_W + 2 * KV_W)])

    ks, vs, hs = [], [], []
    pending = None
    for l in range(depth):
        shift1, scale1, gate1, shift2, scale2, gate2 = (_vec(mod[l], k) for k in range(6))
        w = w_in[l]
        if x is None:
            (h,) = _norm_stage(x_parts, norm1_g[l], n_ctx_rows, mod=(1.0 + scale1, shift1))
        else:
            x, h = _norm_stage(x, norm1_g[l], n_ctx_rows, res=pending, mod=(1.0 + scale1, shift1), write_x=True)
        qkv = _proj_qkv(h, bf(w[:, OFF_Q:OFF_GU][:, qkv_cols]), rope, n_ctx_rows)
        kv32 = _proj(h, bf(w[:, OFF_K:OFF_GU]), n_ctx_rows, out_dtype=F32, row_range=(0, n_ctx_rows))
        uv = _proj(h, bf(w[:, OFF_GU:OFF_CZ]), n_ctx_rows)
        cz = _proj(h, bf(w[:, OFF_CZ:OFF_XBC]), n_ctx_rows)
        w_xbc_dt = jnp.pad(w[:, OFF_XBC:OFF_GATES], ((0, 0), (0, LANES - 2 * C_HEADS)))
        xbc, dt_raw = _proj_split(h, bf(w_xbc_dt), C_XBC, (BF16, F32), n_ctx_rows)
        gates = _proj(h, bf(w[:, OFF_GATES:]), n_ctx_rows, kind="sigmoid", tm_cap=512)

        k_ctx = bf(cache_k[:, l].reshape(n_lat, -1, KV_W)[:, :, qkv_cols[:KV_W]])
        v_ctx = bf(cache_v[:, l].reshape(n_lat, -1, KV_W))
        o_a_lat = _lat_attention(qkv, k_ctx, v_ctx, attn_sink[l], n_ctx_rows, n_lat)
        o_a_ctx = _ctx_attention(qkv, attn_sink[l], n_ctx)
        ks.append(kv32[:, :KV_W].reshape(n_ctx, SEQ, A_KV_HEADS, A_HEAD_DIM))
        vs.append(kv32[:, KV_W:].reshape(n_ctx, SEQ, A_KV_HEADS, A_HEAD_DIM))
        o_b = _gmlp(uv, gm_w[l], gm_b[l], n_ctx_rows)
        act, dt, st, dec = _ssd_local(xbc, dt_raw, conv_w[l], conv_b[l], dt_bias[l], a_log[l], n_ctx_rows)
        ncc = n_ctx_rows // C_CHUNK
        hin_l, _ = _ssd_scan(_state_to_rows(state_ssm[:, l]), st, dec, ncc, n_lat, DEC_SEQ // C_CHUNK)
        hin_c, hfin_c = _ssd_scan(jnp.zeros((n_ctx, 2, C_STATE, C_INNER), F32), st, dec, 0, n_ctx, SEQ // C_CHUNK)
        hs.append(_rows_to_state(hfin_c))
        merged = _ssd_merge(act, dt, hin_c, hin_l, cz, a_log[l], d_skip[l], ssm_norm_g[l],
                            o_a_ctx, o_a_lat, o_b, bf(w_br[l]), gates, n_ctx_rows)
        w_r = bf(jnp.pad(jnp.concatenate([w_rg[l], w_re[l]], axis=1),
                         ((0, 0), (0, LANES - MOE_GROUPS - MOE_EXPERTS))))
        b_r = jnp.pad(jnp.concatenate([b_rg[l], b_re[l]]), (0, LANES - MOE_GROUPS - MOE_EXPERTS)).reshape(1, LANES)
        x, h2, meta = _out_proj(merged, bf(w_o[l]), x_parts if x is None else x, gate1, norm2_g[l],
                                (1.0 + scale2, shift2), (w_r, b_r), n_ctx_rows)
        y_moe = _moe(h2, meta, w_gate_bf, w_up_bf, w_down_bf, l)
        pending = (y_moe, gate2)

    (y_prompt,) = _norm_stage(x, final_g, n_ctx_rows, n_rows=n_ctx_rows, res=pending, out_dtype=F32)
    (y_sample,) = _norm_stage(x, final_g, n_ctx_rows, first=n_ctx_rows, n_rows=n_lat_rows, res=pending,
                              out_dtype=F32)
    return (y_prompt.reshape(n_ctx, SEQ, D_MODEL), y_sample.reshape(n_lat, DEC_SEQ, D_MODEL),
            jnp.stack(ks, axis=1), jnp.stack(vs, axis=1), jnp.stack(hs, axis=1))
```
